```python
import jax
import jax.numpy as jnp
from jax import lax
import numpy as np

D_MODEL = 1024
BATCH = 8
SEQ = 2048
DEPTH = 2

BRANCH_W = D_MODEL // 2
N_BRANCH = 3
SB_HDIM = 64
SB_HEADS = BRANCH_W // SB_HDIM
SB_QBLOCK = 128
MB_HDIM = 64
MB_HEADS = BRANCH_W // MB_HDIM
MB_BLOCK = 256
MB_TOPK = 3
MB_QCHUNK = 16
ROPE_THETA = 500000.0
ROPE_DIM = MB_HDIM // 4
ML_HEADS = 4
ML_HDIM = BRANCH_W // ML_HEADS
ML_CHUNK = 64
CONV_W = 4
N_GROUPS = 4
EXPERTS_PER_GROUP = 8
N_EXPERTS = N_GROUPS * EXPERTS_PER_GROUP
EXPERT_TOPK = 2
D_EXPERT = D_MODEL // 4
RMS_EPS = 1e-6
P_IN = 10 * BRANCH_W + 2 * ML_HEADS + N_BRANCH * D_MODEL

kernel_name = "hybrid_sb_moba_mlstm_hmoe"


def rms_norm(x, g):
    xf = x.astype(jnp.float32)
    y = xf * lax.rsqrt(jnp.mean(xf * xf, axis=-1, keepdims=True) + RMS_EPS)
    return (y * g.astype(jnp.float32)).astype(x.dtype)


def split_heads(t, n_heads):
    b, s, _ = t.shape
    return t.reshape(b, s, n_heads, -1).transpose(0, 2, 1, 3)


def merge_heads(t):
    b, h, s, d = t.shape
    return t.transpose(0, 2, 1, 3).reshape(b, s, h * d)


def rotary_partial(x, positions):
    half = ROPE_DIM // 2
    inv_freq = jnp.power(jnp.float32(ROPE_THETA), -jnp.arange(half, dtype=jnp.float32) / half)
    ang = positions.astype(jnp.float32)[:, None, :, None] * inv_freq
    cos, sin = jnp.cos(ang), jnp.sin(ang)
    xr = x[..., :ROPE_DIM].astype(jnp.float32)
    x1, x2 = xr[..., :half], xr[..., half:]
    rot = jnp.concatenate([x1 * cos - x2 * sin, x1 * sin + x2 * cos], axis=-1).astype(x.dtype)
    return jnp.concatenate([rot, x[..., ROPE_DIM:]], axis=-1)


def causal_depthwise_conv(u, w):
    c = u.shape[-1]
    return lax.conv_general_dilated(
        u, w[:, None, :].astype(u.dtype), window_strides=(1,),
        padding=[(CONV_W - 1, 0)], dimension_numbers=("NWC", "WIO", "NWC"),
        feature_group_count=c)


def stick_breaking_attention(q, k, v):
    b, h, s, d = q.shape
    nq = s // SB_QBLOCK
    scale = d ** -0.5
    kpos = jnp.arange(s)
    qb = q.reshape(b, h, nq, SB_QBLOCK, d).transpose(2, 0, 1, 3, 4)

    def block(args):
        qi, iblk = args
        qpos = iblk * SB_QBLOCK + jnp.arange(SB_QBLOCK)
        z = jnp.einsum("bhqd,bhkd->bhqk", qi, k).astype(jnp.float32) * scale
        past = kpos[None, :] < qpos[:, None]
        log_rem = jnp.where(past, jax.nn.log_sigmoid(-z), 0.0)
        after = lax.cumsum(log_rem, axis=3, reverse=True) - log_rem
        w = jnp.where(past, jnp.exp(jax.nn.log_sigmoid(z) + after), 0.0)
        return jnp.einsum("bhqk,bhkd->bhqd", w.astype(v.dtype), v)

    out = lax.map(block, (qb, jnp.arange(nq)))
    return out.transpose(1, 2, 0, 3, 4).reshape(b, h, s, d)


def moba_attention(q, k, v):
    b, h, s, d = q.shape
    nb = -(-s // MB_BLOCK)
    sp = nb * MB_BLOCK
    pad = ((0, 0), (0, 0), (0, sp - s), (0, 0))
    kb = jnp.pad(k, pad).reshape(b, h, nb, MB_BLOCK, d)
    vb = jnp.pad(v, pad).reshape(b, h, nb, MB_BLOCK, d)
    kmean = jnp.mean(kb.astype(jnp.float32), axis=3)
    topk = min(MB_TOPK, nb - 1)
    scale = d ** -0.5
    nc = s // MB_QCHUNK
    qc = q.reshape(b, h, nc, MB_QCHUNK, d).transpose(2, 0, 1, 3, 4)
    bi = jnp.arange(b)[:, None, None, None]
    hi = jnp.arange(h)[None, :, None, None]
    blk_ids = jnp.arange(nb)

    def chunk(args):
        qi, ci = args
        qpos = ci * MB_QCHUNK + jnp.arange(MB_QCHUNK)
        own = (ci * MB_QCHUNK) // MB_BLOCK
        k_own = lax.dynamic_index_in_dim(kb, own, axis=2, keepdims=False)
        v_own = lax.dynamic_index_in_dim(vb, own, axis=2, keepdims=False)
        kpos_own = own * MB_BLOCK + jnp.arange(MB_BLOCK)
        s_own = jnp.einsum("bhqd,bhkd->bhqk", qi, k_own).astype(jnp.float32) * scale
        s_own = jnp.where(kpos_own[None, :] <= qpos[:, None], s_own, -jnp.inf)
        if topk == 0:
            p = jax.nn.softmax(s_own, axis=-1).astype(v.dtype)
            return jnp.einsum("bhqk,bhkd->bhqd", p, v_own)
        gate = jnp.einsum("bhqd,bhnd->bhqn", qi.astype(jnp.float32), kmean)
        gate = jnp.where(blk_ids < own, gate, -jnp.inf)
        _, idx = lax.top_k(gate, topk)
        valid = idx < own
        k_sel = kb[bi, hi, idx]
        v_sel = vb[bi, hi, idx]
        s_sel = jnp.einsum("bhqd,bhqnkd->bhqnk", qi, k_sel).astype(jnp.float32) * scale
        s_sel = jnp.where(valid[..., None], s_sel, -jnp.inf)
        s_all = jnp.concatenate([s_own, s_sel.reshape(b, h, MB_QCHUNK, topk * MB_BLOCK)], axis=-1)
        p = jax.nn.softmax(s_all, axis=-1).astype(v.dtype)
        p_own = p[..., :MB_BLOCK]
        p_sel = p[..., MB_BLOCK:].reshape(b, h, MB_QCHUNK, topk, MB_BLOCK)
        return (jnp.einsum("bhqk,bhkd->bhqd", p_own, v_own)
                + jnp.einsum("bhqnk,bhqnkd->bhqd", p_sel, v_sel))

    out = lax.map(chunk, (qc, jnp.arange(nc)))
    return out.transpose(1, 2, 0, 3, 4).reshape(b, h, s, d)


def mlstm_chunkwise(q, k, v, log_i, log_f):
    b, h, s, d = q.shape
    L = ML_CHUNK
    nc = s // L
    k = k * d ** -0.5

    def to_chunks(a):
        return jnp.moveaxis(a.reshape(b, h, nc, L, *a.shape[3:]), 2, 0)

    causal = jnp.tril(jnp.ones((L, L), dtype=bool))

    def step(carry, inp):
        C, n, m = carry
        qc, kc, vc, li, lf = inp
        qf, kf, vf = qc.astype(jnp.float32), kc.astype(jnp.float32), vc.astype(jnp.float32)
        bcum = jnp.cumsum(lf, axis=-1)
        dmat = bcum[..., :, None] - bcum[..., None, :] + li[..., None, :]
        dmat = jnp.where(causal, dmat, -jnp.inf)
        inter = bcum + m[..., None]
        m_t = jnp.maximum(inter, jnp.max(dmat, axis=-1))
        w_intra = jnp.exp(dmat - m_t[..., None])
        w_inter = jnp.exp(inter - m_t)
        sc = jnp.einsum("bhtd,bhsd->bhts", qf, kf) * w_intra
        num = (jnp.einsum("bhts,bhse->bhte", sc, vf)
               + w_inter[..., None] * jnp.einsum("bhed,bhtd->bhte", C, qf))
        den = jnp.sum(sc, axis=-1) + w_inter * jnp.einsum("bhd,bhtd->bht", n, qf)
        out = num / jnp.maximum(jnp.abs(den), jnp.exp(-m_t))[..., None]
        m_new = m_t[..., -1]
        decay = jnp.exp(bcum[..., -1] + m - m_new)
        w_k = jnp.exp(bcum[..., -1:] - bcum + li - m_new[..., None])
        C_new = decay[..., None, None] * C + jnp.einsum("bhs,bhse,bhsd->bhed", w_k, vf, kf)
        n_new = decay[..., None] * n + jnp.einsum("bhs,bhsd->bhd", w_k, kf)
        return (C_new, n_new, m_new), out

    init = (jnp.zeros((b, h, d, d), jnp.float32), jnp.zeros((b, h, d), jnp.float32),
            jnp.zeros((b, h), jnp.float32))
    _, hs = lax.scan(step, init, (to_chunks(q), to_chunks(k), to_chunks(v),
                                  to_chunks(log_i), to_chunks(log_f)))
    return jnp.moveaxis(hs, 0, 2).reshape(b, h, s, d).astype(v.dtype)


def mixer_block(x, positions, norm_g, w_in, qn_g, kn_g, conv_w, ml_gate_b, w_branch, w_out):
    W = BRANCH_W
    bsz, s, _ = x.shape
    xn = rms_norm(x, norm_g)
    proj = xn @ w_in
    sb = proj[..., 0:3 * W]
    mb = proj[..., 3 * W:6 * W]
    ml_qk = proj[..., 6 * W:8 * W]
    ml_v = proj[..., 8 * W:9 * W]
    ml_o = proj[..., 9 * W:10 * W]
    ml_if = proj[..., 10 * W:10 * W + 2 * ML_HEADS]
    gate_logits = proj[..., 10 * W + 2 * ML_HEADS:]

    q, k, v = (split_heads(t, SB_HEADS) for t in jnp.split(sb, 3, axis=-1))
    y_sb = merge_heads(stick_breaking_attention(q, k, v))

    q, k, v = (split_heads(t, MB_HEADS) for t in jnp.split(mb, 3, axis=-1))
    q = rotary_partial(rms_norm(q, qn_g), positions)
    k = rotary_partial(rms_norm(k, kn_g), positions)
    y_mb = merge_heads(moba_attention(q, k, v))

    qk = jax.nn.silu(causal_depthwise_conv(ml_qk, conv_w))
    q = split_heads(qk[..., :W], ML_HEADS)
    k = split_heads(qk[..., W:], ML_HEADS)
    v = split_heads(ml_v, ML_HEADS)
    if_pre = ml_if.astype(jnp.float32) + ml_gate_b.astype(jnp.float32)
    log_i = if_pre[..., :ML_HEADS].transpose(0, 2, 1)
    log_f = jax.nn.log_sigmoid(if_pre[..., ML_HEADS:]).transpose(0, 2, 1)
    y_ml = merge_heads(mlstm_chunkwise(q, k, v, log_i, log_f)) * jax.nn.sigmoid(ml_o)

    branches = jnp.stack([y_sb, y_mb, y_ml], axis=0)
    y = jnp.einsum("nbsw,nwd->nbsd", branches, w_branch)
    gates = jax.nn.sigmoid(gate_logits.reshape(bsz, s, N_BRANCH, D_MODEL))
    merged = jnp.einsum("bsnd,nbsd->bsd", gates, y)
    return x + merged @ w_out


def hier_moe_block(x, norm_g, w_rg, b_rg, w_re, b_re, w1, w3, w2):
    bsz, s, d = x.shape
    n_tok = bsz * s
    xn = rms_norm(x, norm_g).reshape(n_tok, d)
    g_prob = jax.nn.softmax((xn @ w_rg).astype(jnp.float32) + b_rg, axis=-1)
    g_w, g_idx = lax.top_k(g_prob, 1)
    e_logits = ((xn @ w_re).astype(jnp.float32) + b_re).reshape(n_tok, N_GROUPS, EXPERTS_PER_GROUP)
    e_in = jnp.take_along_axis(e_logits, g_idx[:, :, None], axis=1)[:, 0]
    e_w, e_idx = lax.top_k(jax.nn.softmax(e_in, axis=-1), EXPERT_TOPK)
    e_w = e_w / jnp.sum(e_w, axis=-1, keepdims=True) * g_w
    flat_idx = g_idx * EXPERTS_PER_GROUP + e_idx
    combine = jnp.zeros((n_tok, N_EXPERTS), jnp.float32).at[
        jnp.arange(n_tok)[:, None], flat_idx].add(e_w)
    hid = jax.nn.silu(jnp.einsum("nd,edf->nef", xn, w1)) * jnp.einsum("nd,edf->nef", xn, w3)
    hid = hid * combine.astype(hid.dtype)[:, :, None]
    y = jnp.einsum("nef,efd->nd", hid, w2)
    return x + y.reshape(bsz, s, d)


def setup_inputs(seed: int = 0) -> dict:
    key = jax.random.key(seed)
    ks = jax.random.split(key, 20)

    def nrm(k, shape, scale):
        return jax.random.normal(k, shape, jnp.float32) * scale

    x = nrm(ks[0], (BATCH, SEQ, D_MODEL), 1.0)
    start = jax.random.randint(ks[1], (BATCH,), 0, 4096, dtype=jnp.int32)
    positions = (start[:, None] + jnp.arange(SEQ, dtype=jnp.int32)[None, :]).astype(jnp.int32)
    norm1_g = 1.0 + nrm(ks[2], (DEPTH, D_MODEL), 0.02)
    w_in = nrm(ks[3], (DEPTH, D_MODEL, P_IN), D_MODEL ** -0.5)
    qn_g = 1.0 + nrm(ks[4], (DEPTH, MB_HDIM), 0.02)
    kn_g = 1.0 + nrm(ks[5], (DEPTH, MB_HDIM), 0.02)
    conv_w = nrm(ks[6], (DEPTH, CONV_W, 2 * BRANCH_W), CONV_W ** -0.5)
    i_bias = nrm(ks[7], (DEPTH, ML_HEADS), 0.1)
    f_bias = jnp.linspace(3.0, 6.0, ML_HEADS, dtype=jnp.float32)[None, :] + nrm(ks[8], (DEPTH, ML_HEADS), 0.1)
    ml_gate_b = jnp.concatenate([i_bias, f_bias], axis=-1)
    w_branch = nrm(ks[9], (DEPTH, N_BRANCH, BRANCH_W, D_MODEL), BRANCH_W ** -0.5)
    w_out = nrm(ks[10], (DEPTH, D_MODEL, D_MODEL), D_MODEL ** -0.5)
    norm2_g = 1.0 + nrm(ks[11], (DEPTH, D_MODEL), 0.02)
    w_rg = nrm(ks[12], (DEPTH, D_MODEL, N_GROUPS), D_MODEL ** -0.5)
    b_rg = nrm(ks[13], (DEPTH, N_GROUPS), 0.01)
    w_re = nrm(ks[14], (DEPTH, D_MODEL, N_EXPERTS), D_MODEL ** -0.5)
    b_re = nrm(ks[15], (DEPTH, N_EXPERTS), 0.01)
    w1 = nrm(ks[16], (DEPTH, N_EXPERTS, D_MODEL, D_EXPERT), D_MODEL ** -0.5)
    w3 = nrm(ks[17], (DEPTH, N_EXPERTS, D_MODEL, D_EXPERT), D_MODEL ** -0.5)
    w2 = nrm(ks[18], (DEPTH, N_EXPERTS, D_EXPERT, D_MODEL), D_EXPERT ** -0.5)
    return {"x": x, "positions": positions, "norm1_g": norm1_g, "w_in": w_in,
            "qn_g": qn_g, "kn_g": kn_g, "conv_w": conv_w, "ml_gate_b": ml_gate_b,
            "w_branch": w_branch, "w_out": w_out, "norm2_g": norm2_g,
            "w_rg": w_rg, "b_rg": b_rg, "w_re": w_re, "b_re": b_re,
            "w1": w1, "w3": w3, "w2": w2}


def reference(x, positions, norm1_g, w_in, qn_g, kn_g, conv_w, ml_gate_b, w_branch, w_out,
              norm2_g, w_rg, b_rg, w_re, b_re, w1, w3, w2):
    for l in range(DEPTH):
        x = mixer_block(x, positions, norm1_g[l], w_in[l], qn_g[l], kn_g[l], conv_w[l],
                        ml_gate_b[l], w_branch[l], w_out[l])
        x = hier_moe_block(x, norm2_g[l], w_rg[l], b_rg[l], w_re[l], b_re[l],
                           w1[l], w3[l], w2[l])
    return x
```

```python
import functools

import jax
import jax.numpy as jnp
from jax import lax
from jax.experimental import pallas as pl
from jax.experimental.pallas import tpu as pltpu

D_MODEL = 1024
BRANCH_W = D_MODEL // 2
N_BRANCH = 3
HEAD_DIM = 64
N_ATT_HEADS = BRANCH_W // HEAD_DIM
MB_BLOCK = 256
MB_TOPK = 3
ROPE_THETA = 500000.0
ROPE_DIM = HEAD_DIM // 4
ML_HEADS = 4
ML_HDIM = BRANCH_W // ML_HEADS
CONV_W = 4
N_GROUPS = 4
EXPERTS_PER_GROUP = 8
N_EXPERTS = N_GROUPS * EXPERTS_PER_GROUP
D_EXPERT = D_MODEL // 4
RMS_EPS = 1e-6

LANES = 128
SUBLANES = 8
VMEM_LIMIT_BYTES = 56 * 1024 * 1024

ROW_TILE = 512
COL_TILE = 1024
ATT_TILE = 256
ML_CHUNK = 256
MOE_TILE = 256

NEG_BIG = -(2.0 ** 100)

F32 = jnp.float32
BF16 = jnp.bfloat16


def _cparams(*sem):
    return pltpu.CompilerParams(dimension_semantics=sem, vmem_limit_bytes=VMEM_LIMIT_BYTES)


def _dot(a, b):
    return jnp.dot(a, b, preferred_element_type=F32)


def _dot_nt(a, b):
    return lax.dot_general(a, b, (((1,), (1,)), ((), ())), preferred_element_type=F32)


def _split3(x):
    hi = x.astype(BF16)
    r1 = x - hi.astype(F32)
    mid = r1.astype(BF16)
    lo = (r1 - mid.astype(F32)).astype(BF16)
    return hi, mid, lo


def _dot_exact_rhs(x, m):
    hi, mid, lo = _split3(x)
    return _dot(hi, m) + _dot(mid, m) + _dot(lo, m)


def _dot_exact_lhs(m, x):
    hi, mid, lo = _split3(x)
    return _dot(m, hi) + _dot(m, mid) + _dot(m, lo)


def _log_sigmoid(x):
    return jnp.minimum(x, 0.0) - jnp.log(1.0 + jnp.exp(-jnp.abs(x)))


def _sigmoid(x):
    return 1.0 / (1.0 + jnp.exp(-x))


def _norm_kernel(x_ref, g_ref, wif_ref, bif_ref, xn_ref, if_ref):
    x = x_ref[...]
    ms = jnp.mean(x * x, axis=-1, keepdims=True)
    xn = (x * lax.rsqrt(ms + RMS_EPS) * g_ref[...]).astype(BF16)
    xn_ref[...] = xn
    if_ref[...] = _dot(xn, wif_ref[...]) + bif_ref[...]


def _norm_call(x, g, wif, bif):
    n = x.shape[0]
    return pl.pallas_call(
        _norm_kernel,
        grid=(n // ROW_TILE,),
        in_specs=[
            pl.BlockSpec((ROW_TILE, D_MODEL), lambda i: (i, 0)),
            pl.BlockSpec((1, D_MODEL), lambda i: (0, 0)),
            pl.BlockSpec((D_MODEL, LANES), lambda i: (0, 0)),
            pl.BlockSpec((1, LANES), lambda i: (0, 0)),
        ],
        out_specs=[
            pl.BlockSpec((ROW_TILE, D_MODEL), lambda i: (i, 0)),
            pl.BlockSpec((ROW_TILE, LANES), lambda i: (i, 0)),
        ],
        out_shape=[
            jax.ShapeDtypeStruct((n, D_MODEL), BF16),
            jax.ShapeDtypeStruct((n, LANES), F32),
        ],
        compiler_params=_cparams("parallel"),
        name="norm_gateproj",
    )(x, g, wif, bif)


def _matmul_kernel(a_ref, w_ref, o_ref):
    o_ref[...] = _dot(a_ref[...], w_ref[...]).astype(o_ref.dtype)


def _matmul_call(a, w, out_dtype, name):
    n, k = a.shape
    c = w.shape[1]
    tm = 2 * ROW_TILE
    return pl.pallas_call(
        _matmul_kernel,
        grid=(n // tm, c // COL_TILE),
        in_specs=[
            pl.BlockSpec((tm, k), lambda i, j: (i, 0)),
            pl.BlockSpec((k, COL_TILE), lambda i, j: (0, j)),
        ],
        out_specs=pl.BlockSpec((tm, COL_TILE), lambda i, j: (i, j)),
        out_shape=jax.ShapeDtypeStruct((n, c), out_dtype),
        compiler_params=_cparams("parallel", "parallel"),
        name=name,
    )(a, w)


def _softplus(z):
    return jnp.maximum(z, 0.0) + jnp.log(1.0 + jnp.exp(-jnp.abs(z)))


def _sb_kernel(q_ref, k_ref, v_ref, u_ref, o_ref, acc_ref, carry_ref):
    t = ATT_TILE
    qi = pl.program_id(2)
    lane = lax.broadcasted_iota(jnp.int32, (1, LANES), 1)
    head_mask = (lane < HEAD_DIM, lane >= HEAD_DIM)
    q = q_ref[...].astype(F32) * (HEAD_DIM ** -0.5)
    qh = [jnp.where(m, q, 0.0).astype(BF16) for m in head_mask]
    u_ext = u_ref[...]

    acc_ref[...] = jnp.zeros_like(acc_ref)
    carry_ref[...] = jnp.zeros_like(carry_ref)

    def tile_step(kj, diagonal):
        start = pl.multiple_of(kj * t, t)
        k = k_ref[pl.ds(start, t), :]
        v = v_ref[pl.ds(start, t), :]
        if diagonal:
            row = lax.broadcasted_iota(jnp.int32, (t, t), 0)
            col = lax.broadcasted_iota(jnp.int32, (t, t), 1)
            past = col < row
        for h in range(2):
            z = _dot_nt(qh[h], k)
            sp = _softplus(z)
            if diagonal:
                sp = jnp.where(past, sp, 0.0)
            r_ext = _dot(sp.astype(BF16), u_ext)
            carry = carry_ref[h]
            r = r_ext[:, :t] + jnp.concatenate([carry] * (t // LANES), axis=1)
            a = jnp.exp(z - r)
            if diagonal:
                a = jnp.where(past, a, 0.0)
            vh = jnp.where(head_mask[h], v, jnp.zeros_like(v))
            acc_ref[...] += _dot(a.astype(BF16), vh)
            carry_ref[h] = carry + r_ext[:, t:]

    tile_step(qi, True)

    def body(it, c):
        tile_step(qi - 1 - it, False)
        return c

    lax.fori_loop(0, qi, body, 0)
    o_ref[...] = acc_ref[...].astype(o_ref.dtype)


def _sb_call(p16, u_ext, batch, seq):
    n = batch * seq
    t = ATT_TILE
    nq = seq // t
    pairs = BRANCH_W // LANES
    return pl.pallas_call(
        _sb_kernel,
        grid=(batch, pairs, nq),
        in_specs=[
            pl.BlockSpec((t, LANES), lambda b, p, i: (b * nq + i, p)),
            pl.BlockSpec((seq, LANES), lambda b, p, i: (b, pairs + p)),
            pl.BlockSpec((seq, LANES), lambda b, p, i: (b, 2 * pairs + p)),
            pl.BlockSpec((t, t + LANES), lambda b, p, i: (0, 0)),
        ],
        out_specs=pl.BlockSpec((t, LANES), lambda b, p, i: (b * nq + i, p)),
        out_shape=jax.ShapeDtypeStruct((n, BRANCH_W), BF16),
        scratch_shapes=[
            pltpu.VMEM((t, LANES), F32),
            pltpu.VMEM((2, t, LANES), F32),
        ],
        compiler_params=_cparams("parallel", "parallel", "arbitrary"),
        name="stick_breaking_attention",
    )(p16, p16, p16, u_ext)


def _moba_prep_kernel(q_ref, k_ref, cos_ref, sin_ref, qg_ref, kg_ref, seg_ref, rot_ref,
                      qa_ref, ka_ref, km_ref):
    blk = pl.program_id(1)
    lane = lax.broadcasted_iota(jnp.int32, (1, LANES), 1)
    even = lane < HEAD_DIM
    cos = cos_ref[...]
    sin = sin_ref[...]
    seg = seg_ref[...]
    rot = rot_ref[...]

    def norm_rope(x, g):
        ss = _dot_exact_rhs(x * x, seg)
        xn = x * lax.rsqrt(ss * (1.0 / HEAD_DIM) + RMS_EPS) * g
        return xn * cos + _dot_exact_rhs(xn, rot) * sin

    onehot_even = jnp.where(lane == HEAD_DIM + blk, 1.0, 0.0)
    onehot_odd = jnp.where(lane == blk, 1.0, 0.0)

    km_rows = []
    for c in range(BRANCH_W // LANES):
        sl = slice(c * LANES, (c + 1) * LANES)
        qr = norm_rope(q_ref[:, sl], qg_ref[...]) * (HEAD_DIM ** -0.5)
        kr = norm_rope(k_ref[:, sl], kg_ref[...])
        qa_ref[0, 2 * c] = jnp.where(even, qr, 0.0).astype(BF16)
        qa_ref[0, 2 * c + 1] = jnp.where(even, 0.0, qr).astype(BF16)
        ka_ref[0, 2 * c] = jnp.where(even, kr, onehot_even).astype(BF16)
        ka_ref[0, 2 * c + 1] = jnp.where(even, onehot_odd, kr).astype(BF16)
        kmean = jnp.mean(kr, axis=0, keepdims=True)
        km_rows.append(jnp.where(even, kmean, 0.0))
        km_rows.append(jnp.where(even, 0.0, kmean))
    km_ref[0, 0] = jnp.concatenate(km_rows, axis=0)


def _moba_prep_call(p32, cos_t, sin_t, qg, kg, seg, rot, batch, seq):
    t = MB_BLOCK
    nb = seq // t
    h = N_ATT_HEADS
    return pl.pallas_call(
        _moba_prep_kernel,
        grid=(batch, nb),
        in_specs=[
            pl.BlockSpec((t, BRANCH_W), lambda b, i: (b * nb + i, 0)),
            pl.BlockSpec((t, BRANCH_W), lambda b, i: (b * nb + i, 1)),
            pl.BlockSpec((t, LANES), lambda b, i: (b * nb + i, 0)),
            pl.BlockSpec((t, LANES), lambda b, i: (b * nb + i, 0)),
            pl.BlockSpec((1, LANES), lambda b, i: (0, 0)),
            pl.BlockSpec((1, LANES), lambda b, i: (0, 0)),
            pl.BlockSpec((LANES, LANES), lambda b, i: (0, 0)),
            pl.BlockSpec((LANES, LANES), lambda b, i: (0, 0)),
        ],
        out_specs=[
            pl.BlockSpec((1, h, t, LANES), lambda b, i: (b, 0, i, 0)),
            pl.BlockSpec((1, h, t, LANES), lambda b, i: (b, 0, i, 0)),
            pl.BlockSpec((1, 1, h, LANES), lambda b, i: (b, i, 0, 0)),
        ],
        out_shape=[
            jax.ShapeDtypeStruct((batch, h, seq, LANES), BF16),
            jax.ShapeDtypeStruct((batch, h, seq, LANES), BF16),
            jax.ShapeDtypeStruct((batch, nb, h, LANES), F32),
        ],
        compiler_params=_cparams("parallel", "parallel"),
        name="moba_qk_prep",
    )(p32, p32, cos_t, sin_t, qg, kg, seg, rot)


def _moba_kernel(qa_ref, ka_ref, v_ref, km_ref, eye_ref, o_ref,
                 q_scr, acc_ref, m_ref, l_ref, *, nb):
    t = ATT_TILE
    own = pl.program_id(2)
    lane = lax.broadcasted_iota(jnp.int32, (1, LANES), 1)
    head_mask = (lane < HEAD_DIM, lane >= HEAD_DIM)
    blk_row = lax.broadcasted_iota(jnp.int32, (nb, t), 0)

    for h in range(2):
        q = qa_ref[0, h]
        km_hi, km_mid, km_lo = _split3(km_ref[0, h])
        g = _dot_nt(km_hi, q) + _dot_nt(km_mid, q) + _dot_nt(km_lo, q)
        cnt = jnp.zeros((nb, t), jnp.int32)
        for m in range(nb):
            gm = g[m:m + 1, :]
            beats = (gm > g) | ((gm == g) & (m < blk_row))
            cnt = cnt + jnp.where(beats, jnp.where(m < own, 1, 0), 0)
        keep = ((blk_row < own) & (cnt < MB_TOPK)) | (blk_row == own)
        bias = jnp.where(keep, 0.0, NEG_BIG)
        aux_lo = HEAD_DIM if h == 0 else 0
        pieces = []
        if aux_lo:
            pieces.append(jnp.zeros((aux_lo, t), F32))
        pieces.append(bias)
        pieces.append(jnp.zeros((LANES - aux_lo - nb, t), F32))
        bias_rows = jnp.concatenate(pieces, axis=0).astype(BF16)
        bias_cols = _dot_nt(eye_ref[...], bias_rows)
        q_scr[h] = (q.astype(F32) + bias_cols).astype(BF16)

    acc_ref[...] = jnp.zeros_like(acc_ref)
    l_ref[...] = jnp.zeros_like(l_ref)
    m_ref[...] = jnp.full_like(m_ref, NEG_BIG)

    def tile_step(kj, diagonal):
        start = pl.multiple_of(kj * t, t)
        v = v_ref[pl.ds(start, t), :]
        if diagonal:
            row = lax.broadcasted_iota(jnp.int32, (t, t), 0)
            col = lax.broadcasted_iota(jnp.int32, (t, t), 1)
            causal = col <= row
        for h in range(2):
            k = ka_ref[0, h, pl.ds(start, t), :]
            s = _dot_nt(q_scr[h], k)
            if diagonal:
                s = jnp.where(causal, s, NEG_BIG)
            m_prev = m_ref[h]
            m_new = jnp.maximum(m_prev, jnp.max(s, axis=-1, keepdims=True))
            p = jnp.exp(s - m_new)
            alpha = jnp.exp(m_prev - m_new)
            l_ref[h] = alpha * l_ref[h] + jnp.sum(p, axis=-1, keepdims=True)
            vh = jnp.where(head_mask[h], v, jnp.zeros_like(v))
            acc_ref[h] = alpha * acc_ref[h] + _dot(p.astype(BF16), vh)
            m_ref[h] = m_new

    tile_step(own, True)

    def body(it, c):
        tile_step(own - 1 - it, False)
        return c

    lax.fori_loop(0, own, body, 0)
    out = acc_ref[0] * (1.0 / l_ref[0]) + acc_ref[1] * (1.0 / l_ref[1])
    o_ref[...] = out.astype(o_ref.dtype)


def _moba_call(qa, ka, p16, km, eye, batch, seq):
    n = batch * seq
    t = ATT_TILE
    nq = seq // t
    nb = seq // MB_BLOCK
    pairs = BRANCH_W // LANES
    v_col0 = 3 * pairs
    return pl.pallas_call(
        functools.partial(_moba_kernel, nb=nb),
        grid=(batch, pairs, nq),
        in_specs=[
            pl.BlockSpec((1, 2, t, LANES), lambda b, p, i: (b, p, i, 0)),
            pl.BlockSpec((1, 2, seq, LANES), lambda b, p, i: (b, p, 0, 0)),
            pl.BlockSpec((seq, LANES), lambda b, p, i: (b, v_col0 + p)),
            pl.BlockSpec((1, 2, nb, LANES), lambda b, p, i: (b, p, 0, 0)),
            pl.BlockSpec((t, t), lambda b, p, i: (0, 0)),
        ],
        out_specs=pl.BlockSpec((t, LANES), lambda b, p, i: (b * nq + i, p)),
        out_shape=jax.ShapeDtypeStruct((n, BRANCH_W), BF16),
        scratch_shapes=[
            pltpu.VMEM((2, t, LANES), BF16),
            pltpu.VMEM((2, t, LANES), F32),
            pltpu.VMEM((2, t, 1), F32),
            pltpu.VMEM((2, t, 1), F32),
        ],
        compiler_params=_cparams("parallel", "parallel", "arbitrary"),
        name="moba_attention",
    )(qa, ka, p16, km, eye)


def _mlstm_kernel(u_ref, v_ref, o_ref, ifc_ref, ift_ref, cw_ref, ltri_ref, utri_ref, y_ref,
                  xbuf, c_ref, n_ref, m_ref):
    L = ML_CHUNK
    W = BRANCH_W
    halo = SUBLANES

    @pl.when(pl.program_id(1) == 0)
    def _():
        xbuf[0:halo, :] = jnp.zeros((halo, 2 * W), F32)
        c_ref[...] = jnp.zeros_like(c_ref)
        n_ref[...] = jnp.zeros_like(n_ref)
        m_ref[...] = jnp.zeros_like(m_ref)

    xbuf[halo:, :] = u_ref[...]
    conv = jnp.zeros((L, 2 * W), F32)
    for j in range(CONV_W):
        off = halo - (CONV_W - 1) + j
        conv = conv + cw_ref[j:j + 1, :] * xbuf[off:off + L, :]
    xbuf[0:halo, :] = u_ref[L - halo:, :]
    qk = conv * _sigmoid(conv)

    ift = ift_ref[0]
    lf_rows = _log_sigmoid(ift)
    bcum_rows = _dot_exact_rhs(lf_rows, utri_ref[...])
    lf_cols = _log_sigmoid(ifc_ref[...])
    bcum_cols = _dot_exact_lhs(ltri_ref[...], lf_cols)

    row = lax.broadcasted_iota(jnp.int32, (L, L), 0)
    col = lax.broadcasted_iota(jnp.int32, (L, L), 1)
    causal = col <= row

    for h in range(ML_HEADS):
        sl = slice(h * ML_HDIM, (h + 1) * ML_HDIM)
        q = qk[:, sl].astype(BF16)
        k = (qk[:, W + h * ML_HDIM:W + (h + 1) * ML_HDIM] * (ML_HDIM ** -0.5)).astype(BF16)
        v = v_ref[:, sl]
        a_row = ift[h:h + 1, :] - bcum_rows[ML_HEADS + h:ML_HEADS + h + 1, :]
        bc = bcum_cols[:, ML_HEADS + h:ML_HEADS + h + 1]
        m_prev = m_ref[h, 0:1, 0:1]

        a_mat = jnp.where(causal, a_row, NEG_BIG)
        mu = jnp.maximum(jnp.max(a_mat, axis=-1, keepdims=True), m_prev)
        w_intra = jnp.exp(a_mat - mu)
        w_inter = jnp.exp(m_prev - mu)
        sc = _dot_nt(q, k) * w_intra
        c_prev = c_ref[h]
        n_prev = n_ref[h, 0:1, :]
        num = _dot(sc.astype(BF16), v) + w_inter * _dot_nt(q, c_prev.astype(BF16))
        qn = jnp.sum(q.astype(F32) * n_prev, axis=-1, keepdims=True)
        den = jnp.sum(sc, axis=-1, keepdims=True) + w_inter * qn
        m_t = bc + mu
        hs = num / jnp.maximum(jnp.abs(den), jnp.exp(-m_t))
        gate = _sigmoid(o_ref[:, sl].astype(F32))
        y_ref[:, sl] = (hs * gate).astype(y_ref.dtype)

        mu_last = mu[L - 1:L, :]
        m_new = bc[L - 1:L, :] + mu_last
        decay = jnp.exp(m_prev - mu_last)
        wk_row = jnp.exp(a_row - mu_last)
        vt = v.astype(F32).T
        c_ref[h] = decay * c_prev + _dot((vt * wk_row).astype(BF16), k)
        wk8 = jnp.broadcast_to(wk_row, (SUBLANES, L)).astype(BF16)
        n_ref[h] = decay * n_ref[h] + _dot(wk8, k)
        m_ref[h] = jnp.broadcast_to(m_new, (SUBLANES, LANES))


def _mlstm_call(p32, p16, ifc, ift, conv_w, ltri, utri, batch, seq):
    n = batch * seq
    L = ML_CHUNK
    nc = seq // L
    W = BRANCH_W
    return pl.pallas_call(
        _mlstm_kernel,
        grid=(batch, nc),
        in_specs=[
            pl.BlockSpec((L, 2 * W), lambda b, i: (b * nc + i, 1)),
            pl.BlockSpec((L, W), lambda b, i: (b * nc + i, 4)),
            pl.BlockSpec((L, W), lambda b, i: (b * nc + i, 5)),
            pl.BlockSpec((L, LANES), lambda b, i: (b * nc + i, 0)),
            pl.BlockSpec((1, SUBLANES, L), lambda b, i: (b, 0, i)),
            pl.BlockSpec((CONV_W, 2 * W), lambda b, i: (0, 0)),
            pl.BlockSpec((L, L), lambda b, i: (0, 0)),
            pl.BlockSpec((L, L), lambda b, i: (0, 0)),
        ],
        out_specs=pl.BlockSpec((L, W), lambda b, i: (b * nc + i, 0)),
        out_shape=jax.ShapeDtypeStruct((n, W), BF16),
        scratch_shapes=[
            pltpu.VMEM((L + SUBLANES, 2 * W), F32),
            pltpu.VMEM((ML_HEADS, ML_HDIM, ML_HDIM), F32),
            pltpu.VMEM((ML_HEADS, SUBLANES, ML_HDIM), F32),
            pltpu.VMEM((ML_HEADS, SUBLANES, LANES), F32),
        ],
        compiler_params=_cparams("parallel", "arbitrary"),
        name="mlstm",
    )(p32, p16, p16, ifc, ift, conv_w, ltri, utri)


def _merge_kernel(ysb_ref, ymb_ref, yml_ref, gl_ref, x_ref, wb_ref, wo_ref, g2_ref, wr_ref, br_ref,
                  h_ref, xn_ref, route_ref):
    merged = jnp.zeros((ROW_TILE, D_MODEL), F32)
    for b, y_ref in enumerate((ysb_ref, ymb_ref, yml_ref)):
        gate = _sigmoid(gl_ref[:, b * D_MODEL:(b + 1) * D_MODEL].astype(F32))
        merged = merged + gate * _dot(y_ref[...], wb_ref[b])
    hres = x_ref[...] + _dot(merged.astype(BF16), wo_ref[...])
    h_ref[...] = hres

    ms = jnp.mean(hres * hres, axis=-1, keepdims=True)
    xn = (hres * lax.rsqrt(ms + RMS_EPS) * g2_ref[...]).astype(BF16)
    xn_ref[...] = xn

    logits = _dot(xn, wr_ref[...]) + br_ref[...]
    lane = lax.broadcasted_iota(jnp.int32, (ROW_TILE, LANES), 1).astype(F32)
    far = float(LANES)
    is_g = lane < N_GROUPS
    gl = jnp.where(is_g, logits, NEG_BIG)
    gmax = jnp.max(gl, axis=-1, keepdims=True)
    gsum = jnp.sum(jnp.where(is_g, jnp.exp(gl - gmax), 0.0), axis=-1, keepdims=True)
    g_w = 1.0 / gsum
    g_idx = jnp.min(jnp.where(is_g & (gl == gmax), lane, far), axis=-1, keepdims=True)
    e_lo = N_GROUPS + g_idx * EXPERTS_PER_GROUP
    in_grp = (lane >= e_lo) & (lane < e_lo + EXPERTS_PER_GROUP)
    el = jnp.where(in_grp, logits, NEG_BIG)
    l1 = jnp.max(el, axis=-1, keepdims=True)
    i1 = jnp.min(jnp.where(in_grp & (el == l1), lane, far), axis=-1, keepdims=True)
    el2 = jnp.where(lane == i1, NEG_BIG, el)
    l2 = jnp.max(el2, axis=-1, keepdims=True)
    i2 = jnp.min(jnp.where(in_grp & (el2 == l2), lane, far), axis=-1, keepdims=True)
    p2 = jnp.exp(l2 - l1)
    w1 = g_w / (1.0 + p2)
    w2 = g_w * p2 / (1.0 + p2)
    slot = lane + e_lo
    cw = jnp.where(slot == i1, w1, 0.0) + jnp.where(slot == i2, w2, 0.0)
    cw = jnp.where(lane < EXPERTS_PER_GROUP, cw, 0.0)
    route_ref[...] = jnp.where(lane == EXPERTS_PER_GROUP, g_idx, cw)


def _merge_call(ysb, ymb, yml, p16, x, wb, wo, g2, wr, br):
    n = x.shape[0]
    row = lambda i: (i, 0)
    const2 = lambda i: (0, 0)
    return pl.pallas_call(
        _merge_kernel,
        grid=(n // ROW_TILE,),
        in_specs=[
            pl.BlockSpec((ROW_TILE, BRANCH_W), row),
            pl.BlockSpec((ROW_TILE, BRANCH_W), row),
            pl.BlockSpec((ROW_TILE, BRANCH_W), row),
            pl.BlockSpec((ROW_TILE, N_BRANCH * D_MODEL), lambda i: (i, 1)),
            pl.BlockSpec((ROW_TILE, D_MODEL), row),
            pl.BlockSpec((N_BRANCH, BRANCH_W, D_MODEL), lambda i: (0, 0, 0)),
            pl.BlockSpec((D_MODEL, D_MODEL), const2),
            pl.BlockSpec((1, D_MODEL), const2),
            pl.BlockSpec((D_MODEL, LANES), const2),
            pl.BlockSpec((1, LANES), const2),
        ],
        out_specs=[
            pl.BlockSpec((ROW_TILE, D_MODEL), row),
            pl.BlockSpec((ROW_TILE, D_MODEL), row),
            pl.BlockSpec((ROW_TILE, LANES), row),
        ],
        out_shape=[
            jax.ShapeDtypeStruct((n, D_MODEL), F32),
            jax.ShapeDtypeStruct((n, D_MODEL), BF16),
            jax.ShapeDtypeStruct((n, LANES), F32),
        ],
        compiler_params=_cparams("parallel"),
        name="merge_outproj_router",
    )(ysb, ymb, yml, p16, x, wb, wo, g2, wr, br)


def _moe_kernel(tg_ref, x_ref, cw_ref, w1_ref, w3_ref, w2_ref, y_ref):
    del tg_ref
    x = x_ref[...]
    cw = cw_ref[...]
    y = jnp.zeros((MOE_TILE, D_MODEL), F32)
    for e in range(EXPERTS_PER_GROUP):
        sl = slice(e * D_EXPERT, (e + 1) * D_EXPERT)
        a = _dot(x, w1_ref[0, :, sl])
        b = _dot(x, w3_ref[0, :, sl])
        hid = a * _sigmoid(a) * b * cw[:, e:e + 1]
        y = y + _dot(hid.astype(BF16), w2_ref[0, sl, :])
    y_ref[...] = y


def _moe_call(tile_group, xs, cws, w1g, w3g, w2g):
    npad = xs.shape[0]
    gw = EXPERTS_PER_GROUP * D_EXPERT
    grid_spec = pltpu.PrefetchScalarGridSpec(
        num_scalar_prefetch=1,
        grid=(npad // MOE_TILE,),
        in_specs=[
            pl.BlockSpec((MOE_TILE, D_MODEL), lambda i, tg: (i, 0)),
            pl.BlockSpec((MOE_TILE, LANES), lambda i, tg: (i, 0)),
            pl.BlockSpec((1, D_MODEL, gw), lambda i, tg: (tg[i], 0, 0)),
            pl.BlockSpec((1, D_MODEL, gw), lambda i, tg: (tg[i], 0, 0)),
            pl.BlockSpec((1, gw, D_MODEL), lambda i, tg: (tg[i], 0, 0)),
        ],
        out_specs=pl.BlockSpec((MOE_TILE, D_MODEL), lambda i, tg: (i, 0)),
    )
    return pl.pallas_call(
        _moe_kernel,
        grid_spec=grid_spec,
        out_shape=jax.ShapeDtypeStruct((npad, D_MODEL), F32),
        compiler_params=_cparams("arbitrary"),
        name="moe_group_experts",
    )(tile_group, xs, cws, w1g, w3g, w2g)


def _tables(positions):
    t = ATT_TILE
    r = jnp.arange(t)
    u_ext = jnp.concatenate(
        [(r[:, None] >= r[None, :]).astype(BF16), jnp.ones((t, LANES), BF16)], axis=1)
    eye = jnp.eye(t, dtype=BF16)
    L = ML_CHUNK
    rl = jnp.arange(L)
    ltri = (rl[:, None] >= rl[None, :]).astype(BF16)
    utri = ltri.T

    li = jnp.arange(LANES)
    seg = ((li[:, None] // HEAD_DIM) == (li[None, :] // HEAD_DIM)).astype(BF16)
    half = ROPE_DIM // 2
    src, dst = li[:, None], li[None, :]
    d_in = dst % HEAD_DIM
    rot = jnp.where((d_in < half) & (src == dst + half), -1.0,
                    jnp.where((d_in >= half) & (d_in < ROPE_DIM) & (src == dst - half), 1.0, 0.0)
                    ).astype(BF16)

    inv_freq = jnp.power(jnp.float32(ROPE_THETA), -jnp.arange(half, dtype=F32) / half)
    lane_in = li % HEAD_DIM
    ang = positions.reshape(-1).astype(F32)[:, None] * inv_freq[lane_in % half][None, :]
    rotated = (lane_in < ROPE_DIM)[None, :]
    cos_t = jnp.where(rotated, jnp.cos(ang), 1.0)
    sin_t = jnp.where(rotated, jnp.sin(ang), 0.0)
    return dict(u_ext=u_ext, eye=eye, ltri=ltri, utri=utri, seg=seg, rot=rot, cos=cos_t, sin=sin_t)


def _moe_dispatch(route, n):
    gid = route[:, EXPERTS_PER_GROUP].astype(jnp.int32)
    npad = n + N_GROUPS * MOE_TILE
    order = jnp.argsort(gid, stable=True).astype(jnp.int32)
    counts = jnp.sum(gid[:, None] == jnp.arange(N_GROUPS)[None, :], axis=0).astype(jnp.int32)
    starts = jnp.cumsum(counts) - counts
    padded = ((counts + MOE_TILE - 1) // MOE_TILE) * MOE_TILE
    pend = jnp.cumsum(padded)
    pstarts = pend - padded
    g_sorted = gid[order]
    dest_sorted = pstarts[g_sorted] + (jnp.arange(n, dtype=jnp.int32) - starts[g_sorted])
    src = jnp.zeros((npad,), jnp.int32).at[dest_sorted].set(order)
    valid = jnp.zeros((npad,), jnp.bool_).at[dest_sorted].set(True)
    dest_of_token = jnp.zeros((n,), jnp.int32).at[order].set(dest_sorted)
    tile_start = jnp.arange(npad // MOE_TILE, dtype=jnp.int32) * MOE_TILE
    tile_group = jnp.minimum(
        jnp.sum(tile_start[:, None] >= pend[None, :], axis=1), N_GROUPS - 1).astype(jnp.int32)
    return src, valid, dest_of_token, tile_group


def kernel(x, positions, norm1_g, w_in, qn_g, kn_g, conv_w, ml_gate_b, w_branch, w_out, norm2_g,
           w_rg, b_rg, w_re, b_re, w1, w3, w2):
    batch, seq, d = x.shape
    n = batch * seq
    W = BRANCH_W
    depth = w_in.shape[0]
    tb = _tables(positions)
    xf = x.reshape(n, d)

    for l in range(depth):
        wl = w_in[l]
        w16 = jnp.concatenate(
            [wl[:, 0:3 * W], wl[:, 5 * W:6 * W], wl[:, 8 * W:10 * W], wl[:, 10 * W + 2 * ML_HEADS:]],
            axis=1).astype(BF16)
        w32 = jnp.concatenate([wl[:, 3 * W:5 * W], wl[:, 6 * W:8 * W]], axis=1).astype(BF16)
        n_if = 2 * ML_HEADS
        wif = jnp.pad(wl[:, 10 * W:10 * W + n_if], ((0, 0), (0, LANES - n_if))).astype(BF16)
        bif = jnp.pad(ml_gate_b[l], (0, LANES - n_if)).reshape(1, LANES)

        xn, ifc = _norm_call(xf, norm1_g[l].reshape(1, d), wif, bif)
        p16 = _matmul_call(xn, w16, BF16, "inproj_bf16")
        p32 = _matmul_call(xn, w32, F32, "inproj_f32")

        y_sb = _sb_call(p16, tb["u_ext"], batch, seq)

        qg = jnp.tile(qn_g[l], LANES // HEAD_DIM).reshape(1, LANES)
        kg = jnp.tile(kn_g[l], LANES // HEAD_DIM).reshape(1, LANES)
        qa, ka, km = _moba_prep_call(p32, tb["cos"], tb["sin"], qg, kg, tb["seg"], tb["rot"],
                                     batch, seq)
        y_mb = _moba_call(qa, ka, p16, km.transpose(0, 2, 1, 3), tb["eye"], batch, seq)

        ift = ifc[:, :SUBLANES].reshape(batch, seq, SUBLANES).transpose(0, 2, 1)
        y_ml = _mlstm_call(p32, p16, ifc, ift, conv_w[l], tb["ltri"], tb["utri"], batch, seq)

        wr = jnp.pad(jnp.concatenate([w_rg[l], w_re[l]], axis=1),
                     ((0, 0), (0, LANES - N_GROUPS - N_EXPERTS))).astype(BF16)
        br = jnp.pad(jnp.concatenate([b_rg[l], b_re[l]]),
                     (0, LANES - N_GROUPS - N_EXPERTS)).reshape(1, LANES)
        hres, xn2, route = _merge_call(y_sb, y_mb, y_ml, p16, xf, w_branch[l].astype(BF16),
                                       w_out[l].astype(BF16), norm2_g[l].reshape(1, d), wr, br)

        src, valid, dest_of_token, tile_group = _moe_dispatch(route, n)
        xs = xn2[src]
        cws = jnp.where(valid[:, None], route[src], 0.0)
        gw = EXPERTS_PER_GROUP * D_EXPERT
        w1g = w1[l].reshape(N_GROUPS, EXPERTS_PER_GROUP, d, D_EXPERT).transpose(0, 2, 1, 3)
        w3g = w3[l].reshape(N_GROUPS, EXPERTS_PER_GROUP, d, D_EXPERT).transpose(0, 2, 1, 3)
        y_sorted = _moe_call(tile_group, xs, cws,
                             w1g.reshape(N_GROUPS, d, gw).astype(BF16),
                             w3g.reshape(N_GROUPS, d, gw).astype(BF16),
                             w2[l].reshape(N_GROUPS, gw, d).astype(BF16))
        xf = hres + y_sorted[dest_of_token]

    return xf.reshape(batch, seq, d)
```

```python
import functools

import jax
import jax.numpy as jnp
from jax import lax
from jax.experimental import pallas as pl
from jax.experimental.pallas import tpu as pltpu

D_MODEL = 1024
BRANCH_W = D_MODEL // 2
N_BRANCH = 3
HEAD_DIM = 64
N_ATT_HEADS = BRANCH_W // HEAD_DIM
MB_BLOCK = 256
MB_TOPK = 3
ROPE_THETA = 500000.0
ROPE_DIM = HEAD_DIM // 4
ML_HEADS = 4
ML_HDIM = BRANCH_W // ML_HEADS
CONV_W = 4
N_GROUPS = 4
EXPERTS_PER_GROUP = 8
N_EXPERTS = N_GROUPS * EXPERTS_PER_GROUP
D_EXPERT = D_MODEL // 4
RMS_EPS = 1e-6

LANES = 128
SUBLANES = 8
VMEM_LIMIT_BYTES = 56 * 1024 * 1024

ROW_TILE = 512
COL_TILE = 1024
ATT_TILE = 256
ML_CHUNK = 256
MOE_TILE = 256

LOG2_E = 1.4426950408889634
SB_PAIRS = 4
SB_EXIT_LOG2 = 160.0
MB_PAIRS = 2
NEG_BIG = -(2.0 ** 100)

F32 = jnp.float32
BF16 = jnp.bfloat16


def _cparams(*sem):
    return pltpu.CompilerParams(dimension_semantics=sem, vmem_limit_bytes=VMEM_LIMIT_BYTES)


def _dot(a, b):
    return jnp.dot(a, b, preferred_element_type=F32)


def _dot_nt(a, b):
    return lax.dot_general(a, b, (((1,), (1,)), ((), ())), preferred_element_type=F32)


def _split3(x):
    hi = x.astype(BF16)
    r1 = x - hi.astype(F32)
    mid = r1.astype(BF16)
    lo = (r1 - mid.astype(F32)).astype(BF16)
    return hi, mid, lo


def _dot_exact_rhs(x, m):
    hi, mid, lo = _split3(x)
    return _dot(hi, m) + _dot(mid, m) + _dot(lo, m)


def _dot_exact_lhs(m, x):
    hi, mid, lo = _split3(x)
    return _dot(m, hi) + _dot(m, mid) + _dot(m, lo)


def _log_sigmoid(x):
    return jnp.minimum(x, 0.0) - jnp.log(1.0 + jnp.exp(-jnp.abs(x)))


def _sigmoid(x):
    return 1.0 / (1.0 + jnp.exp(-x))


def _norm_kernel(x_ref, g_ref, wif_ref, bif_ref, xn_ref, if_ref):
    x = x_ref[...]
    ms = jnp.mean(x * x, axis=-1, keepdims=True)
    xn = (x * lax.rsqrt(ms + RMS_EPS) * g_ref[...]).astype(BF16)
    xn_ref[...] = xn
    if_ref[...] = _dot(xn, wif_ref[...]) + bif_ref[...]


def _norm_call(x, g, wif, bif):
    n = x.shape[0]
    return pl.pallas_call(
        _norm_kernel,
        grid=(n // ROW_TILE,),
        in_specs=[
            pl.BlockSpec((ROW_TILE, D_MODEL), lambda i: (i, 0)),
            pl.BlockSpec((1, D_MODEL), lambda i: (0, 0)),
            pl.BlockSpec((D_MODEL, LANES), lambda i: (0, 0)),
            pl.BlockSpec((1, LANES), lambda i: (0, 0)),
        ],
        out_specs=[
            pl.BlockSpec((ROW_TILE, D_MODEL), lambda i: (i, 0)),
            pl.BlockSpec((ROW_TILE, LANES), lambda i: (i, 0)),
        ],
        out_shape=[
            jax.ShapeDtypeStruct((n, D_MODEL), BF16),
            jax.ShapeDtypeStruct((n, LANES), F32),
        ],
        compiler_params=_cparams("parallel"),
        name="norm_gateproj",
    )(x, g, wif, bif)


def _matmul_kernel(a_ref, w_ref, o_ref):
    o_ref[...] = _dot(a_ref[...], w_ref[...]).astype(o_ref.dtype)


def _matmul_call(a, w, out_dtype, name):
    n, k = a.shape
    c = w.shape[1]
    tm = 2 * ROW_TILE
    return pl.pallas_call(
        _matmul_kernel,
        grid=(n // tm, c // COL_TILE),
        in_specs=[
            pl.BlockSpec((tm, k), lambda i, j: (i, 0)),
            pl.BlockSpec((k, COL_TILE), lambda i, j: (0, j)),
        ],
        out_specs=pl.BlockSpec((tm, COL_TILE), lambda i, j: (i, j)),
        out_shape=jax.ShapeDtypeStruct((n, c), out_dtype),
        compiler_params=_cparams("parallel", "parallel"),
        name=name,
    )(a, w)


def _sb_kernel(q_ref, k_ref, v_ref, u_ref, o_ref, acc_ref, carry_ref):
    t = ATT_TILE
    qi = pl.program_id(1)
    lane = lax.broadcasted_iota(jnp.int32, (1, LANES), 1)
    head_mask = (lane < HEAD_DIM, lane >= HEAD_DIM)
    n_heads = 2 * SB_PAIRS
    q = q_ref[...].astype(F32) * (HEAD_DIM ** -0.5 * LOG2_E)
    qh = []
    for p in range(SB_PAIRS):
        qp = q[:, p * LANES:(p + 1) * LANES]
        qh.extend(jnp.where(m, qp, 0.0).astype(BF16) for m in head_mask)
    u_tri = u_ref[...]

    acc_ref[...] = jnp.zeros_like(acc_ref)
    carry_ref[...] = jnp.zeros_like(carry_ref)

    def tile_step(kj, diagonal):
        start = pl.multiple_of(kj * t, t)
        if diagonal:
            row = lax.broadcasted_iota(jnp.int32, (t, t), 0)
            col = lax.broadcasted_iota(jnp.int32, (t, t), 1)
            past = col < row
        for idx in range(n_heads):
            p, h = divmod(idx, 2)
            k = k_ref[pl.ds(start, t), p * LANES:(p + 1) * LANES]
            v = v_ref[pl.ds(start, t), p * LANES:(p + 1) * LANES]
            z = _dot_nt(qh[idx], k)
            sp = jnp.maximum(z, 0.0) + jnp.log2(1.0 + jnp.exp2(-jnp.abs(z)))
            if diagonal:
                sp = jnp.where(past, sp, 0.0)
            carry = carry_ref[idx]
            r = _dot(sp.astype(BF16), u_tri) + jnp.concatenate([carry] * (t // LANES), axis=1)
            a = jnp.exp2(z - r)
            if diagonal:
                a = jnp.where(past, a, 0.0)
            vh = jnp.where(head_mask[h], v, jnp.zeros_like(v))
            acc_ref[idx] += _dot(a.astype(BF16), vh)
            carry = jnp.broadcast_to(r[:, 0:1], (t, LANES))
            carry_ref[idx] = carry
            carry_min = carry if idx == 0 else jnp.minimum(carry_min, carry)
        return jnp.min(carry_min)

    def cond(c):
        it, carry_min = c
        return (it < qi) & (carry_min < SB_EXIT_LOG2)

    def body(c):
        it, _ = c
        return it + 1, tile_step(qi - 1 - it, False)

    lax.while_loop(cond, body, (jnp.int32(0), tile_step(qi, True)))
    for p in range(SB_PAIRS):
        o_ref[:, p * LANES:(p + 1) * LANES] = (acc_ref[2 * p] + acc_ref[2 * p + 1]).astype(o_ref.dtype)


def _sb_call(p16, u_tri, batch, seq):
    n = batch * seq
    t = ATT_TILE
    nq = seq // t
    groups = BRANCH_W // (SB_PAIRS * LANES)
    w = SB_PAIRS * LANES
    return pl.pallas_call(
        _sb_kernel,
        grid=(batch * groups, nq),
        in_specs=[
            pl.BlockSpec((t, w), lambda g, i: ((g // groups) * nq + i, g % groups)),
            pl.BlockSpec((seq, w), lambda g, i: (g // groups, groups + g % groups)),
            pl.BlockSpec((seq, w), lambda g, i: (g // groups, 2 * groups + g % groups)),
            pl.BlockSpec((t, t), lambda g, i: (0, 0)),
        ],
        out_specs=pl.BlockSpec((t, w), lambda g, i: ((g // groups) * nq + i, g % groups)),
        out_shape=jax.ShapeDtypeStruct((n, BRANCH_W), BF16),
        scratch_shapes=[
            pltpu.VMEM((2 * SB_PAIRS, t, LANES), F32),
            pltpu.VMEM((2 * SB_PAIRS, t, LANES), F32),
        ],
        compiler_params=_cparams("parallel", "arbitrary"),
        name="stick_breaking_attention",
    )(p16, p16, p16, u_tri)


def _moba_prep_kernel(q_ref, k_ref, cos_ref, sin_ref, qg_ref, kg_ref, seg_ref, rot_ref,
                      qa_ref, ka_ref, km_ref):
    blk = pl.program_id(1)
    lane = lax.broadcasted_iota(jnp.int32, (1, LANES), 1)
    even = lane < HEAD_DIM
    cos = cos_ref[...]
    sin = sin_ref[...]
    seg = seg_ref[...]
    rot = rot_ref[...]

    def norm_rope(x, g):
        ss = _dot_exact_rhs(x * x, seg)
        xn = x * lax.rsqrt(ss * (1.0 / HEAD_DIM) + RMS_EPS) * g
        return xn * cos + _dot_exact_rhs(xn, rot) * sin

    onehot_even = jnp.where(lane == HEAD_DIM + blk, 1.0, 0.0)
    onehot_odd = jnp.where(lane == blk, 1.0, 0.0)

    km_rows = []
    for c in range(BRANCH_W // LANES):
        sl = slice(c * LANES, (c + 1) * LANES)
        qr = norm_rope(q_ref[:, sl], qg_ref[...]) * (HEAD_DIM ** -0.5 * LOG2_E)
        kr = norm_rope(k_ref[:, sl], kg_ref[...])
        qa_ref[0, 2 * c] = jnp.where(even, qr, 0.0).astype(BF16)
        qa_ref[0, 2 * c + 1] = jnp.where(even, 0.0, qr).astype(BF16)
        ka_ref[0, 2 * c] = jnp.where(even, kr, onehot_even).astype(BF16)
        ka_ref[0, 2 * c + 1] = jnp.where(even, onehot_odd, kr).astype(BF16)
        kmean = jnp.mean(kr, axis=0, keepdims=True)
        km_rows.append(jnp.where(even, kmean, 0.0))
        km_rows.append(jnp.where(even, 0.0, kmean))
    km_ref[0, 0] = jnp.concatenate(km_rows, axis=0)


def _moba_prep_call(p32, cos_t, sin_t, qg, kg, seg, rot, batch, seq):
    t = MB_BLOCK
    nb = seq // t
    h = N_ATT_HEADS
    return pl.pallas_call(
        _moba_prep_kernel,
        grid=(batch, nb),
        in_specs=[
            pl.BlockSpec((t, BRANCH_W), lambda b, i: (b * nb + i, 0)),
            pl.BlockSpec((t, BRANCH_W), lambda b, i: (b * nb + i, 1)),
            pl.BlockSpec((t, LANES), lambda b, i: (b * nb + i, 0)),
            pl.BlockSpec((t, LANES), lambda b, i: (b * nb + i, 0)),
            pl.BlockSpec((1, LANES), lambda b, i: (0, 0)),
            pl.BlockSpec((1, LANES), lambda b, i: (0, 0)),
            pl.BlockSpec((LANES, LANES), lambda b, i: (0, 0)),
            pl.BlockSpec((LANES, LANES), lambda b, i: (0, 0)),
        ],
        out_specs=[
            pl.BlockSpec((1, h, t, LANES), lambda b, i: (b, 0, i, 0)),
            pl.BlockSpec((1, h, t, LANES), lambda b, i: (b, 0, i, 0)),
            pl.BlockSpec((1, 1, h, LANES), lambda b, i: (b, i, 0, 0)),
        ],
        out_shape=[
            jax.ShapeDtypeStruct((batch, h, seq, LANES), BF16),
            jax.ShapeDtypeStruct((batch, h, seq, LANES), BF16),
            jax.ShapeDtypeStruct((batch, nb, h, LANES), F32),
        ],
        compiler_params=_cparams("parallel", "parallel"),
        name="moba_qk_prep",
    )(p32, p32, cos_t, sin_t, qg, kg, seg, rot)


def _moba_kernel(qa_ref, ka_ref, v_ref, km_ref, eye_ref, o_ref,
                 q_scr, s_scr, mx_ref, l_ref, acc_ref, *, nb):
    t = ATT_TILE
    own = pl.program_id(1)
    n_heads = 2 * MB_PAIRS
    lane = lax.broadcasted_iota(jnp.int32, (1, LANES), 1)
    head_mask = (lane < HEAD_DIM, lane >= HEAD_DIM)
    blk_row = lax.broadcasted_iota(jnp.int32, (nb, t), 0)

    for h in range(n_heads):
        q = qa_ref[0, h]
        km_hi, km_mid, km_lo = _split3(km_ref[0, h])
        g = _dot_nt(km_hi, q) + _dot_nt(km_mid, q) + _dot_nt(km_lo, q)
        cnt = jnp.zeros((nb, t), jnp.int32)
        for m in range(nb):
            gm = g[m:m + 1, :]
            beats = (gm > g) | ((gm == g) & (m < blk_row))
            cnt = cnt + jnp.where(beats, jnp.where(m < own, 1, 0), 0)
        keep = ((blk_row < own) & (cnt < MB_TOPK)) | (blk_row == own)
        bias = jnp.where(keep, 0.0, NEG_BIG)
        aux_lo = HEAD_DIM if h % 2 == 0 else 0
        pieces = []
        if aux_lo:
            pieces.append(jnp.zeros((aux_lo, t), F32))
        pieces.append(bias)
        pieces.append(jnp.zeros((LANES - aux_lo - nb, t), F32))
        bias_rows = jnp.concatenate(pieces, axis=0).astype(BF16)
        bias_cols = _dot_nt(eye_ref[...], bias_rows)
        q_scr[h] = (q.astype(F32) + bias_cols).astype(BF16)

    def score_step(kj, diagonal):
        start = pl.multiple_of(kj * t, t)
        for h in range(n_heads):
            s = _dot_nt(q_scr[h], ka_ref[0, h, pl.ds(start, t), :])
            if diagonal:
                row = lax.broadcasted_iota(jnp.int32, (t, t), 0)
                col = lax.broadcasted_iota(jnp.int32, (t, t), 1)
                s = jnp.where(col <= row, s, NEG_BIG)
                mx_ref[h] = s
            else:
                mx_ref[h] = jnp.maximum(mx_ref[h], s)
            s_scr[h, kj] = s

    score_step(own, True)

    def score_body(it, c):
        score_step(own - 1 - it, False)
        return c

    lax.fori_loop(0, own, score_body, 0)

    for h in range(n_heads):
        mx_ref[h] = jnp.broadcast_to(jnp.max(mx_ref[h], axis=-1, keepdims=True), (t, t))
    acc_ref[...] = jnp.zeros_like(acc_ref)
    l_ref[...] = jnp.zeros_like(l_ref)

    def value_body(kj, c):
        start = pl.multiple_of(kj * t, t)
        for h in range(n_heads):
            pr, e = divmod(h, 2)
            p = jnp.exp2(s_scr[h, kj] - mx_ref[h])
            l_ref[h] += p
            v = v_ref[pl.ds(start, t), pr * LANES:(pr + 1) * LANES]
            vh = jnp.where(head_mask[e], v, jnp.zeros_like(v))
            acc_ref[h] += _dot(p.astype(BF16), vh)
        return c

    lax.fori_loop(0, own + 1, value_body, 0)

    for pr in range(MB_PAIRS):
        inv0 = 1.0 / jnp.sum(l_ref[2 * pr], axis=-1, keepdims=True)
        inv1 = 1.0 / jnp.sum(l_ref[2 * pr + 1], axis=-1, keepdims=True)
        out = acc_ref[2 * pr] * inv0 + acc_ref[2 * pr + 1] * inv1
        o_ref[:, pr * LANES:(pr + 1) * LANES] = out.astype(o_ref.dtype)


def _moba_call(qa, ka, p16, km, eye, batch, seq):
    n = batch * seq
    t = ATT_TILE
    nq = seq // t
    nb = seq // MB_BLOCK
    groups = BRANCH_W // (MB_PAIRS * LANES)
    nh = 2 * MB_PAIRS
    w = MB_PAIRS * LANES
    v_col0 = 3 * BRANCH_W // w
    return pl.pallas_call(
        functools.partial(_moba_kernel, nb=nb),
        grid=(batch * groups, nq),
        in_specs=[
            pl.BlockSpec((1, nh, t, LANES), lambda g, i: (g // groups, g % groups, i, 0)),
            pl.BlockSpec((1, nh, seq, LANES), lambda g, i: (g // groups, g % groups, 0, 0)),
            pl.BlockSpec((seq, w), lambda g, i: (g // groups, v_col0 + g % groups)),
            pl.BlockSpec((1, nh, nb, LANES), lambda g, i: (g // groups, g % groups, 0, 0)),
            pl.BlockSpec((t, t), lambda g, i: (0, 0)),
        ],
        out_specs=pl.BlockSpec((t, w), lambda g, i: ((g // groups) * nq + i, g % groups)),
        out_shape=jax.ShapeDtypeStruct((n, BRANCH_W), BF16),
        scratch_shapes=[
            pltpu.VMEM((nh, t, LANES), BF16),
            pltpu.VMEM((nh, nb, t, t), F32),
            pltpu.VMEM((nh, t, t), F32),
            pltpu.VMEM((nh, t, t), F32),
            pltpu.VMEM((nh, t, LANES), F32),
        ],
        compiler_params=_cparams("parallel", "arbitrary"),
        name="moba_attention",
    )(qa, ka, p16, km, eye)


def _mlstm_kernel(u_ref, v_ref, o_ref, ifc_ref, ift_ref, cw_ref, ltri_ref, utri_ref, y_ref,
                  xbuf, c_ref, n_ref, m_ref):
    L = ML_CHUNK
    W = BRANCH_W
    halo = SUBLANES

    @pl.when(pl.program_id(1) == 0)
    def _():
        xbuf[0:halo, :] = jnp.zeros((halo, 2 * W), F32)
        c_ref[...] = jnp.zeros_like(c_ref)
        n_ref[...] = jnp.zeros_like(n_ref)
        m_ref[...] = jnp.zeros_like(m_ref)

    xbuf[halo:, :] = u_ref[...]
    conv = jnp.zeros((L, 2 * W), F32)
    for j in range(CONV_W):
        off = halo - (CONV_W - 1) + j
        conv = conv + cw_ref[j:j + 1, :] * xbuf[off:off + L, :]
    xbuf[0:halo, :] = u_ref[L - halo:, :]
    qk = conv * _sigmoid(conv)

    ift = ift_ref[0]
    lf_rows = _log_sigmoid(ift)
    bcum_rows = _dot_exact_rhs(lf_rows, utri_ref[...])
    lf_cols = _log_sigmoid(ifc_ref[...])
    bcum_cols = _dot_exact_lhs(ltri_ref[...], lf_cols)

    row = lax.broadcasted_iota(jnp.int32, (L, L), 0)
    col = lax.broadcasted_iota(jnp.int32, (L, L), 1)
    causal = col <= row

    for h in range(ML_HEADS):
        sl = slice(h * ML_HDIM, (h + 1) * ML_HDIM)
        q = qk[:, sl].astype(BF16)
        k = (qk[:, W + h * ML_HDIM:W + (h + 1) * ML_HDIM] * (ML_HDIM ** -0.5)).astype(BF16)
        v = v_ref[:, sl]
        a_row = ift[h:h + 1, :] - bcum_rows[ML_HEADS + h:ML_HEADS + h + 1, :]
        bc = bcum_cols[:, ML_HEADS + h:ML_HEADS + h + 1]
        m_prev = m_ref[h, 0:1, 0:1]

        a_mat = jnp.where(causal, a_row, NEG_BIG)
        mu = jnp.maximum(jnp.max(a_mat, axis=-1, keepdims=True), m_prev)
        w_intra = jnp.exp(a_mat - mu)
        w_inter = jnp.exp(m_prev - mu)
        sc = _dot_nt(q, k) * w_intra
        c_prev = c_ref[h]
        n_prev = n_ref[h, 0:1, :]
        num = _dot(sc.astype(BF16), v) + w_inter * _dot_nt(q, c_prev.astype(BF16))
        qn = jnp.sum(q.astype(F32) * n_prev, axis=-1, keepdims=True)
        den = jnp.sum(sc, axis=-1, keepdims=True) + w_inter * qn
        m_t = bc + mu
        hs = num / jnp.maximum(jnp.abs(den), jnp.exp(-m_t))
        gate = _sigmoid(o_ref[:, sl].astype(F32))
        y_ref[:, sl] = (hs * gate).astype(y_ref.dtype)

        mu_last = mu[L - 1:L, :]
        m_new = bc[L - 1:L, :] + mu_last
        decay = jnp.exp(m_prev - mu_last)
        wk_row = jnp.exp(a_row - mu_last)
        vt = v.astype(F32).T
        c_ref[h] = decay * c_prev + _dot((vt * wk_row).astype(BF16), k)
        wk8 = jnp.broadcast_to(wk_row, (SUBLANES, L)).astype(BF16)
        n_ref[h] = decay * n_ref[h] + _dot(wk8, k)
        m_ref[h] = jnp.broadcast_to(m_new, (SUBLANES, LANES))


def _mlstm_call(p32, p16, ifc, ift, conv_w, ltri, utri, batch, seq):
    n = batch * seq
    L = ML_CHUNK
    nc = seq // L
    W = BRANCH_W
    return pl.pallas_call(
        _mlstm_kernel,
        grid=(batch, nc),
        in_specs=[
            pl.BlockSpec((L, 2 * W), lambda b, i: (b * nc + i, 1)),
            pl.BlockSpec((L, W), lambda b, i: (b * nc + i, 4)),
            pl.BlockSpec((L, W), lambda b, i: (b * nc + i, 5)),
            pl.BlockSpec((L, LANES), lambda b, i: (b * nc + i, 0)),
            pl.BlockSpec((1, SUBLANES, L), lambda b, i: (b, 0, i)),
            pl.BlockSpec((CONV_W, 2 * W), lambda b, i: (0, 0)),
            pl.BlockSpec((L, L), lambda b, i: (0, 0)),
            pl.BlockSpec((L, L), lambda b, i: (0, 0)),
        ],
        out_specs=pl.BlockSpec((L, W), lambda b, i: (b * nc + i, 0)),
        out_shape=jax.ShapeDtypeStruct((n, W), BF16),
        scratch_shapes=[
            pltpu.VMEM((L + SUBLANES, 2 * W), F32),
            pltpu.VMEM((ML_HEADS, ML_HDIM, ML_HDIM), F32),
            pltpu.VMEM((ML_HEADS, SUBLANES, ML_HDIM), F32),
            pltpu.VMEM((ML_HEADS, SUBLANES, LANES), F32),
        ],
        compiler_params=_cparams("parallel", "arbitrary"),
        name="mlstm",
    )(p32, p16, p16, ifc, ift, conv_w, ltri, utri)


def _merge_kernel(ysb_ref, ymb_ref, yml_ref, gl_ref, x_ref, wb_ref, wo_ref, g2_ref, wr_ref, br_ref,
                  h_ref, xn_ref, route_ref):
    merged = jnp.zeros((ROW_TILE, D_MODEL), F32)
    for b, y_ref in enumerate((ysb_ref, ymb_ref, yml_ref)):
        gate = _sigmoid(gl_ref[:, b * D_MODEL:(b + 1) * D_MODEL].astype(F32))
        merged = merged + gate * _dot(y_ref[...], wb_ref[b])
    hres = x_ref[...] + _dot(merged.astype(BF16), wo_ref[...])
    h_ref[...] = hres

    ms = jnp.mean(hres * hres, axis=-1, keepdims=True)
    xn = (hres * lax.rsqrt(ms + RMS_EPS) * g2_ref[...]).astype(BF16)
    xn_ref[...] = xn

    logits = _dot(xn, wr_ref[...]) + br_ref[...]
    lane = lax.broadcasted_iota(jnp.int32, (ROW_TILE, LANES), 1).astype(F32)
    far = float(LANES)
    is_g = lane < N_GROUPS
    gl = jnp.where(is_g, logits, NEG_BIG)
    gmax = jnp.max(gl, axis=-1, keepdims=True)
    gsum = jnp.sum(jnp.where(is_g, jnp.exp(gl - gmax), 0.0), axis=-1, keepdims=True)
    g_w = 1.0 / gsum
    g_idx = jnp.min(jnp.where(is_g & (gl == gmax), lane, far), axis=-1, keepdims=True)
    e_lo = N_GROUPS + g_idx * EXPERTS_PER_GROUP
    in_grp = (lane >= e_lo) & (lane < e_lo + EXPERTS_PER_GROUP)
    el = jnp.where(in_grp, logits, NEG_BIG)
    l1 = jnp.max(el, axis=-1, keepdims=True)
    i1 = jnp.min(jnp.where(in_grp & (el == l1), lane, far), axis=-1, keepdims=True)
    el2 = jnp.where(lane == i1, NEG_BIG, el)
    l2 = jnp.max(el2, axis=-1, keepdims=True)
    i2 = jnp.min(jnp.where(in_grp & (el2 == l2), lane, far), axis=-1, keepdims=True)
    p2 = jnp.exp(l2 - l1)
    w1 = g_w / (1.0 + p2)
    w2 = g_w * p2 / (1.0 + p2)
    slot = lane + e_lo
    cw = jnp.where(slot == i1, w1, 0.0) + jnp.where(slot == i2, w2, 0.0)
    cw = jnp.where(lane < EXPERTS_PER_GROUP, cw, 0.0)
    route_ref[...] = jnp.where(lane == EXPERTS_PER_GROUP, g_idx, cw)


def _merge_call(ysb, ymb, yml, p16, x, wb, wo, g2, wr, br):
    n = x.shape[0]
    row = lambda i: (i, 0)
    const2 = lambda i: (0, 0)
    return pl.pallas_call(
        _merge_kernel,
        grid=(n // ROW_TILE,),
        in_specs=[
            pl.BlockSpec((ROW_TILE, BRANCH_W), row),
            pl.BlockSpec((ROW_TILE, BRANCH_W), row),
            pl.BlockSpec((ROW_TILE, BRANCH_W), row),
            pl.BlockSpec((ROW_TILE, N_BRANCH * D_MODEL), lambda i: (i, 1)),
            pl.BlockSpec((ROW_TILE, D_MODEL), row),
            pl.BlockSpec((N_BRANCH, BRANCH_W, D_MODEL), lambda i: (0, 0, 0)),
            pl.BlockSpec((D_MODEL, D_MODEL), const2),
            pl.BlockSpec((1, D_MODEL), const2),
            pl.BlockSpec((D_MODEL, LANES), const2),
            pl.BlockSpec((1, LANES), const2),
        ],
        out_specs=[
            pl.BlockSpec((ROW_TILE, D_MODEL), row),
            pl.BlockSpec((ROW_TILE, D_MODEL), row),
            pl.BlockSpec((ROW_TILE, LANES), row),
        ],
        out_shape=[
            jax.ShapeDtypeStruct((n, D_MODEL), F32),
            jax.ShapeDtypeStruct((n, D_MODEL), BF16),
            jax.ShapeDtypeStruct((n, LANES), F32),
        ],
        compiler_params=_cparams("parallel"),
        name="merge_outproj_router",
    )(ysb, ymb, yml, p16, x, wb, wo, g2, wr, br)


def _moe_kernel(tg_ref, x_ref, cw_ref, w1_ref, w3_ref, w2_ref, y_ref):
    del tg_ref
    x = x_ref[...]
    cw = cw_ref[...]
    y = jnp.zeros((MOE_TILE, D_MODEL), F32)
    for e in range(EXPERTS_PER_GROUP):
        sl = slice(e * D_EXPERT, (e + 1) * D_EXPERT)
        a = _dot(x, w1_ref[0, :, sl])
        b = _dot(x, w3_ref[0, :, sl])
        hid = a * _sigmoid(a) * b * cw[:, e:e + 1]
        y = y + _dot(hid.astype(BF16), w2_ref[0, sl, :])
    y_ref[...] = y


def _moe_call(tile_group, xs, cws, w1g, w3g, w2g):
    npad = xs.shape[0]
    gw = EXPERTS_PER_GROUP * D_EXPERT
    grid_spec = pltpu.PrefetchScalarGridSpec(
        num_scalar_prefetch=1,
        grid=(npad // MOE_TILE,),
        in_specs=[
            pl.BlockSpec((MOE_TILE, D_MODEL), lambda i, tg: (i, 0)),
            pl.BlockSpec((MOE_TILE, LANES), lambda i, tg: (i, 0)),
            pl.BlockSpec((1, D_MODEL, gw), lambda i, tg: (tg[i], 0, 0)),
            pl.BlockSpec((1, D_MODEL, gw), lambda i, tg: (tg[i], 0, 0)),
            pl.BlockSpec((1, gw, D_MODEL), lambda i, tg: (tg[i], 0, 0)),
        ],
        out_specs=pl.BlockSpec((MOE_TILE, D_MODEL), lambda i, tg: (i, 0)),
    )
    return pl.pallas_call(
        _moe_kernel,
        grid_spec=grid_spec,
        out_shape=jax.ShapeDtypeStruct((npad, D_MODEL), F32),
        compiler_params=_cparams("arbitrary"),
        name="moe_group_experts",
    )(tile_group, xs, cws, w1g, w3g, w2g)


def _tables(positions):
    t = ATT_TILE
    r = jnp.arange(t)
    u_tri = (r[:, None] >= r[None, :]).astype(BF16)
    eye = jnp.eye(t, dtype=BF16)
    L = ML_CHUNK
    rl = jnp.arange(L)
    ltri = (rl[:, None] >= rl[None, :]).astype(BF16)
    utri = ltri.T

    li = jnp.arange(LANES)
    seg = ((li[:, None] // HEAD_DIM) == (li[None, :] // HEAD_DIM)).astype(BF16)
    half = ROPE_DIM // 2
    src, dst = li[:, None], li[None, :]
    d_in = dst % HEAD_DIM
    rot = jnp.where((d_in < half) & (src == dst + half), -1.0,
                    jnp.where((d_in >= half) & (d_in < ROPE_DIM) & (src == dst - half), 1.0, 0.0)
                    ).astype(BF16)

    inv_freq = jnp.power(jnp.float32(ROPE_THETA), -jnp.arange(half, dtype=F32) / half)
    lane_in = li % HEAD_DIM
    ang = positions.reshape(-1).astype(F32)[:, None] * inv_freq[lane_in % half][None, :]
    rotated = (lane_in < ROPE_DIM)[None, :]
    cos_t = jnp.where(rotated, jnp.cos(ang), 1.0)
    sin_t = jnp.where(rotated, jnp.sin(ang), 0.0)
    return dict(u_tri=u_tri, eye=eye, ltri=ltri, utri=utri, seg=seg, rot=rot, cos=cos_t, sin=sin_t)


def _moe_dispatch(route, n):
    gid = route[:, EXPERTS_PER_GROUP].astype(jnp.int32)
    npad = n + N_GROUPS * MOE_TILE
    order = jnp.argsort(gid, stable=True).astype(jnp.int32)
    counts = jnp.sum(gid[:, None] == jnp.arange(N_GROUPS)[None, :], axis=0).astype(jnp.int32)
    starts = jnp.cumsum(counts) - counts
    padded = ((counts + MOE_TILE - 1) // MOE_TILE) * MOE_TILE
    pend = jnp.cumsum(padded)
    pstarts = pend - padded
    g_sorted = gid[order]
    dest_sorted = pstarts[g_sorted] + (jnp.arange(n, dtype=jnp.int32) - starts[g_sorted])
    src = jnp.zeros((npad,), jnp.int32).at[dest_sorted].set(order)
    valid = jnp.zeros((npad,), jnp.bool_).at[dest_sorted].set(True)
    dest_of_token = jnp.zeros((n,), jnp.int32).at[order].set(dest_sorted)
    tile_start = jnp.arange(npad // MOE_TILE, dtype=jnp.int32) * MOE_TILE
    tile_group = jnp.minimum(
        jnp.sum(tile_start[:, None] >= pend[None, :], axis=1), N_GROUPS - 1).astype(jnp.int32)
    return src, valid, dest_of_token, tile_group


def kernel(x, positions, norm1_g, w_in, qn_g, kn_g, conv_w, ml_gate_b, w_branch, w_out, norm2_g,
           w_rg, b_rg, w_re, b_re, w1, w3, w2):
    batch, seq, d = x.shape
    n = batch * seq
    W = BRANCH_W
    depth = w_in.shape[0]
    tb = _tables(positions)
    xf = x.reshape(n, d)

    for l in range(depth):
        wl = w_in[l]
        w16 = jnp.concatenate(
            [wl[:, 0:3 * W], wl[:, 5 * W:6 * W], wl[:, 8 * W:10 * W], wl[:, 10 * W + 2 * ML_HEADS:]],
            axis=1).astype(BF16)
        w32 = jnp.concatenate([wl[:, 3 * W:5 * W], wl[:, 6 * W:8 * W]], axis=1).astype(BF16)
        n_if = 2 * ML_HEADS
        wif = jnp.pad(wl[:, 10 * W:10 * W + n_if], ((0, 0), (0, LANES - n_if))).astype(BF16)
        bif = jnp.pad(ml_gate_b[l], (0, LANES - n_if)).reshape(1, LANES)

        xn, ifc = _norm_call(xf, norm1_g[l].reshape(1, d), wif, bif)
        p16 = _matmul_call(xn, w16, BF16, "inproj_bf16")
        p32 = _matmul_call(xn, w32, F32, "inproj_f32")

        y_sb = _sb_call(p16, tb["u_tri"], batch, seq)

        qg = jnp.tile(qn_g[l], LANES // HEAD_DIM).reshape(1, LANES)
        kg = jnp.tile(kn_g[l], LANES // HEAD_DIM).reshape(1, LANES)
        qa, ka, km = _moba_prep_call(p32, tb["cos"], tb["sin"], qg, kg, tb["seg"], tb["rot"],
                                     batch, seq)
        y_mb = _moba_call(qa, ka, p16, km.transpose(0, 2, 1, 3), tb["eye"], batch, seq)

        ift = ifc[:, :SUBLANES].reshape(batch, seq, SUBLANES).transpose(0, 2, 1)
        y_ml = _mlstm_call(p32, p16, ifc, ift, conv_w[l], tb["ltri"], tb["utri"], batch, seq)

        wr = jnp.pad(jnp.concatenate([w_rg[l], w_re[l]], axis=1),
                     ((0, 0), (0, LANES - N_GROUPS - N_EXPERTS))).astype(BF16)
        br = jnp.pad(jnp.concatenate([b_rg[l], b_re[l]]),
                     (0, LANES - N_GROUPS - N_EXPERTS)).reshape(1, LANES)
        hres, xn2, route = _merge_call(y_sb, y_mb, y_ml, p16, xf, w_branch[l].astype(BF16),
                                       w_out[l].astype(BF16), norm2_g[l].reshape(1, d), wr, br)

        src, valid, dest_of_token, tile_group = _moe_dispatch(route, n)
        xs = xn2[src]
        cws = jnp.where(valid[:, None], route[src], 0.0)
        gw = EXPERTS_PER_GROUP * D_EXPERT
        w1g = w1[l].reshape(N_GROUPS, EXPERTS_PER_GROUP, d, D_EXPERT).transpose(0, 2, 1, 3)
        w3g = w3[l].reshape(N_GROUPS, EXPERTS_PER_GROUP, d, D_EXPERT).transpose(0, 2, 1, 3)
        y_sorted = _moe_call(tile_group, xs, cws,
                             w1g.reshape(N_GROUPS, d, gw).astype(BF16),
                             w3g.reshape(N_GROUPS, d, gw).astype(BF16),
                             w2[l].reshape(N_GROUPS, gw, d).astype(BF16))
        xf = hres + y_sorted[dest_of_token]

    return xf.reshape(batch, seq, d)
```

```python
import functools

import jax
import jax.numpy as jnp
from jax import lax
from jax.experimental import pallas as pl
from jax.experimental.pallas import tpu as pltpu

D_MODEL = 1024
BRANCH_W = D_MODEL // 2
N_BRANCH = 3
HEAD_DIM = 64
N_ATT_HEADS = BRANCH_W // HEAD_DIM
MB_BLOCK = 256
MB_TOPK = 3
ROPE_THETA = 500000.0
ROPE_DIM = HEAD_DIM // 4
ML_HEADS = 4
ML_HDIM = BRANCH_W // ML_HEADS
CONV_W = 4
N_GROUPS = 4
EXPERTS_PER_GROUP = 8
N_EXPERTS = N_GROUPS * EXPERTS_PER_GROUP
D_EXPERT = D_MODEL // 4
RMS_EPS = 1e-6

LANES = 128
SUBLANES = 8
VMEM_LIMIT_BYTES = 56 * 1024 * 1024

ROW_TILE = 512
COL_TILE = 1024
ATT_TILE = 256
ML_CHUNK = 256
MOE_BLOCK = 1024

LOG2_E = 1.4426950408889634
SB_PAIRS = 4
SB_EXIT_LOG2 = 160.0
MB_PAIRS = 2
NEG_BIG = -(2.0 ** 100)

F32 = jnp.float32
BF16 = jnp.bfloat16


def _cparams(*sem):
    return pltpu.CompilerParams(dimension_semantics=sem, vmem_limit_bytes=VMEM_LIMIT_BYTES)


def _dot(a, b):
    return jnp.dot(a, b, preferred_element_type=F32)


def _dot_nt(a, b):
    return lax.dot_general(a, b, (((1,), (1,)), ((), ())), preferred_element_type=F32)


def _split3(x):
    hi = x.astype(BF16)
    r1 = x - hi.astype(F32)
    mid = r1.astype(BF16)
    lo = (r1 - mid.astype(F32)).astype(BF16)
    return hi, mid, lo


def _dot_exact_rhs(x, m):
    hi, mid, lo = _split3(x)
    return _dot(hi, m) + _dot(mid, m) + _dot(lo, m)


def _dot_exact_lhs(m, x):
    hi, mid, lo = _split3(x)
    return _dot(m, hi) + _dot(m, mid) + _dot(m, lo)


def _log_sigmoid(x):
    return jnp.minimum(x, 0.0) - jnp.log(1.0 + jnp.exp(-jnp.abs(x)))


def _sigmoid(x):
    return 1.0 / (1.0 + jnp.exp(-x))


def _norm_kernel(x_ref, g_ref, wif_ref, bif_ref, xn_ref, if_ref):
    x = x_ref[...]
    ms = jnp.mean(x * x, axis=-1, keepdims=True)
    xn = (x * lax.rsqrt(ms + RMS_EPS) * g_ref[...]).astype(BF16)
    xn_ref[...] = xn
    if_ref[...] = _dot(xn, wif_ref[...]) + bif_ref[...]


def _add_norm_kernel(h_ref, y_ref, g_ref, wif_ref, bif_ref, x_ref, xn_ref, if_ref):
    x_ref[...] = h_ref[...] + y_ref[...]
    _norm_kernel(x_ref, g_ref, wif_ref, bif_ref, xn_ref, if_ref)


def _norm_call(x, y, g, wif, bif):
    n = x.shape[0]
    row = pl.BlockSpec((ROW_TILE, D_MODEL), lambda i: (i, 0))
    w_specs = [
        pl.BlockSpec((1, D_MODEL), lambda i: (0, 0)),
        pl.BlockSpec((D_MODEL, LANES), lambda i: (0, 0)),
        pl.BlockSpec((1, LANES), lambda i: (0, 0)),
    ]
    outs = [row, pl.BlockSpec((ROW_TILE, LANES), lambda i: (i, 0))]
    shapes = [jax.ShapeDtypeStruct((n, D_MODEL), BF16), jax.ShapeDtypeStruct((n, LANES), F32)]
    if y is None:
        xn, ifc = pl.pallas_call(
            _norm_kernel, grid=(n // ROW_TILE,), in_specs=[row] + w_specs, out_specs=outs,
            out_shape=shapes, compiler_params=_cparams("parallel"), name="norm_gateproj",
        )(x, g, wif, bif)
        return x, xn, ifc
    return pl.pallas_call(
        _add_norm_kernel, grid=(n // ROW_TILE,), in_specs=[row, row] + w_specs,
        out_specs=[row] + outs, out_shape=[jax.ShapeDtypeStruct((n, D_MODEL), F32)] + shapes,
        compiler_params=_cparams("parallel"), name="add_norm_gateproj",
    )(x, y, g, wif, bif)


def _matmul_kernel(a_ref, w_ref, o_ref):
    o_ref[...] = _dot(a_ref[...], w_ref[...]).astype(o_ref.dtype)


def _matmul_call(a, w, out_dtype, name):
    n, k = a.shape
    c = w.shape[1]
    tm = 2 * ROW_TILE
    return pl.pallas_call(
        _matmul_kernel,
        grid=(n // tm, c // COL_TILE),
        in_specs=[
            pl.BlockSpec((tm, k), lambda i, j: (i, 0)),
            pl.BlockSpec((k, COL_TILE), lambda i, j: (0, j)),
        ],
        out_specs=pl.BlockSpec((tm, COL_TILE), lambda i, j: (i, j)),
        out_shape=jax.ShapeDtypeStruct((n, c), out_dtype),
        compiler_params=_cparams("parallel", "parallel"),
        name=name,
    )(a, w)


def _sb_kernel(q_ref, k_ref, v_ref, u_ref, o_ref, acc_ref, carry_ref):
    t = ATT_TILE
    qi = pl.program_id(1)
    lane = lax.broadcasted_iota(jnp.int32, (1, LANES), 1)
    head_mask = (lane < HEAD_DIM, lane >= HEAD_DIM)
    n_heads = 2 * SB_PAIRS
    q = q_ref[...].astype(F32) * (HEAD_DIM ** -0.5 * LOG2_E)
    qh = []
    for p in range(SB_PAIRS):
        qp = q[:, p * LANES:(p + 1) * LANES]
        qh.extend(jnp.where(m, qp, 0.0).astype(BF16) for m in head_mask)
    u_tri = u_ref[...]

    acc_ref[...] = jnp.zeros_like(acc_ref)
    carry_ref[...] = jnp.zeros_like(carry_ref)

    def tile_step(kj, diagonal):
        start = pl.multiple_of(kj * t, t)
        if diagonal:
            row = lax.broadcasted_iota(jnp.int32, (t, t), 0)
            col = lax.broadcasted_iota(jnp.int32, (t, t), 1)
            past = col < row
        for idx in range(n_heads):
            p, h = divmod(idx, 2)
            k = k_ref[pl.ds(start, t), p * LANES:(p + 1) * LANES]
            v = v_ref[pl.ds(start, t), p * LANES:(p + 1) * LANES]
            z = _dot_nt(qh[idx], k)
            sp = jnp.maximum(z, 0.0) + jnp.log2(1.0 + jnp.exp2(-jnp.abs(z)))
            if diagonal:
                sp = jnp.where(past, sp, 0.0)
            carry = carry_ref[idx]
            r = _dot(sp.astype(BF16), u_tri) + jnp.concatenate([carry] * (t // LANES), axis=1)
            a = jnp.exp2(z - r)
            if diagonal:
                a = jnp.where(past, a, 0.0)
            vh = jnp.where(head_mask[h], v, jnp.zeros_like(v))
            acc_ref[idx] += _dot(a.astype(BF16), vh)
            carry = jnp.broadcast_to(r[:, 0:1], (t, LANES))
            carry_ref[idx] = carry
            carry_min = carry if idx == 0 else jnp.minimum(carry_min, carry)
        return jnp.min(carry_min)

    def cond(c):
        it, carry_min = c
        return (it < qi) & (carry_min < SB_EXIT_LOG2)

    def body(c):
        it, _ = c
        return it + 1, tile_step(qi - 1 - it, False)

    lax.while_loop(cond, body, (jnp.int32(0), tile_step(qi, True)))
    for p in range(SB_PAIRS):
        o_ref[:, p * LANES:(p + 1) * LANES] = (acc_ref[2 * p] + acc_ref[2 * p + 1]).astype(o_ref.dtype)


def _sb_call(p16, u_tri, batch, seq):
    n = batch * seq
    t = ATT_TILE
    nq = seq // t
    groups = BRANCH_W // (SB_PAIRS * LANES)
    w = SB_PAIRS * LANES
    return pl.pallas_call(
        _sb_kernel,
        grid=(batch * groups, nq),
        in_specs=[
            pl.BlockSpec((t, w), lambda g, i: ((g // groups) * nq + i, g % groups)),
            pl.BlockSpec((seq, w), lambda g, i: (g // groups, groups + g % groups)),
            pl.BlockSpec((seq, w), lambda g, i: (g // groups, 2 * groups + g % groups)),
            pl.BlockSpec((t, t), lambda g, i: (0, 0)),
        ],
        out_specs=pl.BlockSpec((t, w), lambda g, i: ((g // groups) * nq + i, g % groups)),
        out_shape=jax.ShapeDtypeStruct((n, BRANCH_W), BF16),
        scratch_shapes=[
            pltpu.VMEM((2 * SB_PAIRS, t, LANES), F32),
            pltpu.VMEM((2 * SB_PAIRS, t, LANES), F32),
        ],
        compiler_params=_cparams("parallel", "arbitrary"),
        name="stick_breaking_attention",
    )(p16, p16, p16, u_tri)


def _moba_prep_kernel(q_ref, k_ref, cos_ref, sin_ref, qg_ref, kg_ref, seg_ref, rot_ref,
                      qa_ref, ka_ref, km_ref):
    blk = pl.program_id(1)
    lane = lax.broadcasted_iota(jnp.int32, (1, LANES), 1)
    even = lane < HEAD_DIM
    cos = cos_ref[...]
    sin = sin_ref[...]
    seg = seg_ref[...]
    rot = rot_ref[...]

    def norm_rope(x, g):
        ss = _dot_exact_rhs(x * x, seg)
        xn = x * lax.rsqrt(ss * (1.0 / HEAD_DIM) + RMS_EPS) * g
        return xn * cos + _dot_exact_rhs(xn, rot) * sin

    onehot_even = jnp.where(lane == HEAD_DIM + blk, 1.0, 0.0)
    onehot_odd = jnp.where(lane == blk, 1.0, 0.0)

    km_rows = []
    for c in range(BRANCH_W // LANES):
        sl = slice(c * LANES, (c + 1) * LANES)
        qr = norm_rope(q_ref[:, sl], qg_ref[...]) * (HEAD_DIM ** -0.5 * LOG2_E)
        kr = norm_rope(k_ref[:, sl], kg_ref[...])
        qa_ref[0, 2 * c] = jnp.where(even, qr, 0.0).astype(BF16)
        qa_ref[0, 2 * c + 1] = jnp.where(even, 0.0, qr).astype(BF16)
        ka_ref[0, 2 * c] = jnp.where(even, kr, onehot_even).astype(BF16)
        ka_ref[0, 2 * c + 1] = jnp.where(even, onehot_odd, kr).astype(BF16)
        kmean = jnp.mean(kr, axis=0, keepdims=True)
        km_rows.append(jnp.where(even, kmean, 0.0))
        km_rows.append(jnp.where(even, 0.0, kmean))
    km_ref[0, 0] = jnp.concatenate(km_rows, axis=0)


def _moba_prep_call(p32, cos_t, sin_t, qg, kg, seg, rot, batch, seq):
    t = MB_BLOCK
    nb = seq // t
    h = N_ATT_HEADS
    return pl.pallas_call(
        _moba_prep_kernel,
        grid=(batch, nb),
        in_specs=[
            pl.BlockSpec((t, BRANCH_W), lambda b, i: (b * nb + i, 0)),
            pl.BlockSpec((t, BRANCH_W), lambda b, i: (b * nb + i, 1)),
            pl.BlockSpec((t, LANES), lambda b, i: (b * nb + i, 0)),
            pl.BlockSpec((t, LANES), lambda b, i: (b * nb + i, 0)),
            pl.BlockSpec((1, LANES), lambda b, i: (0, 0)),
            pl.BlockSpec((1, LANES), lambda b, i: (0, 0)),
            pl.BlockSpec((LANES, LANES), lambda b, i: (0, 0)),
            pl.BlockSpec((LANES, LANES), lambda b, i: (0, 0)),
        ],
        out_specs=[
            pl.BlockSpec((1, h, t, LANES), lambda b, i: (b, 0, i, 0)),
            pl.BlockSpec((1, h, t, LANES), lambda b, i: (b, 0, i, 0)),
            pl.BlockSpec((1, 1, h, LANES), lambda b, i: (b, i, 0, 0)),
        ],
        out_shape=[
            jax.ShapeDtypeStruct((batch, h, seq, LANES), BF16),
            jax.ShapeDtypeStruct((batch, h, seq, LANES), BF16),
            jax.ShapeDtypeStruct((batch, nb, h, LANES), F32),
        ],
        compiler_params=_cparams("parallel", "parallel"),
        name="moba_qk_prep",
    )(p32, p32, cos_t, sin_t, qg, kg, seg, rot)


def _moba_gate_kernel(qa_ref, km_ref, eye_ref, qb_ref, *, nb):
    t = ATT_TILE
    seq = qa_ref.shape[2]
    blk_row = lax.broadcasted_iota(jnp.int32, (nb, seq), 0)
    own = lax.broadcasted_iota(jnp.int32, (nb, seq), 1) // t
    for h in range(2):
        q = qa_ref[0, h]
        km_hi, km_mid, km_lo = _split3(km_ref[0, h])
        g = _dot_nt(km_hi, q) + _dot_nt(km_mid, q) + _dot_nt(km_lo, q)
        cnt = jnp.zeros((nb, seq), jnp.int32)
        for m in range(nb):
            gm = g[m:m + 1, :]
            beats = (gm > g) | ((gm == g) & (m < blk_row))
            cnt = cnt + jnp.where(beats & (m < own), 1, 0)
        keep = ((blk_row < own) & (cnt < MB_TOPK)) | (blk_row == own)
        bias = jnp.where(keep, 0.0, NEG_BIG)
        aux_lo = HEAD_DIM if h == 0 else 0
        pieces = []
        if aux_lo:
            pieces.append(jnp.zeros((aux_lo, seq), F32))
        pieces.append(bias)
        pieces.append(jnp.zeros((LANES - aux_lo - nb, seq), F32))
        bias_rows = jnp.concatenate(pieces, axis=0).astype(BF16)
        for j in range(seq // t):
            sl = slice(j * t, (j + 1) * t)
            bias_cols = _dot_nt(eye_ref[...], bias_rows[:, sl])
            qb_ref[0, h, sl, :] = (q[sl, :].astype(F32) + bias_cols).astype(BF16)


def _moba_gate_call(qa, km, eye, batch, seq):
    nb = seq // MB_BLOCK
    pairs = N_ATT_HEADS // 2
    spec = pl.BlockSpec((1, 2, seq, LANES), lambda b, p: (b, p, 0, 0))
    return pl.pallas_call(
        functools.partial(_moba_gate_kernel, nb=nb),
        grid=(batch, pairs),
        in_specs=[
            spec,
            pl.BlockSpec((1, 2, nb, LANES), lambda b, p: (b, p, 0, 0)),
            pl.BlockSpec((ATT_TILE, ATT_TILE), lambda b, p: (0, 0)),
        ],
        out_specs=spec,
        out_shape=jax.ShapeDtypeStruct(qa.shape, BF16),
        compiler_params=_cparams("parallel", "parallel"),
        name="moba_block_gate",
    )(qa, km, eye)


def _moba_kernel(qb_ref, ka_ref, v_ref, o_ref, s_scr, mx_ref, l_ref, acc_ref):
    t = ATT_TILE
    own = pl.program_id(1)
    n_heads = 2 * MB_PAIRS
    lane = lax.broadcasted_iota(jnp.int32, (1, LANES), 1)
    head_mask = (lane < HEAD_DIM, lane >= HEAD_DIM)

    def score_step(kj, diagonal):
        start = pl.multiple_of(kj * t, t)
        for h in range(n_heads):
            s = _dot_nt(qb_ref[0, h], ka_ref[0, h, pl.ds(start, t), :])
            if diagonal:
                row = lax.broadcasted_iota(jnp.int32, (t, t), 0)
                col = lax.broadcasted_iota(jnp.int32, (t, t), 1)
                s = jnp.where(col <= row, s, NEG_BIG)
                mx_ref[h] = s
            else:
                mx_ref[h] = jnp.maximum(mx_ref[h], s)
            s_scr[h, kj] = s

    score_step(own, True)

    def score_body(it, c):
        score_step(own - 1 - it, False)
        return c

    lax.fori_loop(0, own, score_body, 0)

    def lane_fold(x, op):
        parts = [x[:, c * LANES:(c + 1) * LANES] for c in range(t // LANES)]
        return functools.reduce(op, parts)

    for h in range(n_heads):
        row_max = jnp.max(lane_fold(mx_ref[h], jnp.maximum), axis=-1, keepdims=True)
        mx_ref[h] = jnp.broadcast_to(row_max, (t, t))
    acc_ref[...] = jnp.zeros_like(acc_ref)
    l_ref[...] = jnp.zeros_like(l_ref)

    def value_body(kj, c):
        start = pl.multiple_of(kj * t, t)
        for h in range(n_heads):
            pr, e = divmod(h, 2)
            p = jnp.exp2(s_scr[h, kj] - mx_ref[h])
            l_ref[h] += p
            v = v_ref[pl.ds(start, t), pr * LANES:(pr + 1) * LANES]
            vh = jnp.where(head_mask[e], v, jnp.zeros_like(v))
            acc_ref[h] += _dot(p.astype(BF16), vh)
        return c

    lax.fori_loop(0, own + 1, value_body, 0)

    for pr in range(MB_PAIRS):
        inv0 = 1.0 / jnp.sum(lane_fold(l_ref[2 * pr], jnp.add), axis=-1, keepdims=True)
        inv1 = 1.0 / jnp.sum(lane_fold(l_ref[2 * pr + 1], jnp.add), axis=-1, keepdims=True)
        out = acc_ref[2 * pr] * inv0 + acc_ref[2 * pr + 1] * inv1
        o_ref[:, pr * LANES:(pr + 1) * LANES] = out.astype(o_ref.dtype)


def _moba_call(qb, ka, p16, batch, seq):
    n = batch * seq
    t = ATT_TILE
    nq = seq // t
    nb = seq // MB_BLOCK
    groups = BRANCH_W // (MB_PAIRS * LANES)
    nh = 2 * MB_PAIRS
    w = MB_PAIRS * LANES
    v_col0 = 3 * BRANCH_W // w
    return pl.pallas_call(
        _moba_kernel,
        grid=(batch * groups, nq),
        in_specs=[
            pl.BlockSpec((1, nh, t, LANES), lambda g, i: (g // groups, g % groups, i, 0)),
            pl.BlockSpec((1, nh, seq, LANES), lambda g, i: (g // groups, g % groups, 0, 0)),
            pl.BlockSpec((seq, w), lambda g, i: (g // groups, v_col0 + g % groups)),
        ],
        out_specs=pl.BlockSpec((t, w), lambda g, i: ((g // groups) * nq + i, g % groups)),
        out_shape=jax.ShapeDtypeStruct((n, BRANCH_W), BF16),
        scratch_shapes=[
            pltpu.VMEM((nh, nb, t, t), F32),
            pltpu.VMEM((nh, t, t), F32),
            pltpu.VMEM((nh, t, t), F32),
            pltpu.VMEM((nh, t, LANES), F32),
        ],
        compiler_params=_cparams("parallel", "arbitrary"),
        name="moba_attention",
    )(qb, ka, p16)


def _mlstm_kernel(u_ref, v_ref, o_ref, ifc_ref, ift_ref, cw_ref, ltri_ref, utri_ref, y_ref,
                  xbuf, c_ref, n_ref, m_ref):
    L = ML_CHUNK
    W = BRANCH_W
    halo = SUBLANES

    @pl.when(pl.program_id(1) == 0)
    def _():
        xbuf[0:halo, :] = jnp.zeros((halo, 2 * W), F32)
        c_ref[...] = jnp.zeros_like(c_ref)
        n_ref[...] = jnp.zeros_like(n_ref)
        m_ref[...] = jnp.zeros_like(m_ref)

    xbuf[halo:, :] = u_ref[...]
    conv = jnp.zeros((L, 2 * W), F32)
    for j in range(CONV_W):
        off = halo - (CONV_W - 1) + j
        conv = conv + cw_ref[j:j + 1, :] * xbuf[off:off + L, :]
    xbuf[0:halo, :] = u_ref[L - halo:, :]
    qk = conv * _sigmoid(conv)

    ift = ift_ref[0]
    lf_rows = _log_sigmoid(ift)
    bcum_rows = _dot_exact_rhs(lf_rows, utri_ref[...])
    lf_cols = _log_sigmoid(ifc_ref[...])
    bcum_cols = _dot_exact_lhs(ltri_ref[...], lf_cols)

    row = lax.broadcasted_iota(jnp.int32, (L, L), 0)
    col = lax.broadcasted_iota(jnp.int32, (L, L), 1)
    causal = col <= row

    for h in range(ML_HEADS):
        sl = slice(h * ML_HDIM, (h + 1) * ML_HDIM)
        q = qk[:, sl].astype(BF16)
        k = (qk[:, W + h * ML_HDIM:W + (h + 1) * ML_HDIM] * (ML_HDIM ** -0.5)).astype(BF16)
        v = v_ref[:, sl]
        a_row = ift[h:h + 1, :] - bcum_rows[ML_HEADS + h:ML_HEADS + h + 1, :]
        bc = bcum_cols[:, ML_HEADS + h:ML_HEADS + h + 1]
        m_prev = m_ref[h, 0:1, 0:1]

        a_mat = jnp.where(causal, a_row, NEG_BIG)
        mu = jnp.maximum(jnp.max(a_mat, axis=-1, keepdims=True), m_prev)
        w_intra = jnp.exp(a_mat - mu)
        w_inter = jnp.exp(m_prev - mu)
        sc = _dot_nt(q, k) * w_intra
        c_prev = c_ref[h]
        n_prev = n_ref[h, 0:1, :]
        num = _dot(sc.astype(BF16), v) + w_inter * _dot_nt(q, c_prev.astype(BF16))
        qn = jnp.sum(q.astype(F32) * n_prev, axis=-1, keepdims=True)
        den = jnp.sum(sc, axis=-1, keepdims=True) + w_inter * qn
        m_t = bc + mu
        hs = num / jnp.maximum(jnp.abs(den), jnp.exp(-m_t))
        gate = _sigmoid(o_ref[:, sl].astype(F32))
        y_ref[:, sl] = (hs * gate).astype(y_ref.dtype)

        mu_last = mu[L - 1:L, :]
        m_new = bc[L - 1:L, :] + mu_last
        decay = jnp.exp(m_prev - mu_last)
        wk_row = jnp.exp(a_row - mu_last)
        vt = v.astype(F32).T
        c_ref[h] = decay * c_prev + _dot((vt * wk_row).astype(BF16), k)
        wk8 = jnp.broadcast_to(wk_row, (SUBLANES, L)).astype(BF16)
        n_ref[h] = decay * n_ref[h] + _dot(wk8, k)
        m_ref[h] = jnp.broadcast_to(m_new, (SUBLANES, LANES))


def _mlstm_call(p32, p16, ifc, ift, conv_w, ltri, utri, batch, seq):
    n = batch * seq
    L = ML_CHUNK
    nc = seq // L
    W = BRANCH_W
    return pl.pallas_call(
        _mlstm_kernel,
        grid=(batch, nc),
        in_specs=[
            pl.BlockSpec((L, 2 * W), lambda b, i: (b * nc + i, 1)),
            pl.BlockSpec((L, W), lambda b, i: (b * nc + i, 4)),
            pl.BlockSpec((L, W), lambda b, i: (b * nc + i, 5)),
            pl.BlockSpec((L, LANES), lambda b, i: (b * nc + i, 0)),
            pl.BlockSpec((1, SUBLANES, L), lambda b, i: (b, 0, i)),
            pl.BlockSpec((CONV_W, 2 * W), lambda b, i: (0, 0)),
            pl.BlockSpec((L, L), lambda b, i: (0, 0)),
            pl.BlockSpec((L, L), lambda b, i: (0, 0)),
        ],
        out_specs=pl.BlockSpec((L, W), lambda b, i: (b * nc + i, 0)),
        out_shape=jax.ShapeDtypeStruct((n, W), BF16),
        scratch_shapes=[
            pltpu.VMEM((L + SUBLANES, 2 * W), F32),
            pltpu.VMEM((ML_HEADS, ML_HDIM, ML_HDIM), F32),
            pltpu.VMEM((ML_HEADS, SUBLANES, ML_HDIM), F32),
            pltpu.VMEM((ML_HEADS, SUBLANES, LANES), F32),
        ],
        compiler_params=_cparams("parallel", "arbitrary"),
        name="mlstm",
    )(p32, p16, p16, ifc, ift, conv_w, ltri, utri)


def _merge_kernel(ysb_ref, ymb_ref, yml_ref, gl_ref, x_ref, wb_ref, wo_ref, g2_ref, wr_ref, br_ref,
                  h_ref, xn_ref, route_ref, route_t_ref):
    merged = jnp.zeros((ROW_TILE, D_MODEL), F32)
    for b, y_ref in enumerate((ysb_ref, ymb_ref, yml_ref)):
        gate = _sigmoid(gl_ref[:, b * D_MODEL:(b + 1) * D_MODEL].astype(F32))
        merged = merged + gate * _dot(y_ref[...], wb_ref[b])
    hres = x_ref[...] + _dot(merged.astype(BF16), wo_ref[...])
    h_ref[...] = hres

    ms = jnp.mean(hres * hres, axis=-1, keepdims=True)
    xn = (hres * lax.rsqrt(ms + RMS_EPS) * g2_ref[...]).astype(BF16)
    xn_ref[...] = xn

    logits = _dot(xn, wr_ref[...]) + br_ref[...]
    lane = lax.broadcasted_iota(jnp.int32, (ROW_TILE, LANES), 1).astype(F32)
    far = float(LANES)
    is_g = lane < N_GROUPS
    gl = jnp.where(is_g, logits, NEG_BIG)
    gmax = jnp.max(gl, axis=-1, keepdims=True)
    gsum = jnp.sum(jnp.where(is_g, jnp.exp(gl - gmax), 0.0), axis=-1, keepdims=True)
    g_w = 1.0 / gsum
    g_idx = jnp.min(jnp.where(is_g & (gl == gmax), lane, far), axis=-1, keepdims=True)
    e_lo = N_GROUPS + g_idx * EXPERTS_PER_GROUP
    in_grp = (lane >= e_lo) & (lane < e_lo + EXPERTS_PER_GROUP)
    el = jnp.where(in_grp, logits, NEG_BIG)
    l1 = jnp.max(el, axis=-1, keepdims=True)
    i1 = jnp.min(jnp.where(in_grp & (el == l1), lane, far), axis=-1, keepdims=True)
    el2 = jnp.where(lane == i1, NEG_BIG, el)
    l2 = jnp.max(el2, axis=-1, keepdims=True)
    i2 = jnp.min(jnp.where(in_grp & (el2 == l2), lane, far), axis=-1, keepdims=True)
    p2 = jnp.exp(l2 - l1)
    w1 = g_w / (1.0 + p2)
    w2 = g_w * p2 / (1.0 + p2)
    slot = lane + e_lo
    cw = jnp.where(slot == i1, w1, 0.0) + jnp.where(slot == i2, w2, 0.0)
    cw = jnp.where(lane < EXPERTS_PER_GROUP, cw, 0.0)
    route = jnp.where(lane == EXPERTS_PER_GROUP, g_idx, cw)
    route_ref[...] = route
    route_t_ref[...] = route.T[0:2 * SUBLANES, :]


def _merge_call(ysb, ymb, yml, p16, x, wb, wo, g2, wr, br):
    n = x.shape[0]
    row = lambda i: (i, 0)
    const2 = lambda i: (0, 0)
    return pl.pallas_call(
        _merge_kernel,
        grid=(n // ROW_TILE,),
        in_specs=[
            pl.BlockSpec((ROW_TILE, BRANCH_W), row),
            pl.BlockSpec((ROW_TILE, BRANCH_W), row),
            pl.BlockSpec((ROW_TILE, BRANCH_W), row),
            pl.BlockSpec((ROW_TILE, N_BRANCH * D_MODEL), lambda i: (i, 1)),
            pl.BlockSpec((ROW_TILE, D_MODEL), row),
            pl.BlockSpec((N_BRANCH, BRANCH_W, D_MODEL), lambda i: (0, 0, 0)),
            pl.BlockSpec((D_MODEL, D_MODEL), const2),
            pl.BlockSpec((1, D_MODEL), const2),
            pl.BlockSpec((D_MODEL, LANES), const2),
            pl.BlockSpec((1, LANES), const2),
        ],
        out_specs=[
            pl.BlockSpec((ROW_TILE, D_MODEL), row),
            pl.BlockSpec((ROW_TILE, D_MODEL), row),
            pl.BlockSpec((ROW_TILE, LANES), row),
            pl.BlockSpec((2 * SUBLANES, ROW_TILE), lambda i: (0, i)),
        ],
        out_shape=[
            jax.ShapeDtypeStruct((n, D_MODEL), F32),
            jax.ShapeDtypeStruct((n, D_MODEL), BF16),
            jax.ShapeDtypeStruct((n, LANES), F32),
            jax.ShapeDtypeStruct((2 * SUBLANES, n), F32),
        ],
        compiler_params=_cparams("parallel"),
        name="merge_outproj_router",
    )(ysb, ymb, yml, p16, x, wb, wo, g2, wr, br)


def _moe_kernel(x_ref, rt_ref, rtt_ref, lx_ref, w1_ref, w3_ref, w2_ref, y_ref):
    tb = MOE_BLOCK
    sub = LANES
    grp = pl.program_id(1).astype(F32)

    @pl.when(pl.program_id(1) == 0)
    def _():
        y_ref[...] = jnp.zeros_like(y_ref)

    route = rt_ref[...]
    gid_lane = EXPERTS_PER_GROUP
    in_col = route[:, gid_lane:gid_lane + 1] == grp
    in_row = rtt_ref[gid_lane:gid_lane + 1, :] == grp
    ones_col = jnp.broadcast_to(jnp.where(in_col, 1.0, 0.0), (tb, LANES)).astype(BF16)
    ones_row = jnp.broadcast_to(jnp.where(in_row, 1.0, 0.0), (SUBLANES, tb)).astype(BF16)
    lx = lx_ref[...]
    key_col = jnp.where(in_col, _dot(lx, ones_col), -1.0)
    key_row = jnp.where(in_row, _dot_nt(ones_row, lx)[0:1, :], -1.0)
    count = jnp.sum(jnp.where(in_row, 1.0, 0.0)).astype(jnp.int32)
    r_hi, r_mid, r_lo = _split3(route)

    def sub_tile(s, c):
        base = (s * sub).astype(F32)
        slot_rows = lax.broadcasted_iota(jnp.int32, (sub, tb), 0).astype(F32) + base
        pick = jnp.where(key_row == slot_rows, 1.0, 0.0).astype(BF16)
        xs = _dot(pick, x_ref[...]).astype(BF16)
        cw = _dot(pick, r_hi) + _dot(pick, r_mid) + _dot(pick, r_lo)
        y = jnp.zeros((sub, D_MODEL), F32)
        for e in range(EXPERTS_PER_GROUP):
            a = _dot(xs, w1_ref[e])
            b = _dot(xs, w3_ref[e])
            hid = a * _sigmoid(a) * b * cw[:, e:e + 1]
            y = y + _dot(hid.astype(BF16), w2_ref[e])
        slot_cols = lax.broadcasted_iota(jnp.int32, (tb, sub), 1).astype(F32) + base
        put = jnp.where(key_col == slot_cols, 1.0, 0.0).astype(BF16)
        y_ref[...] += _dot(put, y.astype(BF16))
        return c

    lax.fori_loop(0, (count + sub - 1) // sub, sub_tile, 0)


def _moe_call(xn2, route, route_t, lx, w1b, w3b, w2b):
    n = xn2.shape[0]
    tb = MOE_BLOCK
    e = EXPERTS_PER_GROUP
    return pl.pallas_call(
        _moe_kernel,
        grid=(n // tb, N_GROUPS),
        in_specs=[
            pl.BlockSpec((tb, D_MODEL), lambda i, g: (i, 0)),
            pl.BlockSpec((tb, LANES), lambda i, g: (i, 0)),
            pl.BlockSpec((2 * SUBLANES, tb), lambda i, g: (0, i)),
            pl.BlockSpec((tb, tb), lambda i, g: (0, 0)),
            pl.BlockSpec((e, D_MODEL, D_EXPERT), lambda i, g: (g, 0, 0)),
            pl.BlockSpec((e, D_MODEL, D_EXPERT), lambda i, g: (g, 0, 0)),
            pl.BlockSpec((e, D_EXPERT, D_MODEL), lambda i, g: (g, 0, 0)),
        ],
        out_specs=pl.BlockSpec((tb, D_MODEL), lambda i, g: (i, 0)),
        out_shape=jax.ShapeDtypeStruct((n, D_MODEL), F32),
        compiler_params=_cparams("parallel", "arbitrary"),
        name="moe_group_experts",
    )(xn2, route, route_t, lx, w1b, w3b, w2b)


def _add_kernel(a_ref, b_ref, o_ref):
    o_ref[...] = a_ref[...] + b_ref[...]


def _add_call(a, b):
    n, d = a.shape
    spec = pl.BlockSpec((2 * ROW_TILE, d), lambda i: (i, 0))
    return pl.pallas_call(
        _add_kernel,
        grid=(n // (2 * ROW_TILE),),
        in_specs=[spec, spec],
        out_specs=spec,
        out_shape=jax.ShapeDtypeStruct((n, d), a.dtype),
        compiler_params=_cparams("parallel"),
        name="residual_add",
    )(a, b)


def _tables(positions):
    t = ATT_TILE
    r = jnp.arange(t)
    u_tri = (r[:, None] >= r[None, :]).astype(BF16)
    eye = jnp.eye(t, dtype=BF16)
    L = ML_CHUNK
    rl = jnp.arange(L)
    ltri = (rl[:, None] >= rl[None, :]).astype(BF16)
    utri = ltri.T

    li = jnp.arange(LANES)
    seg = ((li[:, None] // HEAD_DIM) == (li[None, :] // HEAD_DIM)).astype(BF16)
    half = ROPE_DIM // 2
    src, dst = li[:, None], li[None, :]
    d_in = dst % HEAD_DIM
    rot = jnp.where((d_in < half) & (src == dst + half), -1.0,
                    jnp.where((d_in >= half) & (d_in < ROPE_DIM) & (src == dst - half), 1.0, 0.0)
                    ).astype(BF16)

    inv_freq = jnp.power(jnp.float32(ROPE_THETA), -jnp.arange(half, dtype=F32) / half)
    lane_in = li % HEAD_DIM
    ang = positions.reshape(-1).astype(F32)[:, None] * inv_freq[lane_in % half][None, :]
    rotated = (lane_in < ROPE_DIM)[None, :]
    cos_t = jnp.where(rotated, jnp.cos(ang), 1.0)
    sin_t = jnp.where(rotated, jnp.sin(ang), 0.0)
    rb = jnp.arange(MOE_BLOCK)
    lx = (rb[None, :] < rb[:, None]).astype(BF16)
    return dict(u_tri=u_tri, eye=eye, ltri=ltri, utri=utri, seg=seg, rot=rot, cos=cos_t, sin=sin_t,
                lx=lx)


def kernel(x, positions, norm1_g, w_in, qn_g, kn_g, conv_w, ml_gate_b, w_branch, w_out, norm2_g,
           w_rg, b_rg, w_re, b_re, w1, w3, w2):
    batch, seq, d = x.shape
    n = batch * seq
    W = BRANCH_W
    depth = w_in.shape[0]
    tb = _tables(positions)
    xf = x.reshape(n, d)
    y_moe = None

    for l in range(depth):
        wl = w_in[l]
        w16 = jnp.concatenate(
            [wl[:, 0:3 * W], wl[:, 5 * W:6 * W], wl[:, 8 * W:10 * W], wl[:, 10 * W + 2 * ML_HEADS:]],
            axis=1).astype(BF16)
        w32 = jnp.concatenate([wl[:, 3 * W:5 * W], wl[:, 6 * W:8 * W]], axis=1).astype(BF16)
        n_if = 2 * ML_HEADS
        wif = jnp.pad(wl[:, 10 * W:10 * W + n_if], ((0, 0), (0, LANES - n_if))).astype(BF16)
        bif = jnp.pad(ml_gate_b[l], (0, LANES - n_if)).reshape(1, LANES)

        xf, xn, ifc = _norm_call(xf, y_moe, norm1_g[l].reshape(1, d), wif, bif)
        p16 = _matmul_call(xn, w16, BF16, "inproj_bf16")
        p32 = _matmul_call(xn, w32, F32, "inproj_f32")

        y_sb = _sb_call(p16, tb["u_tri"], batch, seq)

        qg = jnp.tile(qn_g[l], LANES // HEAD_DIM).reshape(1, LANES)
        kg = jnp.tile(kn_g[l], LANES // HEAD_DIM).reshape(1, LANES)
        qa, ka, km = _moba_prep_call(p32, tb["cos"], tb["sin"], qg, kg, tb["seg"], tb["rot"],
                                     batch, seq)
        qb = _moba_gate_call(qa, km.transpose(0, 2, 1, 3), tb["eye"], batch, seq)
        y_mb = _moba_call(qb, ka, p16, batch, seq)

        ift = ifc[:, :SUBLANES].reshape(batch, seq, SUBLANES).transpose(0, 2, 1)
        y_ml = _mlstm_call(p32, p16, ifc, ift, conv_w[l], tb["ltri"], tb["utri"], batch, seq)

        wr = jnp.pad(jnp.concatenate([w_rg[l], w_re[l]], axis=1),
                     ((0, 0), (0, LANES - N_GROUPS - N_EXPERTS))).astype(BF16)
        br = jnp.pad(jnp.concatenate([b_rg[l], b_re[l]]),
                     (0, LANES - N_GROUPS - N_EXPERTS)).reshape(1, LANES)
        xf, xn2, route, route_t = _merge_call(
            y_sb, y_mb, y_ml, p16, xf, w_branch[l].astype(BF16), w_out[l].astype(BF16),
            norm2_g[l].reshape(1, d), wr, br)
        y_moe = _moe_call(xn2, route, route_t, tb["lx"], w1[l].astype(BF16), w3[l].astype(BF16),
                          w2[l].astype(BF16))

    return _add_call(xf, y_moe).reshape(batch, seq, d)
```

```python
import functools

import jax
import jax.numpy as jnp
from jax import lax
from jax.experimental import pallas as pl
from jax.experimental.pallas import tpu as pltpu

D_MODEL = 1024
BRANCH_W = D_MODEL // 2
N_BRANCH = 3
HEAD_DIM = 64
N_ATT_HEADS = BRANCH_W // HEAD_DIM
MB_BLOCK = 256
MB_TOPK = 3
ROPE_THETA = 500000.0
ROPE_DIM = HEAD_DIM // 4
ML_HEADS = 4
ML_HDIM = BRANCH_W // ML_HEADS
CONV_W = 4
N_GROUPS = 4
EXPERTS_PER_GROUP = 8
N_EXPERTS = N_GROUPS * EXPERTS_PER_GROUP
D_EXPERT = D_MODEL // 4
RMS_EPS = 1e-6

LANES = 128
SUBLANES = 8
VMEM_LIMIT_BYTES = 56 * 1024 * 1024

ROW_TILE = 512
COL_TILE = 1024
ATT_TILE = 256
ML_CHUNK = 256
MOE_BLOCK = 1024

LOG2_E = 1.4426950408889634
SB_PAIRS = 4
SB_EXIT_LOG2 = 160.0
MB_PAIRS = 4
NEG_BIG = -(2.0 ** 100)

F32 = jnp.float32
BF16 = jnp.bfloat16


def _cparams(*sem):
    return pltpu.CompilerParams(dimension_semantics=sem, vmem_limit_bytes=VMEM_LIMIT_BYTES)


def _dot(a, b):
    return jnp.dot(a, b, preferred_element_type=F32)


def _dot_nt(a, b):
    return lax.dot_general(a, b, (((1,), (1,)), ((), ())), preferred_element_type=F32)


def _split3(x):
    hi = x.astype(BF16)
    r1 = x - hi.astype(F32)
    mid = r1.astype(BF16)
    lo = (r1 - mid.astype(F32)).astype(BF16)
    return hi, mid, lo


def _dot_exact_rhs(x, m):
    hi, mid, lo = _split3(x)
    return _dot(hi, m) + _dot(mid, m) + _dot(lo, m)


def _dot_exact_lhs(m, x):
    hi, mid, lo = _split3(x)
    return _dot(m, hi) + _dot(m, mid) + _dot(m, lo)


def _log_sigmoid(x):
    return jnp.minimum(x, 0.0) - jnp.log(1.0 + jnp.exp(-jnp.abs(x)))


def _sigmoid(x):
    return 1.0 / (1.0 + jnp.exp(-x))


def _norm_kernel(x_ref, g_ref, wif_ref, bif_ref, xn_ref, if_ref):
    x = x_ref[...]
    ms = jnp.mean(x * x, axis=-1, keepdims=True)
    xn = (x * lax.rsqrt(ms + RMS_EPS) * g_ref[...]).astype(BF16)
    xn_ref[...] = xn
    if_ref[...] = _dot(xn, wif_ref[...]) + bif_ref[...]


def _add_norm_kernel(h_ref, y_ref, g_ref, wif_ref, bif_ref, x_ref, xn_ref, if_ref):
    x_ref[...] = h_ref[...] + y_ref[...]
    _norm_kernel(x_ref, g_ref, wif_ref, bif_ref, xn_ref, if_ref)


def _norm_call(x, y, g, wif, bif):
    n = x.shape[0]
    row = pl.BlockSpec((ROW_TILE, D_MODEL), lambda i: (i, 0))
    w_specs = [
        pl.BlockSpec((1, D_MODEL), lambda i: (0, 0)),
        pl.BlockSpec((D_MODEL, LANES), lambda i: (0, 0)),
        pl.BlockSpec((1, LANES), lambda i: (0, 0)),
    ]
    outs = [row, pl.BlockSpec((ROW_TILE, LANES), lambda i: (i, 0))]
    shapes = [jax.ShapeDtypeStruct((n, D_MODEL), BF16), jax.ShapeDtypeStruct((n, LANES), F32)]
    if y is None:
        xn, ifc = pl.pallas_call(
            _norm_kernel, grid=(n // ROW_TILE,), in_specs=[row] + w_specs, out_specs=outs,
            out_shape=shapes, compiler_params=_cparams("parallel"), name="norm_gateproj",
        )(x, g, wif, bif)
        return x, xn, ifc
    return pl.pallas_call(
        _add_norm_kernel, grid=(n // ROW_TILE,), in_specs=[row, row] + w_specs,
        out_specs=[row] + outs, out_shape=[jax.ShapeDtypeStruct((n, D_MODEL), F32)] + shapes,
        compiler_params=_cparams("parallel"), name="add_norm_gateproj",
    )(x, y, g, wif, bif)


def _matmul_kernel(a_ref, w_ref, o_ref):
    o_ref[...] = _dot(a_ref[...], w_ref[...]).astype(o_ref.dtype)


def _matmul_call(a, w, out_dtype, name):
    n, k = a.shape
    c = w.shape[1]
    tm = 2 * ROW_TILE
    return pl.pallas_call(
        _matmul_kernel,
        grid=(n // tm, c // COL_TILE),
        in_specs=[
            pl.BlockSpec((tm, k), lambda i, j: (i, 0)),
            pl.BlockSpec((k, COL_TILE), lambda i, j: (0, j)),
        ],
        out_specs=pl.BlockSpec((tm, COL_TILE), lambda i, j: (i, j)),
        out_shape=jax.ShapeDtypeStruct((n, c), out_dtype),
        compiler_params=_cparams("parallel", "parallel"),
        name=name,
    )(a, w)


def _sb_kernel(q_ref, k_ref, v_ref, u_ref, o_ref, acc_ref, carry_ref):
    t = ATT_TILE
    qi = pl.program_id(1)
    lane = lax.broadcasted_iota(jnp.int32, (1, LANES), 1)
    head_mask = (lane < HEAD_DIM, lane >= HEAD_DIM)
    n_heads = 2 * SB_PAIRS
    q = q_ref[...].astype(F32) * (HEAD_DIM ** -0.5 * LOG2_E)
    qh = []
    for p in range(SB_PAIRS):
        qp = q[:, p * LANES:(p + 1) * LANES]
        qh.extend(jnp.where(m, qp, 0.0).astype(BF16) for m in head_mask)
    u_tri = u_ref[...]

    acc_ref[...] = jnp.zeros_like(acc_ref)
    carry_ref[...] = jnp.zeros_like(carry_ref)

    def tile_step(kj, diagonal):
        start = pl.multiple_of(kj * t, t)
        if diagonal:
            row = lax.broadcasted_iota(jnp.int32, (t, t), 0)
            col = lax.broadcasted_iota(jnp.int32, (t, t), 1)
            past = col < row
        heads = range(n_heads)
        zs = [_dot_nt(qh[i], k_ref[pl.ds(start, t), (i // 2) * LANES:(i // 2 + 1) * LANES])
              for i in heads]
        sps = [jnp.maximum(z, 0.0) + jnp.log2(1.0 + jnp.exp2(-jnp.abs(z))) for z in zs]
        if diagonal:
            sps = [jnp.where(past, sp, 0.0) for sp in sps]
        rs = [_dot(sps[i].astype(BF16), u_tri)
              + jnp.concatenate([carry_ref[i]] * (t // LANES), axis=1) for i in heads]
        ws = [jnp.exp2(zs[i] - rs[i]) for i in heads]
        if diagonal:
            ws = [jnp.where(past, w, 0.0) for w in ws]
        carry_min = None
        for i in heads:
            v = v_ref[pl.ds(start, t), (i // 2) * LANES:(i // 2 + 1) * LANES]
            vh = jnp.where(head_mask[i % 2], v, jnp.zeros_like(v))
            acc_ref[i] += _dot(ws[i].astype(BF16), vh)
            carry = jnp.broadcast_to(rs[i][:, 0:1], (t, LANES))
            carry_ref[i] = carry
            carry_min = carry if i == 0 else jnp.minimum(carry_min, carry)
        return jnp.min(carry_min)

    def cond(c):
        it, carry_min = c
        return (it < qi) & (carry_min < SB_EXIT_LOG2)

    def body(c):
        it, _ = c
        return it + 1, tile_step(qi - 1 - it, False)

    lax.while_loop(cond, body, (jnp.int32(0), tile_step(qi, True)))
    for p in range(SB_PAIRS):
        o_ref[:, p * LANES:(p + 1) * LANES] = (acc_ref[2 * p] + acc_ref[2 * p + 1]).astype(o_ref.dtype)


def _sb_call(p16, u_tri, batch, seq):
    n = batch * seq
    t = ATT_TILE
    nq = seq // t
    groups = BRANCH_W // (SB_PAIRS * LANES)
    w = SB_PAIRS * LANES
    return pl.pallas_call(
        _sb_kernel,
        grid=(batch * groups, nq),
        in_specs=[
            pl.BlockSpec((t, w), lambda g, i: ((g // groups) * nq + i, g % groups)),
            pl.BlockSpec((seq, w), lambda g, i: (g // groups, groups + g % groups)),
            pl.BlockSpec((seq, w), lambda g, i: (g // groups, 2 * groups + g % groups)),
            pl.BlockSpec((t, t), lambda g, i: (0, 0)),
        ],
        out_specs=pl.BlockSpec((t, w), lambda g, i: ((g // groups) * nq + i, g % groups)),
        out_shape=jax.ShapeDtypeStruct((n, BRANCH_W), BF16),
        scratch_shapes=[
            pltpu.VMEM((2 * SB_PAIRS, t, LANES), F32),
            pltpu.VMEM((2 * SB_PAIRS, t, LANES), F32),
        ],
        compiler_params=_cparams("parallel", "arbitrary"),
        name="stick_breaking_attention",
    )(p16, p16, p16, u_tri)


def _moba_prep_kernel(q_ref, k_ref, cos_ref, sin_ref, qg_ref, kg_ref, seg_ref, rot_ref,
                      qa_ref, ka_ref, km_ref):
    blk = pl.program_id(1)
    lane = lax.broadcasted_iota(jnp.int32, (1, LANES), 1)
    even = lane < HEAD_DIM
    cos = cos_ref[...]
    sin = sin_ref[...]
    seg = seg_ref[...]
    rot = rot_ref[...]

    def norm_rope(x, g):
        ss = _dot_exact_rhs(x * x, seg)
        xn = x * lax.rsqrt(ss * (1.0 / HEAD_DIM) + RMS_EPS) * g
        return xn * cos + _dot_exact_rhs(xn, rot) * sin

    onehot_even = jnp.where(lane == HEAD_DIM + blk, 1.0, 0.0)
    onehot_odd = jnp.where(lane == blk, 1.0, 0.0)

    km_rows = []
    for c in range(BRANCH_W // LANES):
        sl = slice(c * LANES, (c + 1) * LANES)
        qr = norm_rope(q_ref[:, sl], qg_ref[...]) * (HEAD_DIM ** -0.5 * LOG2_E)
        kr = norm_rope(k_ref[:, sl], kg_ref[...])
        qa_ref[0, 2 * c] = jnp.where(even, qr, 0.0).astype(BF16)
        qa_ref[0, 2 * c + 1] = jnp.where(even, 0.0, qr).astype(BF16)
        ka_ref[0, 2 * c] = jnp.where(even, kr, onehot_even).astype(BF16)
        ka_ref[0, 2 * c + 1] = jnp.where(even, onehot_odd, kr).astype(BF16)
        kmean = jnp.mean(kr, axis=0, keepdims=True)
        km_rows.append(jnp.where(even, kmean, 0.0))
        km_rows.append(jnp.where(even, 0.0, kmean))
    km_ref[0, 0] = jnp.concatenate(km_rows, axis=0)


def _moba_prep_call(p32, cos_t, sin_t, qg, kg, seg, rot, batch, seq):
    t = MB_BLOCK
    nb = seq // t
    h = N_ATT_HEADS
    return pl.pallas_call(
        _moba_prep_kernel,
        grid=(batch, nb),
        in_specs=[
            pl.BlockSpec((t, BRANCH_W), lambda b, i: (b * nb + i, 0)),
            pl.BlockSpec((t, BRANCH_W), lambda b, i: (b * nb + i, 1)),
            pl.BlockSpec((t, LANES), lambda b, i: (b * nb + i, 0)),
            pl.BlockSpec((t, LANES), lambda b, i: (b * nb + i, 0)),
            pl.BlockSpec((1, LANES), lambda b, i: (0, 0)),
            pl.BlockSpec((1, LANES), lambda b, i: (0, 0)),
            pl.BlockSpec((LANES, LANES), lambda b, i: (0, 0)),
            pl.BlockSpec((LANES, LANES), lambda b, i: (0, 0)),
        ],
        out_specs=[
            pl.BlockSpec((1, h, t, LANES), lambda b, i: (b, 0, i, 0)),
            pl.BlockSpec((1, h, t, LANES), lambda b, i: (b, 0, i, 0)),
            pl.BlockSpec((1, 1, h, LANES), lambda b, i: (b, i, 0, 0)),
        ],
        out_shape=[
            jax.ShapeDtypeStruct((batch, h, seq, LANES), BF16),
            jax.ShapeDtypeStruct((batch, h, seq, LANES), BF16),
            jax.ShapeDtypeStruct((batch, nb, h, LANES), F32),
        ],
        compiler_params=_cparams("parallel", "parallel"),
        name="moba_qk_prep",
    )(p32, p32, cos_t, sin_t, qg, kg, seg, rot)


def _moba_gate_kernel(qa_ref, km_ref, eye_ref, qb_ref, *, nb):
    t = ATT_TILE
    seq = qa_ref.shape[2]
    blk_row = lax.broadcasted_iota(jnp.int32, (nb, seq), 0)
    own = lax.broadcasted_iota(jnp.int32, (nb, seq), 1) // t
    for h in range(2):
        q = qa_ref[0, h]
        km_hi, km_mid, km_lo = _split3(km_ref[0, h])
        g = _dot_nt(km_hi, q) + _dot_nt(km_mid, q) + _dot_nt(km_lo, q)
        cnt = jnp.zeros((nb, seq), jnp.int32)
        for m in range(nb):
            gm = g[m:m + 1, :]
            beats = (gm > g) | ((gm == g) & (m < blk_row))
            cnt = cnt + jnp.where(beats & (m < own), 1, 0)
        keep = ((blk_row < own) & (cnt < MB_TOPK)) | (blk_row == own)
        bias = jnp.where(keep, 0.0, NEG_BIG)
        aux_lo = HEAD_DIM if h == 0 else 0
        pieces = []
        if aux_lo:
            pieces.append(jnp.zeros((aux_lo, seq), F32))
        pieces.append(bias)
        pieces.append(jnp.zeros((LANES - aux_lo - nb, seq), F32))
        bias_rows = jnp.concatenate(pieces, axis=0).astype(BF16)
        for j in range(seq // t):
            sl = slice(j * t, (j + 1) * t)
            bias_cols = _dot_nt(eye_ref[...], bias_rows[:, sl])
            qb_ref[0, h, sl, :] = (q[sl, :].astype(F32) + bias_cols).astype(BF16)


def _moba_gate_call(qa, km, eye, batch, seq):
    nb = seq // MB_BLOCK
    pairs = N_ATT_HEADS // 2
    spec = pl.BlockSpec((1, 2, seq, LANES), lambda b, p: (b, p, 0, 0))
    return pl.pallas_call(
        functools.partial(_moba_gate_kernel, nb=nb),
        grid=(batch, pairs),
        in_specs=[
            spec,
            pl.BlockSpec((1, 2, nb, LANES), lambda b, p: (b, p, 0, 0)),
            pl.BlockSpec((ATT_TILE, ATT_TILE), lambda b, p: (0, 0)),
        ],
        out_specs=spec,
        out_shape=jax.ShapeDtypeStruct(qa.shape, BF16),
        compiler_params=_cparams("parallel", "parallel"),
        name="moba_block_gate",
    )(qa, km, eye)


def _moba_kernel(qb_ref, ka_ref, v_ref, o_ref, s_scr, mx_ref, l_ref, acc_ref):
    t = ATT_TILE
    own = pl.program_id(1)
    n_heads = 2 * MB_PAIRS
    lane = lax.broadcasted_iota(jnp.int32, (1, LANES), 1)
    head_mask = (lane < HEAD_DIM, lane >= HEAD_DIM)

    heads = range(n_heads)

    def lane_fold(x, op):
        parts = [x[:, c * LANES:(c + 1) * LANES] for c in range(t // LANES)]
        return functools.reduce(op, parts)

    def score_step(kj, diagonal):
        start = pl.multiple_of(kj * t, t)
        ss = [_dot_nt(qb_ref[0, h], ka_ref[0, h, pl.ds(start, t), :]) for h in heads]
        if diagonal:
            row = lax.broadcasted_iota(jnp.int32, (t, t), 0)
            col = lax.broadcasted_iota(jnp.int32, (t, t), 1)
            ss = [jnp.where(col <= row, s, NEG_BIG) for s in ss]
        for h in heads:
            s_scr[h, kj] = ss[h]
            part = lane_fold(ss[h], jnp.maximum)
            mx_ref[h] = part if diagonal else jnp.maximum(mx_ref[h], part)

    score_step(own, True)

    def score_body(it, c):
        score_step(own - 1 - it, False)
        return c

    lax.fori_loop(0, own, score_body, 0)

    for h in heads:
        mx_ref[h] = jnp.broadcast_to(jnp.max(mx_ref[h], axis=-1, keepdims=True), (t, LANES))
    acc_ref[...] = jnp.zeros_like(acc_ref)
    l_ref[...] = jnp.zeros_like(l_ref)

    def value_body(kj, c):
        start = pl.multiple_of(kj * t, t)
        ps = [jnp.exp2(s_scr[h, kj] - jnp.concatenate([mx_ref[h]] * (t // LANES), axis=1))
              for h in heads]
        for h in heads:
            l_ref[h] += lane_fold(ps[h], jnp.add)
            v = v_ref[pl.ds(start, t), (h // 2) * LANES:(h // 2 + 1) * LANES]
            vh = jnp.where(head_mask[h % 2], v, jnp.zeros_like(v))
            acc_ref[h] += _dot(ps[h].astype(BF16), vh)
        return c

    lax.fori_loop(0, own + 1, value_body, 0)

    for pr in range(MB_PAIRS):
        inv0 = 1.0 / jnp.sum(l_ref[2 * pr], axis=-1, keepdims=True)
        inv1 = 1.0 / jnp.sum(l_ref[2 * pr + 1], axis=-1, keepdims=True)
        out = acc_ref[2 * pr] * inv0 + acc_ref[2 * pr + 1] * inv1
        o_ref[:, pr * LANES:(pr + 1) * LANES] = out.astype(o_ref.dtype)


def _moba_call(qb, ka, p16, batch, seq):
    n = batch * seq
    t = ATT_TILE
    nq = seq // t
    nb = seq // MB_BLOCK
    groups = BRANCH_W // (MB_PAIRS * LANES)
    nh = 2 * MB_PAIRS
    w = MB_PAIRS * LANES
    v_col0 = 3 * BRANCH_W // w
    return pl.pallas_call(
        _moba_kernel,
        grid=(batch * groups, nq),
        in_specs=[
            pl.BlockSpec((1, nh, t, LANES), lambda g, i: (g // groups, g % groups, i, 0)),
            pl.BlockSpec((1, nh, seq, LANES), lambda g, i: (g // groups, g % groups, 0, 0)),
            pl.BlockSpec((seq, w), lambda g, i: (g // groups, v_col0 + g % groups)),
        ],
        out_specs=pl.BlockSpec((t, w), lambda g, i: ((g // groups) * nq + i, g % groups)),
        out_shape=jax.ShapeDtypeStruct((n, BRANCH_W), BF16),
        scratch_shapes=[
            pltpu.VMEM((nh, nb, t, t), F32),
            pltpu.VMEM((nh, t, LANES), F32),
            pltpu.VMEM((nh, t, LANES), F32),
            pltpu.VMEM((nh, t, LANES), F32),
        ],
        compiler_params=_cparams("parallel", "arbitrary"),
        name="moba_attention",
    )(qb, ka, p16)


def _mlstm_kernel(u_ref, v_ref, o_ref, ifc_ref, ift_ref, cw_ref, ltri_ref, utri_ref, y_ref,
                  xbuf, c_ref, n_ref, m_ref):
    L = ML_CHUNK
    W = BRANCH_W
    halo = SUBLANES

    @pl.when(pl.program_id(1) == 0)
    def _():
        xbuf[0:halo, :] = jnp.zeros((halo, 2 * W), F32)
        c_ref[...] = jnp.zeros_like(c_ref)
        n_ref[...] = jnp.zeros_like(n_ref)
        m_ref[...] = jnp.zeros_like(m_ref)

    xbuf[halo:, :] = u_ref[...]
    conv = jnp.zeros((L, 2 * W), F32)
    for j in range(CONV_W):
        off = halo - (CONV_W - 1) + j
        conv = conv + cw_ref[j:j + 1, :] * xbuf[off:off + L, :]
    xbuf[0:halo, :] = u_ref[L - halo:, :]
    qk = conv * _sigmoid(conv)

    ift = ift_ref[0]
    lf_rows = _log_sigmoid(ift)
    bcum_rows = _dot_exact_rhs(lf_rows, utri_ref[...])
    lf_cols = _log_sigmoid(ifc_ref[...])
    bcum_cols = _dot_exact_lhs(ltri_ref[...], lf_cols)

    row = lax.broadcasted_iota(jnp.int32, (L, L), 0)
    col = lax.broadcasted_iota(jnp.int32, (L, L), 1)
    causal = col <= row

    for h in range(ML_HEADS):
        sl = slice(h * ML_HDIM, (h + 1) * ML_HDIM)
        q = qk[:, sl].astype(BF16)
        k = (qk[:, W + h * ML_HDIM:W + (h + 1) * ML_HDIM] * (ML_HDIM ** -0.5)).astype(BF16)
        v = v_ref[:, sl]
        a_row = ift[h:h + 1, :] - bcum_rows[ML_HEADS + h:ML_HEADS + h + 1, :]
        bc = bcum_cols[:, ML_HEADS + h:ML_HEADS + h + 1]
        m_prev = m_ref[h, 0:1, 0:1]

        a_mat = jnp.where(causal, a_row, NEG_BIG)
        mu = jnp.maximum(jnp.max(a_mat, axis=-1, keepdims=True), m_prev)
        w_intra = jnp.exp(a_mat - mu)
        w_inter = jnp.exp(m_prev - mu)
        sc = _dot_nt(q, k) * w_intra
        c_prev = c_ref[h]
        n_prev = n_ref[h, 0:1, :]
        num = _dot(sc.astype(BF16), v) + w_inter * _dot_nt(q, c_prev.astype(BF16))
        qn = jnp.sum(q.astype(F32) * n_prev, axis=-1, keepdims=True)
        den = jnp.sum(sc, axis=-1, keepdims=True) + w_inter * qn
        m_t = bc + mu
        hs = num / jnp.maximum(jnp.abs(den), jnp.exp(-m_t))
        gate = _sigmoid(o_ref[:, sl].astype(F32))
        y_ref[:, sl] = (hs * gate).astype(y_ref.dtype)

        mu_last = mu[L - 1:L, :]
        m_new = bc[L - 1:L, :] + mu_last
        decay = jnp.exp(m_prev - mu_last)
        wk_row = jnp.exp(a_row - mu_last)
        vt = v.astype(F32).T
        c_ref[h] = decay * c_prev + _dot((vt * wk_row).astype(BF16), k)
        wk8 = jnp.broadcast_to(wk_row, (SUBLANES, L)).astype(BF16)
        n_ref[h] = decay * n_ref[h] + _dot(wk8, k)
        m_ref[h] = jnp.broadcast_to(m_new, (SUBLANES, LANES))


def _mlstm_call(p32, p16, ifc, ift, conv_w, ltri, utri, batch, seq):
    n = batch * seq
    L = ML_CHUNK
    nc = seq // L
    W = BRANCH_W
    return pl.pallas_call(
        _mlstm_kernel,
        grid=(batch, nc),
        in_specs=[
            pl.BlockSpec((L, 2 * W), lambda b, i: (b * nc + i, 1)),
            pl.BlockSpec((L, W), lambda b, i: (b * nc + i, 4)),
            pl.BlockSpec((L, W), lambda b, i: (b * nc + i, 5)),
            pl.BlockSpec((L, LANES), lambda b, i: (b * nc + i, 0)),
            pl.BlockSpec((1, SUBLANES, L), lambda b, i: (b, 0, i)),
            pl.BlockSpec((CONV_W, 2 * W), lambda b, i: (0, 0)),
            pl.BlockSpec((L, L), lambda b, i: (0, 0)),
            pl.BlockSpec((L, L), lambda b, i: (0, 0)),
        ],
        out_specs=pl.BlockSpec((L, W), lambda b, i: (b * nc + i, 0)),
        out_shape=jax.ShapeDtypeStruct((n, W), BF16),
        scratch_shapes=[
            pltpu.VMEM((L + SUBLANES, 2 * W), F32),
            pltpu.VMEM((ML_HEADS, ML_HDIM, ML_HDIM), F32),
            pltpu.VMEM((ML_HEADS, SUBLANES, ML_HDIM), F32),
            pltpu.VMEM((ML_HEADS, SUBLANES, LANES), F32),
        ],
        compiler_params=_cparams("parallel", "arbitrary"),
        name="mlstm",
    )(p32, p16, p16, ifc, ift, conv_w, ltri, utri)


def _merge_kernel(ysb_ref, ymb_ref, yml_ref, gl_ref, x_ref, wb_ref, wo_ref, g2_ref, wr_ref, br_ref,
                  h_ref, xn_ref, route_ref, route_t_ref):
    merged = jnp.zeros((ROW_TILE, D_MODEL), F32)
    for b, y_ref in enumerate((ysb_ref, ymb_ref, yml_ref)):
        gate = _sigmoid(gl_ref[:, b * D_MODEL:(b + 1) * D_MODEL].astype(F32))
        merged = merged + gate * _dot(y_ref[...], wb_ref[b])
    hres = x_ref[...] + _dot(merged.astype(BF16), wo_ref[...])
    h_ref[...] = hres

    ms = jnp.mean(hres * hres, axis=-1, keepdims=True)
    xn = (hres * lax.rsqrt(ms + RMS_EPS) * g2_ref[...]).astype(BF16)
    xn_ref[...] = xn

    logits = _dot(xn, wr_ref[...]) + br_ref[...]
    lane = lax.broadcasted_iota(jnp.int32, (ROW_TILE, LANES), 1).astype(F32)
    far = float(LANES)
    is_g = lane < N_GROUPS
    gl = jnp.where(is_g, logits, NEG_BIG)
    gmax = jnp.max(gl, axis=-1, keepdims=True)
    gsum = jnp.sum(jnp.where(is_g, jnp.exp(gl - gmax), 0.0), axis=-1, keepdims=True)
    g_w = 1.0 / gsum
    g_idx = jnp.min(jnp.where(is_g & (gl == gmax), lane, far), axis=-1, keepdims=True)
    e_lo = N_GROUPS + g_idx * EXPERTS_PER_GROUP
    in_grp = (lane >= e_lo) & (lane < e_lo + EXPERTS_PER_GROUP)
    el = jnp.where(in_grp, logits, NEG_BIG)
    l1 = jnp.max(el, axis=-1, keepdims=True)
    i1 = jnp.min(jnp.where(in_grp & (el == l1), lane, far), axis=-1, keepdims=True)
    el2 = jnp.where(lane == i1, NEG_BIG, el)
    l2 = jnp.max(el2, axis=-1, keepdims=True)
    i2 = jnp.min(jnp.where(in_grp & (el2 == l2), lane, far), axis=-1, keepdims=True)
    p2 = jnp.exp(l2 - l1)
    w1 = g_w / (1.0 + p2)
    w2 = g_w * p2 / (1.0 + p2)
    slot = lane + e_lo
    cw = jnp.where(slot == i1, w1, 0.0) + jnp.where(slot == i2, w2, 0.0)
    cw = jnp.where(lane < EXPERTS_PER_GROUP, cw, 0.0)
    route = jnp.where(lane == EXPERTS_PER_GROUP, g_idx, cw)
    route_ref[...] = route
    route_t_ref[...] = route.T[0:2 * SUBLANES, :]


def _merge_call(ysb, ymb, yml, p16, x, wb, wo, g2, wr, br):
    n = x.shape[0]
    row = lambda i: (i, 0)
    const2 = lambda i: (0, 0)
    return pl.pallas_call(
        _merge_kernel,
        grid=(n // ROW_TILE,),
        in_specs=[
            pl.BlockSpec((ROW_TILE, BRANCH_W), row),
            pl.BlockSpec((ROW_TILE, BRANCH_W), row),
            pl.BlockSpec((ROW_TILE, BRANCH_W), row),
            pl.BlockSpec((ROW_TILE, N_BRANCH * D_MODEL), lambda i: (i, 1)),
            pl.BlockSpec((ROW_TILE, D_MODEL), row),
            pl.BlockSpec((N_BRANCH, BRANCH_W, D_MODEL), lambda i: (0, 0, 0)),
            pl.BlockSpec((D_MODEL, D_MODEL), const2),
            pl.BlockSpec((1, D_MODEL), const2),
            pl.BlockSpec((D_MODEL, LANES), const2),
            pl.BlockSpec((1, LANES), const2),
        ],
        out_specs=[
            pl.BlockSpec((ROW_TILE, D_MODEL), row),
            pl.BlockSpec((ROW_TILE, D_MODEL), row),
            pl.BlockSpec((ROW_TILE, LANES), row),
            pl.BlockSpec((2 * SUBLANES, ROW_TILE), lambda i: (0, i)),
        ],
        out_shape=[
            jax.ShapeDtypeStruct((n, D_MODEL), F32),
            jax.ShapeDtypeStruct((n, D_MODEL), BF16),
            jax.ShapeDtypeStruct((n, LANES), F32),
            jax.ShapeDtypeStruct((2 * SUBLANES, n), F32),
        ],
        compiler_params=_cparams("parallel"),
        name="merge_outproj_router",
    )(ysb, ymb, yml, p16, x, wb, wo, g2, wr, br)


def _moe_kernel(x_ref, rt_ref, rtt_ref, lx_ref, w1_ref, w3_ref, w2_ref, y_ref):
    tb = MOE_BLOCK
    sub = LANES
    grp = pl.program_id(1).astype(F32)

    @pl.when(pl.program_id(1) == 0)
    def _():
        y_ref[...] = jnp.zeros_like(y_ref)

    route = rt_ref[...]
    gid_lane = EXPERTS_PER_GROUP
    in_col = route[:, gid_lane:gid_lane + 1] == grp
    in_row = rtt_ref[gid_lane:gid_lane + 1, :] == grp
    ones_col = jnp.broadcast_to(jnp.where(in_col, 1.0, 0.0), (tb, LANES)).astype(BF16)
    ones_row = jnp.broadcast_to(jnp.where(in_row, 1.0, 0.0), (SUBLANES, tb)).astype(BF16)
    lx = lx_ref[...]
    key_col = jnp.where(in_col, _dot(lx, ones_col), -1.0)
    key_row = jnp.where(in_row, _dot_nt(ones_row, lx)[0:1, :], -1.0)
    count = jnp.sum(jnp.where(in_row, 1.0, 0.0)).astype(jnp.int32)
    r_hi, r_mid, r_lo = _split3(route)

    def sub_tile(s, c):
        base = (s * sub).astype(F32)
        slot_rows = lax.broadcasted_iota(jnp.int32, (sub, tb), 0).astype(F32) + base
        pick = jnp.where(key_row == slot_rows, 1.0, 0.0).astype(BF16)
        xs = _dot(pick, x_ref[...]).astype(BF16)
        cw = _dot(pick, r_hi) + _dot(pick, r_mid) + _dot(pick, r_lo)
        experts = range(EXPERTS_PER_GROUP)
        ups = [(_dot(xs, w1_ref[e]), _dot(xs, w3_ref[e])) for e in experts]
        hid = jnp.concatenate(
            [(a * _sigmoid(a) * b * cw[:, e:e + 1]).astype(BF16) for e, (a, b) in enumerate(ups)],
            axis=1)
        y = _dot(hid, w2_ref[...].reshape(EXPERTS_PER_GROUP * D_EXPERT, D_MODEL))
        slot_cols = lax.broadcasted_iota(jnp.int32, (tb, sub), 1).astype(F32) + base
        put = jnp.where(key_col == slot_cols, 1.0, 0.0).astype(BF16)
        y_ref[...] += _dot(put, y.astype(BF16))
        return c

    lax.fori_loop(0, (count + sub - 1) // sub, sub_tile, 0)


def _moe_call(xn2, route, route_t, lx, w1b, w3b, w2b):
    n = xn2.shape[0]
    tb = MOE_BLOCK
    e = EXPERTS_PER_GROUP
    return pl.pallas_call(
        _moe_kernel,
        grid=(n // tb, N_GROUPS),
        in_specs=[
            pl.BlockSpec((tb, D_MODEL), lambda i, g: (i, 0)),
            pl.BlockSpec((tb, LANES), lambda i, g: (i, 0)),
            pl.BlockSpec((2 * SUBLANES, tb), lambda i, g: (0, i)),
            pl.BlockSpec((tb, tb), lambda i, g: (0, 0)),
            pl.BlockSpec((e, D_MODEL, D_EXPERT), lambda i, g: (g, 0, 0)),
            pl.BlockSpec((e, D_MODEL, D_EXPERT), lambda i, g: (g, 0, 0)),
            pl.BlockSpec((e, D_EXPERT, D_MODEL), lambda i, g: (g, 0, 0)),
        ],
        out_specs=pl.BlockSpec((tb, D_MODEL), lambda i, g: (i, 0)),
        out_shape=jax.ShapeDtypeStruct((n, D_MODEL), F32),
        compiler_params=_cparams("parallel", "arbitrary"),
        name="moe_group_experts",
    )(xn2, route, route_t, lx, w1b, w3b, w2b)


def _add_kernel(a_ref, b_ref, o_ref):
    o_ref[...] = a_ref[...] + b_ref[...]


def _add_call(a, b):
    n, d = a.shape
    spec = pl.BlockSpec((2 * ROW_TILE, d), lambda i: (i, 0))
    return pl.pallas_call(
        _add_kernel,
        grid=(n // (2 * ROW_TILE),),
        in_specs=[spec, spec],
        out_specs=spec,
        out_shape=jax.ShapeDtypeStruct((n, d), a.dtype),
        compiler_params=_cparams("parallel"),
        name="residual_add",
    )(a, b)


def _tables(positions):
    t = ATT_TILE
    r = jnp.arange(t)
    u_tri = (r[:, None] >= r[None, :]).astype(BF16)
    eye = jnp.eye(t, dtype=BF16)
    L = ML_CHUNK
    rl = jnp.arange(L)
    ltri = (rl[:, None] >= rl[None, :]).astype(BF16)
    utri = ltri.T

    li = jnp.arange(LANES)
    seg = ((li[:, None] // HEAD_DIM) == (li[None, :] // HEAD_DIM)).astype(BF16)
    half = ROPE_DIM // 2
    src, dst = li[:, None], li[None, :]
    d_in = dst % HEAD_DIM
    rot = jnp.where((d_in < half) & (src == dst + half), -1.0,
                    jnp.where((d_in >= half) & (d_in < ROPE_DIM) & (src == dst - half), 1.0, 0.0)
                    ).astype(BF16)

    inv_freq = jnp.power(jnp.float32(ROPE_THETA), -jnp.arange(half, dtype=F32) / half)
    lane_in = li % HEAD_DIM
    ang = positions.reshape(-1).astype(F32)[:, None] * inv_freq[lane_in % half][None, :]
    rotated = (lane_in < ROPE_DIM)[None, :]
    cos_t = jnp.where(rotated, jnp.cos(ang), 1.0)
    sin_t = jnp.where(rotated, jnp.sin(ang), 0.0)
    rb = jnp.arange(MOE_BLOCK)
    lx = (rb[None, :] < rb[:, None]).astype(BF16)
    return dict(u_tri=u_tri, eye=eye, ltri=ltri, utri=utri, seg=seg, rot=rot, cos=cos_t, sin=sin_t,
                lx=lx)


def kernel(x, positions, norm1_g, w_in, qn_g, kn_g, conv_w, ml_gate_b, w_branch, w_out, norm2_g,
           w_rg, b_rg, w_re, b_re, w1, w3, w2):
    batch, seq, d = x.shape
    n = batch * seq
    W = BRANCH_W
    depth = w_in.shape[0]
    tb = _tables(positions)
    xf = x.reshape(n, d)
    y_moe = None

    for l in range(depth):
        wl = w_in[l]
        w16 = jnp.concatenate(
            [wl[:, 0:3 * W], wl[:, 5 * W:6 * W], wl[:, 8 * W:10 * W], wl[:, 10 * W + 2 * ML_HEADS:]],
            axis=1).astype(BF16)
        w32 = jnp.concatenate([wl[:, 3 * W:5 * W], wl[:, 6 * W:8 * W]], axis=1).astype(BF16)
        n_if = 2 * ML_HEADS
        wif = jnp.pad(wl[:, 10 * W:10 * W + n_if], ((0, 0), (0, LANES - n_if))).astype(BF16)
        bif = jnp.pad(ml_gate_b[l], (0, LANES - n_if)).reshape(1, LANES)

        xf, xn, ifc = _norm_call(xf, y_moe, norm1_g[l].reshape(1, d), wif, bif)
        p16 = _matmul_call(xn, w16, BF16, "inproj_bf16")
        p32 = _matmul_call(xn, w32, F32, "inproj_f32")

        y_sb = _sb_call(p16, tb["u_tri"], batch, seq)

        qg = jnp.tile(qn_g[l], LANES // HEAD_DIM).reshape(1, LANES)
        kg = jnp.tile(kn_g[l], LANES // HEAD_DIM).reshape(1, LANES)
        qa, ka, km = _moba_prep_call(p32, tb["cos"], tb["sin"], qg, kg, tb["seg"], tb["rot"],
                                     batch, seq)
        qb = _moba_gate_call(qa, km.transpose(0, 2, 1, 3), tb["eye"], batch, seq)
        y_mb = _moba_call(qb, ka, p16, batch, seq)

        ift = ifc[:, :SUBLANES].reshape(batch, seq, SUBLANES).transpose(0, 2, 1)
        y_ml = _mlstm_call(p32, p16, ifc, ift, conv_w[l], tb["ltri"], tb["utri"], batch, seq)

        wr = jnp.pad(jnp.concatenate([w_rg[l], w_re[l]], axis=1),
                     ((0, 0), (0, LANES - N_GROUPS - N_EXPERTS))).astype(BF16)
        br = jnp.pad(jnp.concatenate([b_rg[l], b_re[l]]),
                     (0, LANES - N_GROUPS - N_EXPERTS)).reshape(1, LANES)
        xf, xn2, route, route_t = _merge_call(
            y_sb, y_mb, y_ml, p16, xf, w_branch[l].astype(BF16), w_out[l].astype(BF16),
            norm2_g[l].reshape(1, d), wr, br)
        y_moe = _moe_call(xn2, route, route_t, tb["lx"], w1[l].astype(BF16), w3[l].astype(BF16),
                          w2[l].astype(BF16))

    return _add_call(xf, y_moe).reshape(batch, seq, d)
```

```python
import functools

import jax
import jax.numpy as jnp
from jax import lax
from jax.experimental import pallas as pl
from jax.experimental.pallas import tpu as pltpu

D_MODEL = 1024
BRANCH_W = D_MODEL // 2
N_BRANCH = 3
HEAD_DIM = 64
N_ATT_HEADS = BRANCH_W // HEAD_DIM
MB_BLOCK = 256
MB_TOPK = 3
ROPE_THETA = 500000.0
ROPE_DIM = HEAD_DIM // 4
ML_HEADS = 4
ML_HDIM = BRANCH_W // ML_HEADS
CONV_W = 4
N_GROUPS = 4
EXPERTS_PER_GROUP = 8
N_EXPERTS = N_GROUPS * EXPERTS_PER_GROUP
D_EXPERT = D_MODEL // 4
RMS_EPS = 1e-6

LANES = 128
SUBLANES = 8
VMEM_LIMIT_BYTES = 56 * 1024 * 1024

ROW_TILE = 512
COL_TILE = 1024
ATT_TILE = 256
ML_CHUNK = 256
MOE_BLOCK = 1024

LOG2_E = 1.4426950408889634
SB_PAIRS = 4
SB_EXIT_LOG2 = 160.0
MB_PAIRS = 4
NEG_BIG = -(2.0 ** 100)

F32 = jnp.float32
BF16 = jnp.bfloat16


def _cparams(*sem):
    return pltpu.CompilerParams(dimension_semantics=sem, vmem_limit_bytes=VMEM_LIMIT_BYTES)


def _dot(a, b):
    return jnp.dot(a, b, preferred_element_type=F32)


def _dot_nt(a, b):
    return lax.dot_general(a, b, (((1,), (1,)), ((), ())), preferred_element_type=F32)


def _split3(x):
    hi = x.astype(BF16)
    r1 = x - hi.astype(F32)
    mid = r1.astype(BF16)
    lo = (r1 - mid.astype(F32)).astype(BF16)
    return hi, mid, lo


def _dot_exact_rhs(x, m):
    hi, mid, lo = _split3(x)
    return _dot(hi, m) + _dot(mid, m) + _dot(lo, m)


def _dot_2term_rhs(x, m):
    hi = x.astype(BF16)
    lo = (x - hi.astype(F32)).astype(BF16)
    return _dot(hi, m) + _dot(lo, m)


def _dot_exact_lhs(m, x):
    hi, mid, lo = _split3(x)
    return _dot(m, hi) + _dot(m, mid) + _dot(m, lo)


def _log_sigmoid(x):
    return jnp.minimum(x, 0.0) - jnp.log(1.0 + jnp.exp(-jnp.abs(x)))


def _sigmoid(x):
    return 1.0 / (1.0 + jnp.exp(-x))


def _norm_kernel(x_ref, g_ref, wif_ref, bif_ref, xn_ref, if_ref):
    x = x_ref[...]
    ms = jnp.mean(x * x, axis=-1, keepdims=True)
    xn = (x * lax.rsqrt(ms + RMS_EPS) * g_ref[...]).astype(BF16)
    xn_ref[...] = xn
    if_ref[...] = _dot(xn, wif_ref[...]) + bif_ref[...]


def _norm_call(x, g, wif, bif):
    n = x.shape[0]
    return pl.pallas_call(
        _norm_kernel,
        grid=(n // ROW_TILE,),
        in_specs=[
            pl.BlockSpec((ROW_TILE, D_MODEL), lambda i: (i, 0)),
            pl.BlockSpec((1, D_MODEL), lambda i: (0, 0)),
            pl.BlockSpec((D_MODEL, LANES), lambda i: (0, 0)),
            pl.BlockSpec((1, LANES), lambda i: (0, 0)),
        ],
        out_specs=[
            pl.BlockSpec((ROW_TILE, D_MODEL), lambda i: (i, 0)),
            pl.BlockSpec((ROW_TILE, LANES), lambda i: (i, 0)),
        ],
        out_shape=[
            jax.ShapeDtypeStruct((n, D_MODEL), BF16),
            jax.ShapeDtypeStruct((n, LANES), F32),
        ],
        compiler_params=_cparams("parallel"),
        name="norm_gateproj",
    )(x, g, wif, bif)


def _matmul_kernel(a_ref, w_ref, o_ref):
    o_ref[...] = _dot(a_ref[...], w_ref[...]).astype(o_ref.dtype)


def _matmul_call(a, w, out_dtype, name):
    n, k = a.shape
    c = w.shape[1]
    tm = 2 * ROW_TILE
    return pl.pallas_call(
        _matmul_kernel,
        grid=(n // tm, c // COL_TILE),
        in_specs=[
            pl.BlockSpec((tm, k), lambda i, j: (i, 0)),
            pl.BlockSpec((k, COL_TILE), lambda i, j: (0, j)),
        ],
        out_specs=pl.BlockSpec((tm, COL_TILE), lambda i, j: (i, j)),
        out_shape=jax.ShapeDtypeStruct((n, c), out_dtype),
        compiler_params=_cparams("parallel", "parallel"),
        name=name,
    )(a, w)


def _sb_kernel(q_ref, k_ref, v_ref, u_ref, o_ref, acc_ref, carry_ref):
    t = ATT_TILE
    qi = pl.program_id(1)
    lane = lax.broadcasted_iota(jnp.int32, (1, LANES), 1)
    head_mask = (lane < HEAD_DIM, lane >= HEAD_DIM)
    n_heads = 2 * SB_PAIRS
    q = q_ref[...].astype(F32) * (HEAD_DIM ** -0.5 * LOG2_E)
    qh = []
    for p in range(SB_PAIRS):
        qp = q[:, p * LANES:(p + 1) * LANES]
        qh.extend(jnp.where(m, qp, 0.0).astype(BF16) for m in head_mask)
    u_tri = u_ref[...]

    acc_ref[...] = jnp.zeros_like(acc_ref)
    carry_ref[...] = jnp.zeros_like(carry_ref)

    def tile_step(kj, diagonal):
        start = pl.multiple_of(kj * t, t)
        if diagonal:
            row = lax.broadcasted_iota(jnp.int32, (t, t), 0)
            col = lax.broadcasted_iota(jnp.int32, (t, t), 1)
            past = col < row
        heads = range(n_heads)
        zs = [_dot_nt(qh[i], k_ref[pl.ds(start, t), (i // 2) * LANES:(i // 2 + 1) * LANES])
              for i in heads]
        sps = [jnp.maximum(z, 0.0) + jnp.log2(1.0 + jnp.exp2(-jnp.abs(z))) for z in zs]
        if diagonal:
            sps = [jnp.where(past, sp, 0.0) for sp in sps]
        rs = [_dot(sps[i].astype(BF16), u_tri)
              + jnp.concatenate([carry_ref[i]] * (t // LANES), axis=1) for i in heads]
        ws = [jnp.exp2(zs[i] - rs[i]) for i in heads]
        if diagonal:
            ws = [jnp.where(past, w, 0.0) for w in ws]
        carry_min = None
        for i in heads:
            v = v_ref[pl.ds(start, t), (i // 2) * LANES:(i // 2 + 1) * LANES]
            vh = jnp.where(head_mask[i % 2], v, jnp.zeros_like(v))
            acc_ref[i] += _dot(ws[i].astype(BF16), vh)
            carry = jnp.broadcast_to(rs[i][:, 0:1], (t, LANES))
            carry_ref[i] = carry
            carry_min = carry if i == 0 else jnp.minimum(carry_min, carry)
        return jnp.min(carry_min)

    def cond(c):
        it, carry_min = c
        return (it < qi) & (carry_min < SB_EXIT_LOG2)

    def body(c):
        it, _ = c
        return it + 1, tile_step(qi - 1 - it, False)

    lax.while_loop(cond, body, (jnp.int32(0), tile_step(qi, True)))
    for p in range(SB_PAIRS):
        o_ref[:, p * LANES:(p + 1) * LANES] = (acc_ref[2 * p] + acc_ref[2 * p + 1]).astype(o_ref.dtype)


def _sb_call(p16, u_tri, batch, seq):
    n = batch * seq
    t = ATT_TILE
    nq = seq // t
    groups = BRANCH_W // (SB_PAIRS * LANES)
    w = SB_PAIRS * LANES
    return pl.pallas_call(
        _sb_kernel,
        grid=(batch * groups, nq),
        in_specs=[
            pl.BlockSpec((t, w), lambda g, i: ((g // groups) * nq + i, g % groups)),
            pl.BlockSpec((seq, w), lambda g, i: (g // groups, groups + g % groups)),
            pl.BlockSpec((seq, w), lambda g, i: (g // groups, 2 * groups + g % groups)),
            pl.BlockSpec((t, t), lambda g, i: (0, 0)),
        ],
        out_specs=pl.BlockSpec((t, w), lambda g, i: ((g // groups) * nq + i, g % groups)),
        out_shape=jax.ShapeDtypeStruct((n, BRANCH_W), BF16),
        scratch_shapes=[
            pltpu.VMEM((2 * SB_PAIRS, t, LANES), F32),
            pltpu.VMEM((2 * SB_PAIRS, t, LANES), F32),
        ],
        compiler_params=_cparams("parallel", "arbitrary"),
        name="stick_breaking_attention",
    )(p16, p16, p16, u_tri)


def _moba_prep_kernel(q_ref, k_ref, cos_ref, sin_ref, qg_ref, kg_ref, seg_ref, rot_ref,
                      qa_ref, ka_ref, km_ref):
    blk = pl.program_id(1)
    lane = lax.broadcasted_iota(jnp.int32, (1, LANES), 1)
    even = lane < HEAD_DIM
    cos = cos_ref[...]
    sin = sin_ref[...]
    seg = seg_ref[...]
    rot = rot_ref[...]

    def norm_rope(x, g):
        ss = _dot_2term_rhs(x * x, seg)
        xn = x * lax.rsqrt(ss * (1.0 / HEAD_DIM) + RMS_EPS) * g
        return xn * cos + _dot_2term_rhs(xn, rot) * sin

    onehot_even = jnp.where(lane == HEAD_DIM + blk, 1.0, 0.0)
    onehot_odd = jnp.where(lane == blk, 1.0, 0.0)

    km_rows = []
    for c in range(BRANCH_W // LANES):
        sl = slice(c * LANES, (c + 1) * LANES)
        qr = norm_rope(q_ref[:, sl], qg_ref[...]) * (HEAD_DIM ** -0.5 * LOG2_E)
        kr = norm_rope(k_ref[:, sl], kg_ref[...])
        qa_ref[0, 2 * c] = jnp.where(even, qr, 0.0).astype(BF16)
        qa_ref[0, 2 * c + 1] = jnp.where(even, 0.0, qr).astype(BF16)
        ka_ref[0, 2 * c] = jnp.where(even, kr, onehot_even).astype(BF16)
        ka_ref[0, 2 * c + 1] = jnp.where(even, onehot_odd, kr).astype(BF16)
        kmean = jnp.mean(kr, axis=0, keepdims=True)
        km_rows.append(jnp.where(even, kmean, 0.0))
        km_rows.append(jnp.where(even, 0.0, kmean))
    km_ref[0, 0] = jnp.concatenate(km_rows, axis=0)


def _moba_prep_call(p32, cos_t, sin_t, qg, kg, seg, rot, batch, seq):
    t = MB_BLOCK
    nb = seq // t
    h = N_ATT_HEADS
    return pl.pallas_call(
        _moba_prep_kernel,
        grid=(batch, nb),
        in_specs=[
            pl.BlockSpec((t, BRANCH_W), lambda b, i: (b * nb + i, 0)),
            pl.BlockSpec((t, BRANCH_W), lambda b, i: (b * nb + i, 1)),
            pl.BlockSpec((t, LANES), lambda b, i: (b * nb + i, 0)),
            pl.BlockSpec((t, LANES), lambda b, i: (b * nb + i, 0)),
            pl.BlockSpec((1, LANES), lambda b, i: (0, 0)),
            pl.BlockSpec((1, LANES), lambda b, i: (0, 0)),
            pl.BlockSpec((LANES, LANES), lambda b, i: (0, 0)),
            pl.BlockSpec((LANES, LANES), lambda b, i: (0, 0)),
        ],
        out_specs=[
            pl.BlockSpec((1, h, t, LANES), lambda b, i: (b, 0, i, 0)),
            pl.BlockSpec((1, h, t, LANES), lambda b, i: (b, 0, i, 0)),
            pl.BlockSpec((1, 1, h, LANES), lambda b, i: (b, i, 0, 0)),
        ],
        out_shape=[
            jax.ShapeDtypeStruct((batch, h, seq, LANES), BF16),
            jax.ShapeDtypeStruct((batch, h, seq, LANES), BF16),
            jax.ShapeDtypeStruct((batch, nb, h, LANES), F32),
        ],
        compiler_params=_cparams("parallel", "parallel"),
        name="moba_qk_prep",
    )(p32, p32, cos_t, sin_t, qg, kg, seg, rot)


def _moba_gate_kernel(qa_ref, km_ref, eye_ref, qb_ref, *, nb):
    t = ATT_TILE
    seq = qa_ref.shape[2]
    blk_row = lax.broadcasted_iota(jnp.int32, (nb, seq), 0)
    own = lax.broadcasted_iota(jnp.int32, (nb, seq), 1) // t
    for h in range(2):
        q = qa_ref[0, h]
        km_hi, km_mid, km_lo = _split3(km_ref[0, h])
        g = _dot_nt(km_hi, q) + _dot_nt(km_mid, q) + _dot_nt(km_lo, q)
        cnt = jnp.zeros((nb, seq), jnp.int32)
        for m in range(nb):
            gm = g[m:m + 1, :]
            beats = (gm > g) | ((gm == g) & (m < blk_row))
            cnt = cnt + jnp.where(beats & (m < own), 1, 0)
        keep = ((blk_row < own) & (cnt < MB_TOPK)) | (blk_row == own)
        bias = jnp.where(keep, 0.0, NEG_BIG)
        aux_lo = HEAD_DIM if h == 0 else 0
        pieces = []
        if aux_lo:
            pieces.append(jnp.zeros((aux_lo, seq), F32))
        pieces.append(bias)
        pieces.append(jnp.zeros((LANES - aux_lo - nb, seq), F32))
        bias_rows = jnp.concatenate(pieces, axis=0).astype(BF16)
        for j in range(seq // t):
            sl = slice(j * t, (j + 1) * t)
            bias_cols = _dot_nt(eye_ref[...], bias_rows[:, sl])
            qb_ref[0, h, sl, :] = (q[sl, :].astype(F32) + bias_cols).astype(BF16)


def _moba_gate_call(qa, km, eye, batch, seq):
    nb = seq // MB_BLOCK
    pairs = N_ATT_HEADS // 2
    spec = pl.BlockSpec((1, 2, seq, LANES), lambda b, p: (b, p, 0, 0))
    return pl.pallas_call(
        functools.partial(_moba_gate_kernel, nb=nb),
        grid=(batch, pairs),
        in_specs=[
            spec,
            pl.BlockSpec((1, 2, nb, LANES), lambda b, p: (b, p, 0, 0)),
            pl.BlockSpec((ATT_TILE, ATT_TILE), lambda b, p: (0, 0)),
        ],
        out_specs=spec,
        out_shape=jax.ShapeDtypeStruct(qa.shape, BF16),
        compiler_params=_cparams("parallel", "parallel"),
        name="moba_block_gate",
    )(qa, km, eye)


def _moba_kernel(qb_ref, ka_ref, v_ref, o_ref, s_scr, mx_ref, acc_ref):
    t = ATT_TILE
    own = pl.program_id(1)
    n_heads = 2 * MB_PAIRS
    lane = lax.broadcasted_iota(jnp.int32, (1, LANES), 1)
    head_mask = (lane < HEAD_DIM, lane >= HEAD_DIM)

    heads = range(n_heads)

    def lane_fold(x, op):
        parts = [x[:, c * LANES:(c + 1) * LANES] for c in range(t // LANES)]
        return functools.reduce(op, parts)

    def score_step(kj, diagonal):
        start = pl.multiple_of(kj * t, t)
        ss = [_dot_nt(qb_ref[0, h], ka_ref[0, h, pl.ds(start, t), :]) for h in heads]
        if diagonal:
            row = lax.broadcasted_iota(jnp.int32, (t, t), 0)
            col = lax.broadcasted_iota(jnp.int32, (t, t), 1)
            ss = [jnp.where(col <= row, s, NEG_BIG) for s in ss]
        for h in heads:
            s_scr[h, kj] = ss[h]
            part = lane_fold(ss[h], jnp.maximum)
            mx_ref[h] = part if diagonal else jnp.maximum(mx_ref[h], part)

    score_step(own, True)

    def score_body(it, c):
        score_step(own - 1 - it, False)
        return c

    lax.fori_loop(0, own, score_body, 0)

    for h in heads:
        mx_ref[h] = jnp.broadcast_to(jnp.max(mx_ref[h], axis=-1, keepdims=True), (t, LANES))
    acc_ref[...] = jnp.zeros_like(acc_ref)

    ones_lane = (HEAD_DIM, 0)
    ones_col = [jnp.where(lane == ones_lane[e], 1.0, 0.0).astype(BF16) for e in range(2)]

    def value_body(kj, c):
        start = pl.multiple_of(kj * t, t)
        ps = [jnp.exp2((s_scr[h, kj] - jnp.concatenate([mx_ref[h]] * (t // LANES), axis=1))
                       .astype(BF16)) for h in heads]
        for h in heads:
            v = v_ref[pl.ds(start, t), (h // 2) * LANES:(h // 2 + 1) * LANES]
            vh = jnp.where(head_mask[h % 2], v, ones_col[h % 2])
            acc_ref[h] += _dot(ps[h], vh)
        return c

    lax.fori_loop(0, own + 1, value_body, 0)

    for pr in range(MB_PAIRS):
        acc0 = acc_ref[2 * pr]
        acc1 = acc_ref[2 * pr + 1]
        inv0 = 1.0 / acc0[:, ones_lane[0]:ones_lane[0] + 1]
        inv1 = 1.0 / acc1[:, ones_lane[1]:ones_lane[1] + 1]
        out = jnp.where(head_mask[0], acc0 * inv0, acc1 * inv1)
        o_ref[:, pr * LANES:(pr + 1) * LANES] = out.astype(o_ref.dtype)


def _moba_call(qb, ka, p16, batch, seq):
    n = batch * seq
    t = ATT_TILE
    nq = seq // t
    nb = seq // MB_BLOCK
    groups = BRANCH_W // (MB_PAIRS * LANES)
    nh = 2 * MB_PAIRS
    w = MB_PAIRS * LANES
    v_col0 = 3 * BRANCH_W // w
    return pl.pallas_call(
        _moba_kernel,
        grid=(batch * groups, nq),
        in_specs=[
            pl.BlockSpec((1, nh, t, LANES), lambda g, i: (g // groups, g % groups, i, 0)),
            pl.BlockSpec((1, nh, seq, LANES), lambda g, i: (g // groups, g % groups, 0, 0)),
            pl.BlockSpec((seq, w), lambda g, i: (g // groups, v_col0 + g % groups)),
        ],
        out_specs=pl.BlockSpec((t, w), lambda g, i: ((g // groups) * nq + i, g % groups)),
        out_shape=jax.ShapeDtypeStruct((n, BRANCH_W), BF16),
        scratch_shapes=[
            pltpu.VMEM((nh, nb, t, t), F32),
            pltpu.VMEM((nh, t, LANES), F32),
            pltpu.VMEM((nh, t, LANES), F32),
        ],
        compiler_params=_cparams("parallel", "arbitrary"),
        name="moba_attention",
    )(qb, ka, p16)


def _mlstm_kernel(u_ref, v_ref, o_ref, ifc_ref, ift_ref, cw_ref, ltri_ref, utri_ref, y_ref,
                  xbuf, c_ref, n_ref, m_ref):
    L = ML_CHUNK
    W = BRANCH_W
    halo = SUBLANES

    @pl.when(pl.program_id(1) == 0)
    def _():
        xbuf[0:halo, :] = jnp.zeros((halo, 2 * W), F32)
        c_ref[...] = jnp.zeros_like(c_ref)
        n_ref[...] = jnp.zeros_like(n_ref)
        m_ref[...] = jnp.zeros_like(m_ref)

    xbuf[halo:, :] = u_ref[...]
    conv = jnp.zeros((L, 2 * W), F32)
    for j in range(CONV_W):
        off = halo - (CONV_W - 1) + j
        conv = conv + cw_ref[j:j + 1, :] * xbuf[off:off + L, :]
    xbuf[0:halo, :] = u_ref[L - halo:, :]
    qk = conv * _sigmoid(conv)

    ift = ift_ref[0]
    lf_rows = _log_sigmoid(ift)
    bcum_rows = _dot_exact_rhs(lf_rows, utri_ref[...])
    lf_cols = _log_sigmoid(ifc_ref[...])
    bcum_cols = _dot_exact_lhs(ltri_ref[...], lf_cols)

    row = lax.broadcasted_iota(jnp.int32, (L, L), 0)
    col = lax.broadcasted_iota(jnp.int32, (L, L), 1)
    causal = col <= row

    for h in range(ML_HEADS):
        sl = slice(h * ML_HDIM, (h + 1) * ML_HDIM)
        q = qk[:, sl].astype(BF16)
        k = (qk[:, W + h * ML_HDIM:W + (h + 1) * ML_HDIM] * (ML_HDIM ** -0.5)).astype(BF16)
        v = v_ref[:, sl]
        a_row = ift[h:h + 1, :] - bcum_rows[ML_HEADS + h:ML_HEADS + h + 1, :]
        bc = bcum_cols[:, ML_HEADS + h:ML_HEADS + h + 1]
        m_prev = m_ref[h, 0:1, 0:1]

        a_mat = jnp.where(causal, a_row, NEG_BIG)
        mu = jnp.maximum(jnp.max(a_mat, axis=-1, keepdims=True), m_prev)
        w_intra = jnp.exp(a_mat - mu)
        w_inter = jnp.exp(m_prev - mu)
        sc = _dot_nt(q, k) * w_intra
        c_prev = c_ref[h]
        n_prev = n_ref[h, 0:1, :]
        num = _dot(sc.astype(BF16), v) + w_inter * _dot_nt(q, c_prev.astype(BF16))
        qn = jnp.sum(q.astype(F32) * n_prev, axis=-1, keepdims=True)
        den = jnp.sum(sc, axis=-1, keepdims=True) + w_inter * qn
        m_t = bc + mu
        hs = num / jnp.maximum(jnp.abs(den), jnp.exp(-m_t))
        gate = _sigmoid(o_ref[:, sl].astype(F32))
        y_ref[:, sl] = (hs * gate).astype(y_ref.dtype)

        mu_last = mu[L - 1:L, :]
        m_new = bc[L - 1:L, :] + mu_last
        decay = jnp.exp(m_prev - mu_last)
        wk_row = jnp.exp(a_row - mu_last)
        vt = v.astype(F32).T
        c_ref[h] = decay * c_prev + _dot((vt * wk_row).astype(BF16), k)
        wk8 = jnp.broadcast_to(wk_row, (SUBLANES, L)).astype(BF16)
        n_ref[h] = decay * n_ref[h] + _dot(wk8, k)
        m_ref[h] = jnp.broadcast_to(m_new, (SUBLANES, LANES))


def _mlstm_call(p32, p16, ifc, ift, conv_w, ltri, utri, batch, seq):
    n = batch * seq
    L = ML_CHUNK
    nc = seq // L
    W = BRANCH_W
    return pl.pallas_call(
        _mlstm_kernel,
        grid=(batch, nc),
        in_specs=[
            pl.BlockSpec((L, 2 * W), lambda b, i: (b * nc + i, 1)),
            pl.BlockSpec((L, W), lambda b, i: (b * nc + i, 4)),
            pl.BlockSpec((L, W), lambda b, i: (b * nc + i, 5)),
            pl.BlockSpec((L, LANES), lambda b, i: (b * nc + i, 0)),
            pl.BlockSpec((1, SUBLANES, L), lambda b, i: (b, 0, i)),
            pl.BlockSpec((CONV_W, 2 * W), lambda b, i: (0, 0)),
            pl.BlockSpec((L, L), lambda b, i: (0, 0)),
            pl.BlockSpec((L, L), lambda b, i: (0, 0)),
        ],
        out_specs=pl.BlockSpec((L, W), lambda b, i: (b * nc + i, 0)),
        out_shape=jax.ShapeDtypeStruct((n, W), BF16),
        scratch_shapes=[
            pltpu.VMEM((L + SUBLANES, 2 * W), F32),
            pltpu.VMEM((ML_HEADS, ML_HDIM, ML_HDIM), F32),
            pltpu.VMEM((ML_HEADS, SUBLANES, ML_HDIM), F32),
            pltpu.VMEM((ML_HEADS, SUBLANES, LANES), F32),
        ],
        compiler_params=_cparams("parallel", "arbitrary"),
        name="mlstm",
    )(p32, p16, p16, ifc, ift, conv_w, ltri, utri)


def _merge_kernel(ysb_ref, ymb_ref, yml_ref, gl_ref, x_ref, wb_ref, wo_ref, g2_ref, wr_ref, br_ref,
                  h_ref, xn_ref, route_ref, route_t_ref):
    merged = jnp.zeros((ROW_TILE, D_MODEL), F32)
    for b, y_ref in enumerate((ysb_ref, ymb_ref, yml_ref)):
        gate = _sigmoid(gl_ref[:, b * D_MODEL:(b + 1) * D_MODEL].astype(F32))
        merged = merged + gate * _dot(y_ref[...], wb_ref[b])
    hres = x_ref[...] + _dot(merged.astype(BF16), wo_ref[...])
    h_ref[...] = hres

    ms = jnp.mean(hres * hres, axis=-1, keepdims=True)
    xn = (hres * lax.rsqrt(ms + RMS_EPS) * g2_ref[...]).astype(BF16)
    xn_ref[...] = xn

    logits = _dot(xn, wr_ref[...]) + br_ref[...]
    lane = lax.broadcasted_iota(jnp.int32, (ROW_TILE, LANES), 1).astype(F32)
    far = float(LANES)
    is_g = lane < N_GROUPS
    gl = jnp.where(is_g, logits, NEG_BIG)
    gmax = jnp.max(gl, axis=-1, keepdims=True)
    gsum = jnp.sum(jnp.where(is_g, jnp.exp(gl - gmax), 0.0), axis=-1, keepdims=True)
    g_w = 1.0 / gsum
    g_idx = jnp.min(jnp.where(is_g & (gl == gmax), lane, far), axis=-1, keepdims=True)
    e_lo = N_GROUPS + g_idx * EXPERTS_PER_GROUP
    in_grp = (lane >= e_lo) & (lane < e_lo + EXPERTS_PER_GROUP)
    el = jnp.where(in_grp, logits, NEG_BIG)
    l1 = jnp.max(el, axis=-1, keepdims=True)
    i1 = jnp.min(jnp.where(in_grp & (el == l1), lane, far), axis=-1, keepdims=True)
    el2 = jnp.where(lane == i1, NEG_BIG, el)
    l2 = jnp.max(el2, axis=-1, keepdims=True)
    i2 = jnp.min(jnp.where(in_grp & (el2 == l2), lane, far), axis=-1, keepdims=True)
    p2 = jnp.exp(l2 - l1)
    w1 = g_w / (1.0 + p2)
    w2 = g_w * p2 / (1.0 + p2)
    slot = lane + e_lo
    cw = jnp.where(slot == i1, w1, 0.0) + jnp.where(slot == i2, w2, 0.0)
    cw = jnp.where(lane < EXPERTS_PER_GROUP, cw, 0.0)
    route = jnp.where(lane == EXPERTS_PER_GROUP, g_idx, cw)
    route_ref[...] = route
    route_t_ref[...] = route.T[0:2 * SUBLANES, :]


def _merge_call(ysb, ymb, yml, p16, x, wb, wo, g2, wr, br):
    n = x.shape[0]
    row = lambda i: (i, 0)
    const2 = lambda i: (0, 0)
    return pl.pallas_call(
        _merge_kernel,
        grid=(n // ROW_TILE,),
        in_specs=[
            pl.BlockSpec((ROW_TILE, BRANCH_W), row),
            pl.BlockSpec((ROW_TILE, BRANCH_W), row),
            pl.BlockSpec((ROW_TILE, BRANCH_W), row),
            pl.BlockSpec((ROW_TILE, N_BRANCH * D_MODEL), lambda i: (i, 1)),
            pl.BlockSpec((ROW_TILE, D_MODEL), row),
            pl.BlockSpec((N_BRANCH, BRANCH_W, D_MODEL), lambda i: (0, 0, 0)),
            pl.BlockSpec((D_MODEL, D_MODEL), const2),
            pl.BlockSpec((1, D_MODEL), const2),
            pl.BlockSpec((D_MODEL, LANES), const2),
            pl.BlockSpec((1, LANES), const2),
        ],
        out_specs=[
            pl.BlockSpec((ROW_TILE, D_MODEL), row),
            pl.BlockSpec((ROW_TILE, D_MODEL), row),
            pl.BlockSpec((ROW_TILE, LANES), row),
            pl.BlockSpec((2 * SUBLANES, ROW_TILE), lambda i: (0, i)),
        ],
        out_shape=[
            jax.ShapeDtypeStruct((n, D_MODEL), F32),
            jax.ShapeDtypeStruct((n, D_MODEL), BF16),
            jax.ShapeDtypeStruct((n, LANES), F32),
            jax.ShapeDtypeStruct((2 * SUBLANES, n), F32),
        ],
        compiler_params=_cparams("parallel"),
        name="merge_outproj_router",
    )(ysb, ymb, yml, p16, x, wb, wo, g2, wr, br)


def _moe_kernel(x_ref, h_ref, rt_ref, rtt_ref, lx_ref, w1_ref, w3_ref, w2_ref, y_ref, ybuf):
    tb = MOE_BLOCK
    sub = LANES
    grp = pl.program_id(1).astype(F32)

    @pl.when(pl.program_id(1) == 0)
    def _():
        y_ref[...] = h_ref[...]

    gid_lane = EXPERTS_PER_GROUP
    in_col = rt_ref[:, gid_lane:gid_lane + 1] == grp
    route_t = rtt_ref[...]
    in_row = route_t[gid_lane:gid_lane + 1, :] == grp
    ones_col = jnp.broadcast_to(jnp.where(in_col, 1.0, 0.0), (tb, LANES)).astype(BF16)
    ones_row = jnp.broadcast_to(jnp.where(in_row, 1.0, 0.0), (SUBLANES, tb)).astype(BF16)
    lx = lx_ref[...]
    key_col = jnp.where(in_col, _dot(lx, ones_col), -1.0)
    key_col = jnp.concatenate([key_col, key_col], axis=1)
    key_row = jnp.where(in_row, _dot_nt(ones_row, lx)[0:1, :], -1.0)
    count = jnp.sum(jnp.where(in_row, 1.0, 0.0)).astype(jnp.int32)
    n_sub = (count + sub - 1) // sub
    t_hi, t_mid, t_lo = _split3(route_t)

    def experts_of_sub_tile(s):
        base = (s * sub).astype(F32)
        slot_rows = lax.broadcasted_iota(jnp.int32, (sub, tb), 0).astype(F32) + base
        pick = jnp.where(key_row == slot_rows, 1.0, 0.0).astype(BF16)
        xs = _dot(pick, x_ref[...]).astype(BF16)
        cw_t = _dot_nt(t_hi, pick) + _dot_nt(t_mid, pick) + _dot_nt(t_lo, pick)
        cw = jnp.concatenate(
            [cw_t, jnp.zeros((LANES - 2 * SUBLANES, sub), F32)], axis=0).T
        experts = range(EXPERTS_PER_GROUP)
        ups = [(_dot(xs, w1_ref[e]), _dot(xs, w3_ref[e])) for e in experts]
        hid = jnp.concatenate(
            [(a * _sigmoid(a) * b * cw[:, e:e + 1]).astype(BF16) for e, (a, b) in enumerate(ups)],
            axis=1)
        y = _dot(hid, w2_ref[...].reshape(EXPERTS_PER_GROUP * D_EXPERT, D_MODEL))
        return y.astype(BF16)

    def sub_tile_pair(pi, c):
        s0 = 2 * pi
        ybuf[0:sub, :] = experts_of_sub_tile(s0)

        @pl.when(s0 + 1 < n_sub)
        def _():
            ybuf[sub:, :] = experts_of_sub_tile(s0 + 1)

        @pl.when(s0 + 1 >= n_sub)
        def _():
            ybuf[sub:, :] = jnp.zeros((sub, D_MODEL), BF16)

        slot_cols = (lax.broadcasted_iota(jnp.int32, (tb, 2 * sub), 1).astype(F32)
                     + (s0 * sub).astype(F32))
        put = jnp.where(key_col == slot_cols, 1.0, 0.0).astype(BF16)
        y_ref[...] += _dot(put, ybuf[...])
        return c

    lax.fori_loop(0, (n_sub + 1) // 2, sub_tile_pair, 0)


def _moe_call(xn2, h, route, route_t, lx, w1b, w3b, w2b):
    n = xn2.shape[0]
    tb = MOE_BLOCK
    e = EXPERTS_PER_GROUP
    return pl.pallas_call(
        _moe_kernel,
        grid=(n // tb, N_GROUPS),
        in_specs=[
            pl.BlockSpec((tb, D_MODEL), lambda i, g: (i, 0)),
            pl.BlockSpec((tb, D_MODEL), lambda i, g: (i, 0)),
            pl.BlockSpec((tb, LANES), lambda i, g: (i, 0)),
            pl.BlockSpec((2 * SUBLANES, tb), lambda i, g: (0, i)),
            pl.BlockSpec((tb, tb), lambda i, g: (0, 0)),
            pl.BlockSpec((e, D_MODEL, D_EXPERT), lambda i, g: (g, 0, 0)),
            pl.BlockSpec((e, D_MODEL, D_EXPERT), lambda i, g: (g, 0, 0)),
            pl.BlockSpec((e, D_EXPERT, D_MODEL), lambda i, g: (g, 0, 0)),
        ],
        out_specs=pl.BlockSpec((tb, D_MODEL), lambda i, g: (i, 0)),
        out_shape=jax.ShapeDtypeStruct((n, D_MODEL), F32),
        scratch_shapes=[pltpu.VMEM((2 * LANES, D_MODEL), BF16)],
        compiler_params=_cparams("parallel", "arbitrary"),
        name="moe_group_experts",
    )(xn2, h, route, route_t, lx, w1b, w3b, w2b)


def _tables(positions):
    t = ATT_TILE
    r = jnp.arange(t)
    u_tri = (r[:, None] >= r[None, :]).astype(BF16)
    eye = jnp.eye(t, dtype=BF16)
    L = ML_CHUNK
    rl = jnp.arange(L)
    ltri = (rl[:, None] >= rl[None, :]).astype(BF16)
    utri = ltri.T

    li = jnp.arange(LANES)
    seg = ((li[:, None] // HEAD_DIM) == (li[None, :] // HEAD_DIM)).astype(BF16)
    half = ROPE_DIM // 2
    src, dst = li[:, None], li[None, :]
    d_in = dst % HEAD_DIM
    rot = jnp.where((d_in < half) & (src == dst + half), -1.0,
                    jnp.where((d_in >= half) & (d_in < ROPE_DIM) & (src == dst - half), 1.0, 0.0)
                    ).astype(BF16)

    inv_freq = jnp.power(jnp.float32(ROPE_THETA), -jnp.arange(half, dtype=F32) / half)
    lane_in = li % HEAD_DIM
    ang = positions.reshape(-1).astype(F32)[:, None] * inv_freq[lane_in % half][None, :]
    rotated = (lane_in < ROPE_DIM)[None, :]
    cos_t = jnp.where(rotated, jnp.cos(ang), 1.0)
    sin_t = jnp.where(rotated, jnp.sin(ang), 0.0)
    rb = jnp.arange(MOE_BLOCK)
    lx = (rb[None, :] < rb[:, None]).astype(BF16)
    return dict(u_tri=u_tri, eye=eye, ltri=ltri, utri=utri, seg=seg, rot=rot, cos=cos_t, sin=sin_t,
                lx=lx)


def kernel(x, positions, norm1_g, w_in, qn_g, kn_g, conv_w, ml_gate_b, w_branch, w_out, norm2_g,
           w_rg, b_rg, w_re, b_re, w1, w3, w2):
    batch, seq, d = x.shape
    n = batch * seq
    W = BRANCH_W
    depth = w_in.shape[0]
    tb = _tables(positions)
    xf = x.reshape(n, d)

    for l in range(depth):
        wl = w_in[l]
        w16 = jnp.concatenate(
            [wl[:, 0:3 * W], wl[:, 5 * W:6 * W], wl[:, 8 * W:10 * W], wl[:, 10 * W + 2 * ML_HEADS:]],
            axis=1).astype(BF16)
        w32 = jnp.concatenate([wl[:, 3 * W:5 * W], wl[:, 6 * W:8 * W]], axis=1).astype(BF16)
        n_if = 2 * ML_HEADS
        wif = jnp.pad(wl[:, 10 * W:10 * W + n_if], ((0, 0), (0, LANES - n_if))).astype(BF16)
        bif = jnp.pad(ml_gate_b[l], (0, LANES - n_if)).reshape(1, LANES)

        xn, ifc = _norm_call(xf, norm1_g[l].reshape(1, d), wif, bif)
        p16 = _matmul_call(xn, w16, BF16, "inproj_bf16")
        p32 = _matmul_call(xn, w32, F32, "inproj_f32")

        y_sb = _sb_call(p16, tb["u_tri"], batch, seq)

        qg = jnp.tile(qn_g[l], LANES // HEAD_DIM).reshape(1, LANES)
        kg = jnp.tile(kn_g[l], LANES // HEAD_DIM).reshape(1, LANES)
        qa, ka, km = _moba_prep_call(p32, tb["cos"], tb["sin"], qg, kg, tb["seg"], tb["rot"],
                                     batch, seq)
        qb = _moba_gate_call(qa, km.transpose(0, 2, 1, 3), tb["eye"], batch, seq)
        y_mb = _moba_call(qb, ka, p16, batch, seq)

        ift = ifc[:, :SUBLANES].reshape(batch, seq, SUBLANES).transpose(0, 2, 1)
        y_ml = _mlstm_call(p32, p16, ifc, ift, conv_w[l], tb["ltri"], tb["utri"], batch, seq)

        wr = jnp.pad(jnp.concatenate([w_rg[l], w_re[l]], axis=1),
                     ((0, 0), (0, LANES - N_GROUPS - N_EXPERTS))).astype(BF16)
        br = jnp.pad(jnp.concatenate([b_rg[l], b_re[l]]),
                     (0, LANES - N_GROUPS - N_EXPERTS)).reshape(1, LANES)
        hres, xn2, route, route_t = _merge_call(
            y_sb, y_mb, y_ml, p16, xf, w_branch[l].astype(BF16), w_out[l].astype(BF16),
            norm2_g[l].reshape(1, d), wr, br)
        xf = _moe_call(xn2, hres, route, route_t, tb["lx"], w1[l].astype(BF16),
                       w3[l].astype(BF16), w2[l].astype(BF16))

    return xf.reshape(batch, seq, d)
```

```python
import functools

import jax
import jax.numpy as jnp
from jax import lax
from jax.experimental import pallas as pl
from jax.experimental.pallas import tpu as pltpu

D_MODEL = 1024
BRANCH_W = D_MODEL // 2
N_BRANCH = 3
HEAD_DIM = 64
N_ATT_HEADS = BRANCH_W // HEAD_DIM
MB_BLOCK = 256
MB_TOPK = 3
ROPE_THETA = 500000.0
ROPE_DIM = HEAD_DIM // 4
ML_HEADS = 4
ML_HDIM = BRANCH_W // ML_HEADS
CONV_W = 4
N_GROUPS = 4
EXPERTS_PER_GROUP = 8
N_EXPERTS = N_GROUPS * EXPERTS_PER_GROUP
D_EXPERT = D_MODEL // 4
RMS_EPS = 1e-6

LANES = 128
SUBLANES = 8
VMEM_LIMIT_BYTES = 56 * 1024 * 1024

ROW_TILE = 512
COL_TILE = 1024
ATT_TILE = 256
ML_CHUNK = 256
MOE_BLOCK = 1024
MOE_SUB = 256

LOG2_E = 1.4426950408889634
SB_PAIRS = 4
SB_EXIT_LOG2 = 160.0
MB_PAIRS = 4
NEG_BIG = -(2.0 ** 100)

F32 = jnp.float32
BF16 = jnp.bfloat16


def _cparams(*sem):
    return pltpu.CompilerParams(dimension_semantics=sem, vmem_limit_bytes=VMEM_LIMIT_BYTES)


def _dot(a, b):
    return jnp.dot(a, b, preferred_element_type=F32)


def _dot_nt(a, b):
    return lax.dot_general(a, b, (((1,), (1,)), ((), ())), preferred_element_type=F32)


def _split3(x):
    hi = x.astype(BF16)
    r1 = x - hi.astype(F32)
    mid = r1.astype(BF16)
    lo = (r1 - mid.astype(F32)).astype(BF16)
    return hi, mid, lo


def _dot_exact_rhs(x, m):
    hi, mid, lo = _split3(x)
    return _dot(hi, m) + _dot(mid, m) + _dot(lo, m)


def _dot_2term_rhs(x, m):
    hi = x.astype(BF16)
    lo = (x - hi.astype(F32)).astype(BF16)
    return _dot(hi, m) + _dot(lo, m)


def _dot_exact_lhs(m, x):
    hi, mid, lo = _split3(x)
    return _dot(m, hi) + _dot(m, mid) + _dot(m, lo)


def _log_sigmoid(x):
    return jnp.minimum(x, 0.0) - jnp.log(1.0 + jnp.exp(-jnp.abs(x)))


def _sigmoid(x):
    return 1.0 / (1.0 + jnp.exp(-x))


def _norm_kernel(x_ref, g_ref, wif_ref, bif_ref, xn_ref, if_ref):
    x = x_ref[...]
    ms = jnp.mean(x * x, axis=-1, keepdims=True)
    xn = (x * lax.rsqrt(ms + RMS_EPS) * g_ref[...]).astype(BF16)
    xn_ref[...] = xn
    if_ref[...] = _dot(xn, wif_ref[...]) + bif_ref[...]


def _norm_call(x, g, wif, bif):
    n = x.shape[0]
    return pl.pallas_call(
        _norm_kernel,
        grid=(n // ROW_TILE,),
        in_specs=[
            pl.BlockSpec((ROW_TILE, D_MODEL), lambda i: (i, 0)),
            pl.BlockSpec((1, D_MODEL), lambda i: (0, 0)),
            pl.BlockSpec((D_MODEL, LANES), lambda i: (0, 0)),
            pl.BlockSpec((1, LANES), lambda i: (0, 0)),
        ],
        out_specs=[
            pl.BlockSpec((ROW_TILE, D_MODEL), lambda i: (i, 0)),
            pl.BlockSpec((ROW_TILE, LANES), lambda i: (i, 0)),
        ],
        out_shape=[
            jax.ShapeDtypeStruct((n, D_MODEL), BF16),
            jax.ShapeDtypeStruct((n, LANES), F32),
        ],
        compiler_params=_cparams("parallel"),
        name="norm_gateproj",
    )(x, g, wif, bif)


def _matmul_kernel(a_ref, w_ref, o_ref):
    o_ref[...] = _dot(a_ref[...], w_ref[...]).astype(o_ref.dtype)


def _matmul_call(a, w, out_dtype, name):
    n, k = a.shape
    c = w.shape[1]
    tm = 2 * ROW_TILE
    return pl.pallas_call(
        _matmul_kernel,
        grid=(n // tm, c // COL_TILE),
        in_specs=[
            pl.BlockSpec((tm, k), lambda i, j: (i, 0)),
            pl.BlockSpec((k, COL_TILE), lambda i, j: (0, j)),
        ],
        out_specs=pl.BlockSpec((tm, COL_TILE), lambda i, j: (i, j)),
        out_shape=jax.ShapeDtypeStruct((n, c), out_dtype),
        compiler_params=_cparams("parallel", "parallel"),
        name=name,
    )(a, w)


def _sb_kernel(q_ref, k_ref, v_ref, u_ref, o_ref, acc_ref, carry_ref):
    t = ATT_TILE
    qi = pl.program_id(1)
    lane = lax.broadcasted_iota(jnp.int32, (1, LANES), 1)
    head_mask = (lane < HEAD_DIM, lane >= HEAD_DIM)
    n_heads = 2 * SB_PAIRS
    q = q_ref[...].astype(F32) * (HEAD_DIM ** -0.5 * LOG2_E)
    qh = []
    for p in range(SB_PAIRS):
        qp = q[:, p * LANES:(p + 1) * LANES]
        qh.extend(jnp.where(m, qp, 0.0).astype(BF16) for m in head_mask)
    u_tri = u_ref[...]

    acc_ref[...] = jnp.zeros_like(acc_ref)
    carry_ref[...] = jnp.zeros_like(carry_ref)

    def tile_step(kj, diagonal):
        start = pl.multiple_of(kj * t, t)
        if diagonal:
            row = lax.broadcasted_iota(jnp.int32, (t, t), 0)
            col = lax.broadcasted_iota(jnp.int32, (t, t), 1)
            past = col < row
        heads = range(n_heads)
        zs = [_dot_nt(qh[i], k_ref[pl.ds(start, t), (i // 2) * LANES:(i // 2 + 1) * LANES])
              for i in heads]
        sps = [jnp.maximum(z, 0.0) + jnp.log2(1.0 + jnp.exp2(-jnp.abs(z))) for z in zs]
        if diagonal:
            sps = [jnp.where(past, sp, 0.0) for sp in sps]
        rs = [_dot(sps[i].astype(BF16), u_tri)
              + jnp.concatenate([carry_ref[i]] * (t // LANES), axis=1) for i in heads]
        ws = [jnp.exp2((zs[i] - rs[i]).astype(BF16)) for i in heads]
        if diagonal:
            ws = [jnp.where(past, w, jnp.zeros_like(w)) for w in ws]
        carry_min = None
        for i in heads:
            v = v_ref[pl.ds(start, t), (i // 2) * LANES:(i // 2 + 1) * LANES]
            vh = jnp.where(head_mask[i % 2], v, jnp.zeros_like(v))
            acc_ref[i] += _dot(ws[i].astype(BF16), vh)
            carry = jnp.broadcast_to(rs[i][:, 0:1], (t, LANES))
            carry_ref[i] = carry
            carry_min = carry if i == 0 else jnp.minimum(carry_min, carry)
        return jnp.min(carry_min)

    def cond(c):
        it, carry_min = c
        return (it < qi) & (carry_min < SB_EXIT_LOG2)

    def body(c):
        it, _ = c
        return it + 1, tile_step(qi - 1 - it, False)

    lax.while_loop(cond, body, (jnp.int32(0), tile_step(qi, True)))
    for p in range(SB_PAIRS):
        o_ref[:, p * LANES:(p + 1) * LANES] = (acc_ref[2 * p] + acc_ref[2 * p + 1]).astype(o_ref.dtype)


def _sb_call(p16, u_tri, batch, seq):
    n = batch * seq
    t = ATT_TILE
    nq = seq // t
    groups = BRANCH_W // (SB_PAIRS * LANES)
    w = SB_PAIRS * LANES
    return pl.pallas_call(
        _sb_kernel,
        grid=(batch * groups, nq),
        in_specs=[
            pl.BlockSpec((t, w), lambda g, i: ((g // groups) * nq + i, g % groups)),
            pl.BlockSpec((seq, w), lambda g, i: (g // groups, groups + g % groups)),
            pl.BlockSpec((seq, w), lambda g, i: (g // groups, 2 * groups + g % groups)),
            pl.BlockSpec((t, t), lambda g, i: (0, 0)),
        ],
        out_specs=pl.BlockSpec((t, w), lambda g, i: ((g // groups) * nq + i, g % groups)),
        out_shape=jax.ShapeDtypeStruct((n, BRANCH_W), BF16),
        scratch_shapes=[
            pltpu.VMEM((2 * SB_PAIRS, t, LANES), F32),
            pltpu.VMEM((2 * SB_PAIRS, t, LANES), F32),
        ],
        compiler_params=_cparams("parallel", "arbitrary"),
        name="stick_breaking_attention",
    )(p16, p16, p16, u_tri)


def _moba_prep_kernel(q_ref, k_ref, cos_ref, sin_ref, qg_ref, kg_ref, seg_ref, rot_ref,
                      qa_ref, ka_ref, km_ref):
    blk = pl.program_id(1)
    lane = lax.broadcasted_iota(jnp.int32, (1, LANES), 1)
    even = lane < HEAD_DIM
    cos = cos_ref[...]
    sin = sin_ref[...]
    seg = seg_ref[...]
    rot = rot_ref[...]

    def norm_rope(x, g):
        ss = _dot_2term_rhs(x * x, seg)
        xn = x * lax.rsqrt(ss * (1.0 / HEAD_DIM) + RMS_EPS) * g
        return xn * cos + _dot_2term_rhs(xn, rot) * sin

    onehot_even = jnp.where(lane == HEAD_DIM + blk, 1.0, 0.0)
    onehot_odd = jnp.where(lane == blk, 1.0, 0.0)

    km_rows = []
    for c in range(BRANCH_W // LANES):
        sl = slice(c * LANES, (c + 1) * LANES)
        qr = norm_rope(q_ref[:, sl], qg_ref[...]) * (HEAD_DIM ** -0.5 * LOG2_E)
        kr = norm_rope(k_ref[:, sl], kg_ref[...])
        qa_ref[0, 2 * c] = jnp.where(even, qr, 0.0).astype(BF16)
        qa_ref[0, 2 * c + 1] = jnp.where(even, 0.0, qr).astype(BF16)
        ka_ref[0, 2 * c] = jnp.where(even, kr, onehot_even).astype(BF16)
        ka_ref[0, 2 * c + 1] = jnp.where(even, onehot_odd, kr).astype(BF16)
        kmean = jnp.mean(kr, axis=0, keepdims=True)
        km_rows.append(jnp.where(even, kmean, 0.0))
        km_rows.append(jnp.where(even, 0.0, kmean))
    km_ref[0, 0] = jnp.concatenate(km_rows, axis=0)


def _moba_prep_call(p32, cos_t, sin_t, qg, kg, seg, rot, batch, seq):
    t = MB_BLOCK
    nb = seq // t
    h = N_ATT_HEADS
    return pl.pallas_call(
        _moba_prep_kernel,
        grid=(batch, nb),
        in_specs=[
            pl.BlockSpec((t, BRANCH_W), lambda b, i: (b * nb + i, 0)),
            pl.BlockSpec((t, BRANCH_W), lambda b, i: (b * nb + i, 1)),
            pl.BlockSpec((t, LANES), lambda b, i: (b * nb + i, 0)),
            pl.BlockSpec((t, LANES), lambda b, i: (b * nb + i, 0)),
            pl.BlockSpec((1, LANES), lambda b, i: (0, 0)),
            pl.BlockSpec((1, LANES), lambda b, i: (0, 0)),
            pl.BlockSpec((LANES, LANES), lambda b, i: (0, 0)),
            pl.BlockSpec((LANES, LANES), lambda b, i: (0, 0)),
        ],
        out_specs=[
            pl.BlockSpec((1, h, t, LANES), lambda b, i: (b, 0, i, 0)),
            pl.BlockSpec((1, h, t, LANES), lambda b, i: (b, 0, i, 0)),
            pl.BlockSpec((1, 1, h, LANES), lambda b, i: (b, i, 0, 0)),
        ],
        out_shape=[
            jax.ShapeDtypeStruct((batch, h, seq, LANES), BF16),
            jax.ShapeDtypeStruct((batch, h, seq, LANES), BF16),
            jax.ShapeDtypeStruct((batch, nb, h, LANES), F32),
        ],
        compiler_params=_cparams("parallel", "parallel"),
        name="moba_qk_prep",
    )(p32, p32, cos_t, sin_t, qg, kg, seg, rot)


def _moba_gate_kernel(qa_ref, km_ref, eye_ref, qb_ref, *, nb):
    t = ATT_TILE
    seq = qa_ref.shape[2]
    blk_row = lax.broadcasted_iota(jnp.int32, (nb, seq), 0)
    own = lax.broadcasted_iota(jnp.int32, (nb, seq), 1) // t
    for h in range(2):
        q = qa_ref[0, h]
        km_hi, km_mid, km_lo = _split3(km_ref[0, h])
        g = _dot_nt(km_hi, q) + _dot_nt(km_mid, q) + _dot_nt(km_lo, q)
        cnt = jnp.zeros((nb, seq), jnp.int32)
        for m in range(nb):
            gm = g[m:m + 1, :]
            beats = (gm > g) | ((gm == g) & (m < blk_row))
            cnt = cnt + jnp.where(beats & (m < own), 1, 0)
        keep = ((blk_row < own) & (cnt < MB_TOPK)) | (blk_row == own)
        bias = jnp.where(keep, 0.0, NEG_BIG)
        aux_lo = HEAD_DIM if h == 0 else 0
        pieces = []
        if aux_lo:
            pieces.append(jnp.zeros((aux_lo, seq), F32))
        pieces.append(bias)
        pieces.append(jnp.zeros((LANES - aux_lo - nb, seq), F32))
        bias_rows = jnp.concatenate(pieces, axis=0).astype(BF16)
        for j in range(seq // t):
            sl = slice(j * t, (j + 1) * t)
            bias_cols = _dot_nt(eye_ref[...], bias_rows[:, sl])
            qb_ref[0, h, sl, :] = (q[sl, :].astype(F32) + bias_cols).astype(BF16)


def _moba_gate_call(qa, km, eye, batch, seq):
    nb = seq // MB_BLOCK
    pairs = N_ATT_HEADS // 2
    spec = pl.BlockSpec((1, 2, seq, LANES), lambda b, p: (b, p, 0, 0))
    return pl.pallas_call(
        functools.partial(_moba_gate_kernel, nb=nb),
        grid=(batch, pairs),
        in_specs=[
            spec,
            pl.BlockSpec((1, 2, nb, LANES), lambda b, p: (b, p, 0, 0)),
            pl.BlockSpec((ATT_TILE, ATT_TILE), lambda b, p: (0, 0)),
        ],
        out_specs=spec,
        out_shape=jax.ShapeDtypeStruct(qa.shape, BF16),
        compiler_params=_cparams("parallel", "parallel"),
        name="moba_block_gate",
    )(qa, km, eye)


def _moba_kernel(qb_ref, ka_ref, v_ref, o_ref, s_scr, mx_ref, acc_ref):
    t = ATT_TILE
    own = pl.program_id(1)
    n_heads = 2 * MB_PAIRS
    lane = lax.broadcasted_iota(jnp.int32, (1, LANES), 1)
    head_mask = (lane < HEAD_DIM, lane >= HEAD_DIM)

    heads = range(n_heads)

    def lane_fold(x, op):
        parts = [x[:, c * LANES:(c + 1) * LANES] for c in range(t // LANES)]
        return functools.reduce(op, parts)

    def score_step(kj, diagonal):
        start = pl.multiple_of(kj * t, t)
        ss = [_dot_nt(qb_ref[0, h], ka_ref[0, h, pl.ds(start, t), :]) for h in heads]
        if diagonal:
            row = lax.broadcasted_iota(jnp.int32, (t, t), 0)
            col = lax.broadcasted_iota(jnp.int32, (t, t), 1)
            ss = [jnp.where(col <= row, s, NEG_BIG) for s in ss]
        for h in heads:
            s_scr[h, kj] = ss[h]
            part = lane_fold(ss[h], jnp.maximum)
            mx_ref[h] = part if diagonal else jnp.maximum(mx_ref[h], part)

    score_step(own, True)

    def score_body(it, c):
        score_step(own - 1 - it, False)
        return c

    lax.fori_loop(0, own, score_body, 0)

    for h in heads:
        mx_ref[h] = jnp.broadcast_to(jnp.max(mx_ref[h], axis=-1, keepdims=True), (t, LANES))
    acc_ref[...] = jnp.zeros_like(acc_ref)

    ones_lane = (HEAD_DIM, 0)
    ones_col = [jnp.where(lane == ones_lane[e], 1.0, 0.0).astype(BF16) for e in range(2)]

    def value_body(kj, c):
        start = pl.multiple_of(kj * t, t)
        ps = [jnp.exp2((s_scr[h, kj] - jnp.concatenate([mx_ref[h]] * (t // LANES), axis=1))
                       .astype(BF16)) for h in heads]
        for h in heads:
            v = v_ref[pl.ds(start, t), (h // 2) * LANES:(h // 2 + 1) * LANES]
            vh = jnp.where(head_mask[h % 2], v, ones_col[h % 2])
            acc_ref[h] += _dot(ps[h], vh)
        return c

    lax.fori_loop(0, own + 1, value_body, 0)

    for pr in range(MB_PAIRS):
        acc0 = acc_ref[2 * pr]
        acc1 = acc_ref[2 * pr + 1]
        inv0 = 1.0 / acc0[:, ones_lane[0]:ones_lane[0] + 1]
        inv1 = 1.0 / acc1[:, ones_lane[1]:ones_lane[1] + 1]
        out = jnp.where(head_mask[0], acc0 * inv0, acc1 * inv1)
        o_ref[:, pr * LANES:(pr + 1) * LANES] = out.astype(o_ref.dtype)


def _moba_call(qb, ka, p16, batch, seq):
    n = batch * seq
    t = ATT_TILE
    nq = seq // t
    nb = seq // MB_BLOCK
    groups = BRANCH_W // (MB_PAIRS * LANES)
    nh = 2 * MB_PAIRS
    w = MB_PAIRS * LANES
    v_col0 = 3 * BRANCH_W // w
    return pl.pallas_call(
        _moba_kernel,
        grid=(batch * groups, nq),
        in_specs=[
            pl.BlockSpec((1, nh, t, LANES), lambda g, i: (g // groups, g % groups, i, 0)),
            pl.BlockSpec((1, nh, seq, LANES), lambda g, i: (g // groups, g % groups, 0, 0)),
            pl.BlockSpec((seq, w), lambda g, i: (g // groups, v_col0 + g % groups)),
        ],
        out_specs=pl.BlockSpec((t, w), lambda g, i: ((g // groups) * nq + i, g % groups)),
        out_shape=jax.ShapeDtypeStruct((n, BRANCH_W), BF16),
        scratch_shapes=[
            pltpu.VMEM((nh, nb, t, t), F32),
            pltpu.VMEM((nh, t, LANES), F32),
            pltpu.VMEM((nh, t, LANES), F32),
        ],
        compiler_params=_cparams("parallel", "arbitrary"),
        name="moba_attention",
    )(qb, ka, p16)


def _mlstm_kernel(u_ref, v_ref, o_ref, ifc_ref, ift_ref, cw_ref, ltri_ref, utri_ref, y_ref,
                  xbuf, c_ref, n_ref, m_ref):
    L = ML_CHUNK
    W = BRANCH_W
    halo = SUBLANES

    @pl.when(pl.program_id(1) == 0)
    def _():
        xbuf[0:halo, :] = jnp.zeros((halo, 2 * W), F32)
        c_ref[...] = jnp.zeros_like(c_ref)
        n_ref[...] = jnp.zeros_like(n_ref)
        m_ref[...] = jnp.zeros_like(m_ref)

    xbuf[halo:, :] = u_ref[...]
    conv = jnp.zeros((L, 2 * W), F32)
    for j in range(CONV_W):
        off = halo - (CONV_W - 1) + j
        conv = conv + cw_ref[j:j + 1, :] * xbuf[off:off + L, :]
    xbuf[0:halo, :] = u_ref[L - halo:, :]
    qk = conv * _sigmoid(conv)

    ift = ift_ref[0]
    lf_rows = _log_sigmoid(ift)
    bcum_rows = _dot_exact_rhs(lf_rows, utri_ref[...])
    lf_cols = _log_sigmoid(ifc_ref[...])
    bcum_cols = _dot_exact_lhs(ltri_ref[...], lf_cols)

    row = lax.broadcasted_iota(jnp.int32, (L, L), 0)
    col = lax.broadcasted_iota(jnp.int32, (L, L), 1)
    causal = col <= row

    for h in range(ML_HEADS):
        sl = slice(h * ML_HDIM, (h + 1) * ML_HDIM)
        q = qk[:, sl].astype(BF16)
        k = (qk[:, W + h * ML_HDIM:W + (h + 1) * ML_HDIM] * (ML_HDIM ** -0.5)).astype(BF16)
        v = v_ref[:, sl]
        a_row = ift[h:h + 1, :] - bcum_rows[ML_HEADS + h:ML_HEADS + h + 1, :]
        bc = bcum_cols[:, ML_HEADS + h:ML_HEADS + h + 1]
        m_prev = m_ref[h, 0:1, 0:1]

        a_mat = jnp.where(causal, a_row, NEG_BIG)
        mu = jnp.maximum(jnp.max(a_mat, axis=-1, keepdims=True), m_prev)
        w_intra = jnp.exp(a_mat - mu)
        w_inter = jnp.exp(m_prev - mu)
        sc = _dot_nt(q, k) * w_intra
        c_prev = c_ref[h]
        n_prev = n_ref[h, 0:1, :]
        num = _dot(sc.astype(BF16), v) + w_inter * _dot_nt(q, c_prev.astype(BF16))
        qn = jnp.sum(q.astype(F32) * n_prev, axis=-1, keepdims=True)
        den = jnp.sum(sc, axis=-1, keepdims=True) + w_inter * qn
        m_t = bc + mu
        hs = num / jnp.maximum(jnp.abs(den), jnp.exp(-m_t))
        gate = _sigmoid(o_ref[:, sl].astype(F32))
        y_ref[:, sl] = (hs * gate).astype(y_ref.dtype)

        mu_last = mu[L - 1:L, :]
        m_new = bc[L - 1:L, :] + mu_last
        decay = jnp.exp(m_prev - mu_last)
        wk_row = jnp.exp(a_row - mu_last)
        vt = v.astype(F32).T
        c_ref[h] = decay * c_prev + _dot((vt * wk_row).astype(BF16), k)
        wk8 = jnp.broadcast_to(wk_row, (SUBLANES, L)).astype(BF16)
        n_ref[h] = decay * n_ref[h] + _dot(wk8, k)
        m_ref[h] = jnp.broadcast_to(m_new, (SUBLANES, LANES))


def _mlstm_call(p32, p16, ifc, ift, conv_w, ltri, utri, batch, seq):
    n = batch * seq
    L = ML_CHUNK
    nc = seq // L
    W = BRANCH_W
    return pl.pallas_call(
        _mlstm_kernel,
        grid=(batch, nc),
        in_specs=[
            pl.BlockSpec((L, 2 * W), lambda b, i: (b * nc + i, 1)),
            pl.BlockSpec((L, W), lambda b, i: (b * nc + i, 4)),
            pl.BlockSpec((L, W), lambda b, i: (b * nc + i, 5)),
            pl.BlockSpec((L, LANES), lambda b, i: (b * nc + i, 0)),
            pl.BlockSpec((1, SUBLANES, L), lambda b, i: (b, 0, i)),
            pl.BlockSpec((CONV_W, 2 * W), lambda b, i: (0, 0)),
            pl.BlockSpec((L, L), lambda b, i: (0, 0)),
            pl.BlockSpec((L, L), lambda b, i: (0, 0)),
        ],
        out_specs=pl.BlockSpec((L, W), lambda b, i: (b * nc + i, 0)),
        out_shape=jax.ShapeDtypeStruct((n, W), BF16),
        scratch_shapes=[
            pltpu.VMEM((L + SUBLANES, 2 * W), F32),
            pltpu.VMEM((ML_HEADS, ML_HDIM, ML_HDIM), F32),
            pltpu.VMEM((ML_HEADS, SUBLANES, ML_HDIM), F32),
            pltpu.VMEM((ML_HEADS, SUBLANES, LANES), F32),
        ],
        compiler_params=_cparams("parallel", "arbitrary"),
        name="mlstm",
    )(p32, p16, p16, ifc, ift, conv_w, ltri, utri)


def _merge_kernel(ysb_ref, ymb_ref, yml_ref, gl_ref, x_ref, wb_ref, wo_ref, g2_ref, wr_ref, br_ref,
                  h_ref, xn_ref, route_ref, route_t_ref):
    merged = jnp.zeros((ROW_TILE, D_MODEL), F32)
    for b, y_ref in enumerate((ysb_ref, ymb_ref, yml_ref)):
        gate = _sigmoid(gl_ref[:, b * D_MODEL:(b + 1) * D_MODEL].astype(F32))
        merged = merged + gate * _dot(y_ref[...], wb_ref[b])
    hres = x_ref[...] + _dot(merged.astype(BF16), wo_ref[...])
    h_ref[...] = hres

    ms = jnp.mean(hres * hres, axis=-1, keepdims=True)
    xn = (hres * lax.rsqrt(ms + RMS_EPS) * g2_ref[...]).astype(BF16)
    xn_ref[...] = xn

    logits = _dot(xn, wr_ref[...]) + br_ref[...]
    lane = lax.broadcasted_iota(jnp.int32, (ROW_TILE, LANES), 1).astype(F32)
    far = float(LANES)
    is_g = lane < N_GROUPS
    gl = jnp.where(is_g, logits, NEG_BIG)
    gmax = jnp.max(gl, axis=-1, keepdims=True)
    gsum = jnp.sum(jnp.where(is_g, jnp.exp(gl - gmax), 0.0), axis=-1, keepdims=True)
    g_w = 1.0 / gsum
    g_idx = jnp.min(jnp.where(is_g & (gl == gmax), lane, far), axis=-1, keepdims=True)
    e_lo = N_GROUPS + g_idx * EXPERTS_PER_GROUP
    in_grp = (lane >= e_lo) & (lane < e_lo + EXPERTS_PER_GROUP)
    el = jnp.where(in_grp, logits, NEG_BIG)
    l1 = jnp.max(el, axis=-1, keepdims=True)
    i1 = jnp.min(jnp.where(in_grp & (el == l1), lane, far), axis=-1, keepdims=True)
    el2 = jnp.where(lane == i1, NEG_BIG, el)
    l2 = jnp.max(el2, axis=-1, keepdims=True)
    i2 = jnp.min(jnp.where(in_grp & (el2 == l2), lane, far), axis=-1, keepdims=True)
    p2 = jnp.exp(l2 - l1)
    w1 = g_w / (1.0 + p2)
    w2 = g_w * p2 / (1.0 + p2)
    slot = lane + e_lo
    cw = jnp.where(slot == i1, w1, 0.0) + jnp.where(slot == i2, w2, 0.0)
    cw = jnp.where(lane < EXPERTS_PER_GROUP, cw, 0.0)
    route = jnp.where(lane == EXPERTS_PER_GROUP, g_idx, cw)
    route_ref[...] = route
    route_t_ref[...] = route.T[0:2 * SUBLANES, :]


def _merge_call(ysb, ymb, yml, p16, x, wb, wo, g2, wr, br):
    n = x.shape[0]
    row = lambda i: (i, 0)
    const2 = lambda i: (0, 0)
    return pl.pallas_call(
        _merge_kernel,
        grid=(n // ROW_TILE,),
        in_specs=[
            pl.BlockSpec((ROW_TILE, BRANCH_W), row),
            pl.BlockSpec((ROW_TILE, BRANCH_W), row),
            pl.BlockSpec((ROW_TILE, BRANCH_W), row),
            pl.BlockSpec((ROW_TILE, N_BRANCH * D_MODEL), lambda i: (i, 1)),
            pl.BlockSpec((ROW_TILE, D_MODEL), row),
            pl.BlockSpec((N_BRANCH, BRANCH_W, D_MODEL), lambda i: (0, 0, 0)),
            pl.BlockSpec((D_MODEL, D_MODEL), const2),
            pl.BlockSpec((1, D_MODEL), const2),
            pl.BlockSpec((D_MODEL, LANES), const2),
            pl.BlockSpec((1, LANES), const2),
        ],
        out_specs=[
            pl.BlockSpec((ROW_TILE, D_MODEL), row),
            pl.BlockSpec((ROW_TILE, D_MODEL), row),
            pl.BlockSpec((ROW_TILE, LANES), row),
            pl.BlockSpec((2 * SUBLANES, ROW_TILE), lambda i: (0, i)),
        ],
        out_shape=[
            jax.ShapeDtypeStruct((n, D_MODEL), F32),
            jax.ShapeDtypeStruct((n, D_MODEL), BF16),
            jax.ShapeDtypeStruct((n, LANES), F32),
            jax.ShapeDtypeStruct((2 * SUBLANES, n), F32),
        ],
        compiler_params=_cparams("parallel"),
        name="merge_outproj_router",
    )(ysb, ymb, yml, p16, x, wb, wo, g2, wr, br)


def _moe_kernel(x_ref, h_ref, rt_ref, rtt_ref, lx_ref, w1_ref, w3_ref, w2_ref, y_ref,
                keyc_ref, keyr_ref):
    tb = MOE_BLOCK
    sub = MOE_SUB
    grp = pl.program_id(1).astype(F32)
    gid_lane = EXPERTS_PER_GROUP
    gid_col = rt_ref[:, gid_lane:gid_lane + 1]
    route_t = rtt_ref[...]
    gid_row = route_t[gid_lane:gid_lane + 1, :]

    @pl.when(pl.program_id(1) == 0)
    def _():
        y_ref[...] = h_ref[...]
        lx = lx_ref[...]
        lane = lax.broadcasted_iota(jnp.int32, (tb, LANES), 1).astype(F32)
        own_col = lane == gid_col
        before_col = _dot(lx, jnp.where(own_col, 1.0, 0.0).astype(BF16))
        pos_col = jnp.sum(jnp.where(own_col, before_col, 0.0), axis=-1, keepdims=True)
        keyc_ref[...] = jnp.broadcast_to(pos_col, (tb, LANES))
        row = lax.broadcasted_iota(jnp.int32, (SUBLANES, tb), 0).astype(F32)
        own_row = row == gid_row
        before_row = _dot_nt(jnp.where(own_row, 1.0, 0.0).astype(BF16), lx)
        pos_row = jnp.sum(jnp.where(own_row, before_row, 0.0), axis=0, keepdims=True)
        keyr_ref[...] = jnp.broadcast_to(pos_row, (SUBLANES, tb))

    in_col = gid_col == grp
    in_row = gid_row == grp
    key_col = jnp.where(in_col, keyc_ref[...], -1.0)
    key_col = jnp.concatenate([key_col] * (sub // LANES), axis=1)
    key_row = jnp.where(in_row, keyr_ref[0:1, :], -1.0)
    count = jnp.sum(jnp.where(in_row, 1.0, 0.0)).astype(jnp.int32)
    rem = count % sub
    short_tail = (rem > 0) & (rem <= sub // 2)
    n_sub = count // sub + jnp.where(rem > sub // 2, 1, 0)
    t_hi, t_mid, t_lo = _split3(route_t)

    def experts_of_sub_tile(base, rows):
        slot_rows = lax.broadcasted_iota(jnp.int32, (rows, tb), 0).astype(F32) + base
        pick = jnp.where(key_row == slot_rows, 1.0, 0.0).astype(BF16)
        xs = _dot(pick, x_ref[...]).astype(BF16)
        cw_t = _dot_nt(t_hi, pick) + _dot_nt(t_mid, pick) + _dot_nt(t_lo, pick)
        cw = jnp.concatenate(
            [cw_t, jnp.zeros((LANES - 2 * SUBLANES, rows), F32)], axis=0).T
        experts = range(EXPERTS_PER_GROUP)
        ups = [(_dot(xs, w1_ref[e]), _dot(xs, w3_ref[e])) for e in experts]
        hid = jnp.concatenate(
            [(a * _sigmoid(a) * b * cw[:, e:e + 1]).astype(BF16) for e, (a, b) in enumerate(ups)],
            axis=1)
        y = _dot(hid, w2_ref[...].reshape(EXPERTS_PER_GROUP * D_EXPERT, D_MODEL))
        return y.astype(BF16)

    def sub_tile(base, rows):
        y = experts_of_sub_tile(base, rows)
        slot_cols = lax.broadcasted_iota(jnp.int32, (tb, rows), 1).astype(F32) + base
        put = jnp.where(key_col[:, :rows] == slot_cols, 1.0, 0.0).astype(BF16)
        y_ref[...] += _dot(put, y)

    def full_tile(s, c):
        sub_tile((s * sub).astype(F32), sub)
        return c

    lax.fori_loop(0, n_sub, full_tile, 0)

    @pl.when(short_tail)
    def _():
        sub_tile((n_sub * sub).astype(F32), sub // 2)


def _moe_call(xn2, h, route, route_t, lx, w1b, w3b, w2b):
    n = xn2.shape[0]
    tb = MOE_BLOCK
    e = EXPERTS_PER_GROUP
    return pl.pallas_call(
        _moe_kernel,
        grid=(n // tb, N_GROUPS),
        in_specs=[
            pl.BlockSpec((tb, D_MODEL), lambda i, g: (i, 0)),
            pl.BlockSpec((tb, D_MODEL), lambda i, g: (i, 0)),
            pl.BlockSpec((tb, LANES), lambda i, g: (i, 0)),
            pl.BlockSpec((2 * SUBLANES, tb), lambda i, g: (0, i)),
            pl.BlockSpec((tb, tb), lambda i, g: (0, 0)),
            pl.BlockSpec((e, D_MODEL, D_EXPERT), lambda i, g: (g, 0, 0)),
            pl.BlockSpec((e, D_MODEL, D_EXPERT), lambda i, g: (g, 0, 0)),
            pl.BlockSpec((e, D_EXPERT, D_MODEL), lambda i, g: (g, 0, 0)),
        ],
        out_specs=pl.BlockSpec((tb, D_MODEL), lambda i, g: (i, 0)),
        out_shape=jax.ShapeDtypeStruct((n, D_MODEL), F32),
        scratch_shapes=[
            pltpu.VMEM((tb, LANES), F32),
            pltpu.VMEM((SUBLANES, tb), F32),
        ],
        compiler_params=_cparams("parallel", "arbitrary"),
        name="moe_group_experts",
    )(xn2, h, route, route_t, lx, w1b, w3b, w2b)


def _tables(positions):
    t = ATT_TILE
    r = jnp.arange(t)
    u_tri = (r[:, None] >= r[None, :]).astype(BF16)
    eye = jnp.eye(t, dtype=BF16)
    L = ML_CHUNK
    rl = jnp.arange(L)
    ltri = (rl[:, None] >= rl[None, :]).astype(BF16)
    utri = ltri.T

    li = jnp.arange(LANES)
    seg = ((li[:, None] // HEAD_DIM) == (li[None, :] // HEAD_DIM)).astype(BF16)
    half = ROPE_DIM // 2
    src, dst = li[:, None], li[None, :]
    d_in = dst % HEAD_DIM
    rot = jnp.where((d_in < half) & (src == dst + half), -1.0,
                    jnp.where((d_in >= half) & (d_in < ROPE_DIM) & (src == dst - half), 1.0, 0.0)
                    ).astype(BF16)

    inv_freq = jnp.power(jnp.float32(ROPE_THETA), -jnp.arange(half, dtype=F32) / half)
    lane_in = li % HEAD_DIM
    ang = positions.reshape(-1).astype(F32)[:, None] * inv_freq[lane_in % half][None, :]
    rotated = (lane_in < ROPE_DIM)[None, :]
    cos_t = jnp.where(rotated, jnp.cos(ang), 1.0)
    sin_t = jnp.where(rotated, jnp.sin(ang), 0.0)
    rb = jnp.arange(MOE_BLOCK)
    lx = (rb[None, :] < rb[:, None]).astype(BF16)
    return dict(u_tri=u_tri, eye=eye, ltri=ltri, utri=utri, seg=seg, rot=rot, cos=cos_t, sin=sin_t,
                lx=lx)


def kernel(x, positions, norm1_g, w_in, qn_g, kn_g, conv_w, ml_gate_b, w_branch, w_out, norm2_g,
           w_rg, b_rg, w_re, b_re, w1, w3, w2):
    batch, seq, d = x.shape
    n = batch * seq
    W = BRANCH_W
    depth = w_in.shape[0]
    tb = _tables(positions)
    xf = x.reshape(n, d)

    for l in range(depth):
        wl = w_in[l]
        w16 = jnp.concatenate(
            [wl[:, 0:3 * W], wl[:, 5 * W:6 * W], wl[:, 8 * W:10 * W], wl[:, 10 * W + 2 * ML_HEADS:]],
            axis=1).astype(BF16)
        w32 = jnp.concatenate([wl[:, 3 * W:5 * W], wl[:, 6 * W:8 * W]], axis=1).astype(BF16)
        n_if = 2 * ML_HEADS
        wif = jnp.pad(wl[:, 10 * W:10 * W + n_if], ((0, 0), (0, LANES - n_if))).astype(BF16)
        bif = jnp.pad(ml_gate_b[l], (0, LANES - n_if)).reshape(1, LANES)

        xn, ifc = _norm_call(xf, norm1_g[l].reshape(1, d), wif, bif)
        p16 = _matmul_call(xn, w16, BF16, "inproj_bf16")
        p32 = _matmul_call(xn, w32, F32, "inproj_f32")

        y_sb = _sb_call(p16, tb["u_tri"], batch, seq)

        qg = jnp.tile(qn_g[l], LANES // HEAD_DIM).reshape(1, LANES)
        kg = jnp.tile(kn_g[l], LANES // HEAD_DIM).reshape(1, LANES)
        qa, ka, km = _moba_prep_call(p32, tb["cos"], tb["sin"], qg, kg, tb["seg"], tb["rot"],
                                     batch, seq)
        qb = _moba_gate_call(qa, km.transpose(0, 2, 1, 3), tb["eye"], batch, seq)
        y_mb = _moba_call(qb, ka, p16, batch, seq)

        ift = ifc[:, :SUBLANES].reshape(batch, seq, SUBLANES).transpose(0, 2, 1)
        y_ml = _mlstm_call(p32, p16, ifc, ift, conv_w[l], tb["ltri"], tb["utri"], batch, seq)

        wr = jnp.pad(jnp.concatenate([w_rg[l], w_re[l]], axis=1),
                     ((0, 0), (0, LANES - N_GROUPS - N_EXPERTS))).astype(BF16)
        br = jnp.pad(jnp.concatenate([b_rg[l], b_re[l]]),
                     (0, LANES - N_GROUPS - N_EXPERTS)).reshape(1, LANES)
        hres, xn2, route, route_t = _merge_call(
            y_sb, y_mb, y_ml, p16, xf, w_branch[l].astype(BF16), w_out[l].astype(BF16),
            norm2_g[l].reshape(1, d), wr, br)
        xf = _moe_call(xn2, hres, route, route_t, tb["lx"], w1[l].astype(BF16),
                       w3[l].astype(BF16), w2[l].astype(BF16))

    return xf.reshape(batch, seq, d)
```

```python
import functools

import jax
import jax.numpy as jnp
from jax import lax
from jax.experimental import pallas as pl
from jax.experimental.pallas import tpu as pltpu

D_MODEL = 1024
BRANCH_W = D_MODEL // 2
N_BRANCH = 3
HEAD_DIM = 64
N_ATT_HEADS = BRANCH_W // HEAD_DIM
MB_BLOCK = 256
MB_TOPK = 3
ROPE_THETA = 500000.0
ROPE_DIM = HEAD_DIM // 4
ML_HEADS = 4
ML_HDIM = BRANCH_W // ML_HEADS
CONV_W = 4
N_GROUPS = 4
EXPERTS_PER_GROUP = 8
N_EXPERTS = N_GROUPS * EXPERTS_PER_GROUP
D_EXPERT = D_MODEL // 4
RMS_EPS = 1e-6

LANES = 128
SUBLANES = 8
VMEM_LIMIT_BYTES = 56 * 1024 * 1024

ROW_TILE = 512
COL_TILE = 1024
ATT_TILE = 256
ML_CHUNK = 256
MOE_BLOCK = 1024
MOE_SUB = 256

LOG2_E = 1.4426950408889634
SB_PAIRS = 4
SB_EXIT_LOG2 = 160.0
MB_PAIRS = 4
NEG_BIG = -(2.0 ** 100)

F32 = jnp.float32
BF16 = jnp.bfloat16


def _cparams(*sem):
    return pltpu.CompilerParams(dimension_semantics=sem, vmem_limit_bytes=VMEM_LIMIT_BYTES)


def _layer_spec(layer, block, index_map):
    return pl.BlockSpec((None,) + tuple(block), lambda *idx: (layer,) + tuple(index_map(*idx)))


def _dot(a, b):
    return jnp.dot(a, b, preferred_element_type=F32)


def _dot_nt(a, b):
    return lax.dot_general(a, b, (((1,), (1,)), ((), ())), preferred_element_type=F32)


def _split3(x):
    hi = x.astype(BF16)
    r1 = x - hi.astype(F32)
    mid = r1.astype(BF16)
    lo = (r1 - mid.astype(F32)).astype(BF16)
    return hi, mid, lo


def _dot_exact_rhs(x, m):
    hi, mid, lo = _split3(x)
    return _dot(hi, m) + _dot(mid, m) + _dot(lo, m)


def _dot_2term_rhs(x, m):
    hi = x.astype(BF16)
    lo = (x - hi.astype(F32)).astype(BF16)
    return _dot(hi, m) + _dot(lo, m)


def _dot_exact_lhs(m, x):
    hi, mid, lo = _split3(x)
    return _dot(m, hi) + _dot(m, mid) + _dot(m, lo)


def _log_sigmoid(x):
    return jnp.minimum(x, 0.0) - jnp.log(1.0 + jnp.exp(-jnp.abs(x)))


def _sigmoid(x):
    return 0.5 * jnp.tanh(0.5 * x) + 0.5


def _norm_kernel(x_ref, g_ref, wif_ref, bif_ref, xn_ref, if_ref):
    x = x_ref[...]
    ms = jnp.mean(x * x, axis=-1, keepdims=True)
    xn = (x * lax.rsqrt(ms + RMS_EPS) * g_ref[...]).astype(BF16)
    xn_ref[...] = xn
    if_ref[...] = _dot(xn, wif_ref[...]) + bif_ref[...]


def _norm_call(x, g, wif, bif, layer):
    n = x.shape[0]
    return pl.pallas_call(
        _norm_kernel,
        grid=(n // ROW_TILE,),
        in_specs=[
            pl.BlockSpec((ROW_TILE, D_MODEL), lambda i: (i, 0)),
            _layer_spec(layer, (1, D_MODEL), lambda i: (0, 0)),
            _layer_spec(layer, (D_MODEL, LANES), lambda i: (0, 0)),
            _layer_spec(layer, (1, LANES), lambda i: (0, 0)),
        ],
        out_specs=[
            pl.BlockSpec((ROW_TILE, D_MODEL), lambda i: (i, 0)),
            pl.BlockSpec((ROW_TILE, LANES), lambda i: (i, 0)),
        ],
        out_shape=[
            jax.ShapeDtypeStruct((n, D_MODEL), BF16),
            jax.ShapeDtypeStruct((n, LANES), F32),
        ],
        compiler_params=_cparams("parallel"),
        name="norm_gateproj",
    )(x, g, wif, bif)


def _matmul_kernel(a_ref, w_ref, o_ref):
    o_ref[...] = _dot(a_ref[...], w_ref[...]).astype(o_ref.dtype)


def _matmul_call(a, w, layer, out_dtype, name):
    n, k = a.shape
    c = w.shape[2]
    tm = 2 * ROW_TILE
    return pl.pallas_call(
        _matmul_kernel,
        grid=(n // tm, c // COL_TILE),
        in_specs=[
            pl.BlockSpec((tm, k), lambda i, j: (i, 0)),
            _layer_spec(layer, (k, COL_TILE), lambda i, j: (0, j)),
        ],
        out_specs=pl.BlockSpec((tm, COL_TILE), lambda i, j: (i, j)),
        out_shape=jax.ShapeDtypeStruct((n, c), out_dtype),
        compiler_params=_cparams("parallel", "parallel"),
        name=name,
    )(a, w)


def _sb_kernel(q_ref, k_ref, v_ref, u_ref, o_ref, acc_ref, carry_ref):
    t = ATT_TILE
    qi = pl.program_id(1)
    lane = lax.broadcasted_iota(jnp.int32, (1, LANES), 1)
    head_mask = (lane < HEAD_DIM, lane >= HEAD_DIM)
    n_heads = 2 * SB_PAIRS
    q = q_ref[...].astype(F32) * (HEAD_DIM ** -0.5 * LOG2_E)
    qh = []
    for p in range(SB_PAIRS):
        qp = q[:, p * LANES:(p + 1) * LANES]
        qh.extend(jnp.where(m, qp, 0.0).astype(BF16) for m in head_mask)
    u_tri = u_ref[...]

    acc_ref[...] = jnp.zeros_like(acc_ref)
    carry_ref[...] = jnp.zeros_like(carry_ref)

    def tile_step(kj, diagonal):
        start = pl.multiple_of(kj * t, t)
        if diagonal:
            row = lax.broadcasted_iota(jnp.int32, (t, t), 0)
            col = lax.broadcasted_iota(jnp.int32, (t, t), 1)
            past = col < row
        heads = range(n_heads)
        zs = [_dot_nt(qh[i], k_ref[pl.ds(start, t), (i // 2) * LANES:(i // 2 + 1) * LANES])
              for i in heads]
        sps = [jnp.maximum(z, 0.0) + jnp.log2(1.0 + jnp.exp2(-jnp.abs(z))) for z in zs]
        if diagonal:
            sps = [jnp.where(past, sp, 0.0) for sp in sps]
        rs = [_dot(sps[i].astype(BF16), u_tri)
              + jnp.concatenate([carry_ref[i]] * (t // LANES), axis=1) for i in heads]
        ws = [jnp.exp2((zs[i] - rs[i]).astype(BF16)) for i in heads]
        if diagonal:
            ws = [jnp.where(past, w, jnp.zeros_like(w)) for w in ws]
        carry_min = None
        for i in heads:
            v = v_ref[pl.ds(start, t), (i // 2) * LANES:(i // 2 + 1) * LANES]
            vh = jnp.where(head_mask[i % 2], v, jnp.zeros_like(v))
            acc_ref[i] += _dot(ws[i].astype(BF16), vh)
            carry = jnp.broadcast_to(rs[i][:, 0:1], (t, LANES))
            carry_ref[i] = carry
            carry_min = carry if i == 0 else jnp.minimum(carry_min, carry)
        return jnp.min(carry_min)

    def cond(c):
        it, carry_min = c
        return (it < qi) & (carry_min < SB_EXIT_LOG2)

    def body(c):
        it, _ = c
        return it + 1, tile_step(qi - 1 - it, False)

    lax.while_loop(cond, body, (jnp.int32(0), tile_step(qi, True)))
    for p in range(SB_PAIRS):
        o_ref[:, p * LANES:(p + 1) * LANES] = (acc_ref[2 * p] + acc_ref[2 * p + 1]).astype(o_ref.dtype)


def _sb_call(p16, u_tri, batch, seq):
    n = batch * seq
    t = ATT_TILE
    nq = seq // t
    groups = BRANCH_W // (SB_PAIRS * LANES)
    w = SB_PAIRS * LANES
    return pl.pallas_call(
        _sb_kernel,
        grid=(batch * groups, nq),
        in_specs=[
            pl.BlockSpec((t, w), lambda g, i: ((g // groups) * nq + i, g % groups)),
            pl.BlockSpec((seq, w), lambda g, i: (g // groups, groups + g % groups)),
            pl.BlockSpec((seq, w), lambda g, i: (g // groups, 2 * groups + g % groups)),
            pl.BlockSpec((t, t), lambda g, i: (0, 0)),
        ],
        out_specs=pl.BlockSpec((t, w), lambda g, i: ((g // groups) * nq + i, g % groups)),
        out_shape=jax.ShapeDtypeStruct((n, BRANCH_W), BF16),
        scratch_shapes=[
            pltpu.VMEM((2 * SB_PAIRS, t, LANES), F32),
            pltpu.VMEM((2 * SB_PAIRS, t, LANES), F32),
        ],
        compiler_params=_cparams("parallel", "arbitrary"),
        name="stick_breaking_attention",
    )(p16, p16, p16, u_tri)


def _moba_prep_kernel(q_ref, k_ref, cos_ref, sin_ref, qg_ref, kg_ref, seg_ref, rot_ref,
                      qa_ref, ka_ref, km_ref):
    blk = pl.program_id(1)
    lane = lax.broadcasted_iota(jnp.int32, (1, LANES), 1)
    even = lane < HEAD_DIM
    cos = cos_ref[...]
    sin = sin_ref[...]
    seg = seg_ref[...]
    rot = rot_ref[...]

    def norm_rope(x, g):
        ss = _dot_2term_rhs(x * x, seg)
        xn = x * lax.rsqrt(ss * (1.0 / HEAD_DIM) + RMS_EPS) * g
        return xn * cos + _dot_2term_rhs(xn, rot) * sin

    onehot_even = jnp.where(lane == HEAD_DIM + blk, 1.0, 0.0)
    onehot_odd = jnp.where(lane == blk, 1.0, 0.0)

    km_rows = []
    for c in range(BRANCH_W // LANES):
        sl = slice(c * LANES, (c + 1) * LANES)
        qr = norm_rope(q_ref[:, sl], qg_ref[...]) * (HEAD_DIM ** -0.5 * LOG2_E)
        kr = norm_rope(k_ref[:, sl], kg_ref[...])
        qa_ref[0, 2 * c] = jnp.where(even, qr, 0.0).astype(BF16)
        qa_ref[0, 2 * c + 1] = jnp.where(even, 0.0, qr).astype(BF16)
        ka_ref[0, 2 * c] = jnp.where(even, kr, onehot_even).astype(BF16)
        ka_ref[0, 2 * c + 1] = jnp.where(even, onehot_odd, kr).astype(BF16)
        kmean = jnp.mean(kr, axis=0, keepdims=True)
        km_rows.append(jnp.where(even, kmean, 0.0))
        km_rows.append(jnp.where(even, 0.0, kmean))
    km_ref[0, 0] = jnp.concatenate(km_rows, axis=0)


def _moba_prep_call(p32, cos_t, sin_t, qg, kg, layer, seg, rot, batch, seq):
    t = MB_BLOCK
    nb = seq // t
    h = N_ATT_HEADS
    return pl.pallas_call(
        _moba_prep_kernel,
        grid=(batch, nb),
        in_specs=[
            pl.BlockSpec((t, BRANCH_W), lambda b, i: (b * nb + i, 0)),
            pl.BlockSpec((t, BRANCH_W), lambda b, i: (b * nb + i, 1)),
            pl.BlockSpec((t, LANES), lambda b, i: (b * nb + i, 0)),
            pl.BlockSpec((t, LANES), lambda b, i: (b * nb + i, 0)),
            _layer_spec(layer, (1, LANES), lambda b, i: (0, 0)),
            _layer_spec(layer, (1, LANES), lambda b, i: (0, 0)),
            pl.BlockSpec((LANES, LANES), lambda b, i: (0, 0)),
            pl.BlockSpec((LANES, LANES), lambda b, i: (0, 0)),
        ],
        out_specs=[
            pl.BlockSpec((1, h, t, LANES), lambda b, i: (b, 0, i, 0)),
            pl.BlockSpec((1, h, t, LANES), lambda b, i: (b, 0, i, 0)),
            pl.BlockSpec((1, 1, h, LANES), lambda b, i: (b, i, 0, 0)),
        ],
        out_shape=[
            jax.ShapeDtypeStruct((batch, h, seq, LANES), BF16),
            jax.ShapeDtypeStruct((batch, h, seq, LANES), BF16),
            jax.ShapeDtypeStruct((batch, nb, h, LANES), F32),
        ],
        compiler_params=_cparams("parallel", "parallel"),
        name="moba_qk_prep",
    )(p32, p32, cos_t, sin_t, qg, kg, seg, rot)


def _moba_gate_kernel(qa_ref, km_ref, eye_ref, qb_ref, *, nb):
    t = ATT_TILE
    seq = qa_ref.shape[2]
    blk_row = lax.broadcasted_iota(jnp.int32, (nb, seq), 0)
    own = lax.broadcasted_iota(jnp.int32, (nb, seq), 1) // t
    for h in range(2):
        q = qa_ref[0, h]
        km_hi, km_mid, km_lo = _split3(km_ref[0, h])
        g = _dot_nt(km_hi, q) + _dot_nt(km_mid, q) + _dot_nt(km_lo, q)
        cnt = jnp.zeros((nb, seq), jnp.int32)
        for m in range(nb):
            gm = g[m:m + 1, :]
            beats = (gm > g) | ((gm == g) & (m < blk_row))
            cnt = cnt + jnp.where(beats & (m < own), 1, 0)
        keep = ((blk_row < own) & (cnt < MB_TOPK)) | (blk_row == own)
        bias = jnp.where(keep, 0.0, NEG_BIG)
        aux_lo = HEAD_DIM if h == 0 else 0
        pieces = []
        if aux_lo:
            pieces.append(jnp.zeros((aux_lo, seq), F32))
        pieces.append(bias)
        pieces.append(jnp.zeros((LANES - aux_lo - nb, seq), F32))
        bias_rows = jnp.concatenate(pieces, axis=0).astype(BF16)
        for j in range(seq // t):
            sl = slice(j * t, (j + 1) * t)
            bias_cols = _dot_nt(eye_ref[...], bias_rows[:, sl])
            qb_ref[0, h, sl, :] = (q[sl, :].astype(F32) + bias_cols).astype(BF16)


def _moba_gate_call(qa, km, eye, batch, seq):
    nb = seq // MB_BLOCK
    pairs = N_ATT_HEADS // 2
    spec = pl.BlockSpec((1, 2, seq, LANES), lambda b, p: (b, p, 0, 0))
    return pl.pallas_call(
        functools.partial(_moba_gate_kernel, nb=nb),
        grid=(batch, pairs),
        in_specs=[
            spec,
            pl.BlockSpec((1, 2, nb, LANES), lambda b, p: (b, p, 0, 0)),
            pl.BlockSpec((ATT_TILE, ATT_TILE), lambda b, p: (0, 0)),
        ],
        out_specs=spec,
        out_shape=jax.ShapeDtypeStruct(qa.shape, BF16),
        compiler_params=_cparams("parallel", "parallel"),
        name="moba_block_gate",
    )(qa, km, eye)


def _moba_kernel(qb_ref, ka_ref, v_ref, o_ref, s_scr, mx_ref, acc_ref):
    t = ATT_TILE
    own = pl.program_id(1)
    n_heads = 2 * MB_PAIRS
    lane = lax.broadcasted_iota(jnp.int32, (1, LANES), 1)
    head_mask = (lane < HEAD_DIM, lane >= HEAD_DIM)

    heads = range(n_heads)

    def lane_fold(x, op):
        parts = [x[:, c * LANES:(c + 1) * LANES] for c in range(t // LANES)]
        return functools.reduce(op, parts)

    def score_step(kj, diagonal):
        start = pl.multiple_of(kj * t, t)
        ss = [_dot_nt(qb_ref[0, h], ka_ref[0, h, pl.ds(start, t), :]) for h in heads]
        if diagonal:
            row = lax.broadcasted_iota(jnp.int32, (t, t), 0)
            col = lax.broadcasted_iota(jnp.int32, (t, t), 1)
            ss = [jnp.where(col <= row, s, NEG_BIG) for s in ss]
        for h in heads:
            s_scr[h, kj] = ss[h]
            part = lane_fold(ss[h], jnp.maximum)
            mx_ref[h] = part if diagonal else jnp.maximum(mx_ref[h], part)

    score_step(own, True)

    def score_body(it, c):
        score_step(own - 1 - it, False)
        return c

    lax.fori_loop(0, own, score_body, 0)

    for h in heads:
        mx_ref[h] = jnp.broadcast_to(jnp.max(mx_ref[h], axis=-1, keepdims=True), (t, LANES))
    acc_ref[...] = jnp.zeros_like(acc_ref)

    ones_lane = (HEAD_DIM, 0)
    ones_col = [jnp.where(lane == ones_lane[e], 1.0, 0.0).astype(BF16) for e in range(2)]

    def value_body(kj, c):
        start = pl.multiple_of(kj * t, t)
        ps = [jnp.exp2((s_scr[h, kj] - jnp.concatenate([mx_ref[h]] * (t // LANES), axis=1))
                       .astype(BF16)) for h in heads]
        for h in heads:
            v = v_ref[pl.ds(start, t), (h // 2) * LANES:(h // 2 + 1) * LANES]
            vh = jnp.where(head_mask[h % 2], v, ones_col[h % 2])
            acc_ref[h] += _dot(ps[h], vh)
        return c

    lax.fori_loop(0, own + 1, value_body, 0)

    for pr in range(MB_PAIRS):
        acc0 = acc_ref[2 * pr]
        acc1 = acc_ref[2 * pr + 1]
        inv0 = 1.0 / acc0[:, ones_lane[0]:ones_lane[0] + 1]
        inv1 = 1.0 / acc1[:, ones_lane[1]:ones_lane[1] + 1]
        out = jnp.where(head_mask[0], acc0 * inv0, acc1 * inv1)
        o_ref[:, pr * LANES:(pr + 1) * LANES] = out.astype(o_ref.dtype)


def _moba_call(qb, ka, p16, batch, seq):
    n = batch * seq
    t = ATT_TILE
    nq = seq // t
    nb = seq // MB_BLOCK
    groups = BRANCH_W // (MB_PAIRS * LANES)
    nh = 2 * MB_PAIRS
    w = MB_PAIRS * LANES
    v_col0 = 3 * BRANCH_W // w
    return pl.pallas_call(
        _moba_kernel,
        grid=(batch * groups, nq),
        in_specs=[
            pl.BlockSpec((1, nh, t, LANES), lambda g, i: (g // groups, g % groups, i, 0)),
            pl.BlockSpec((1, nh, seq, LANES), lambda g, i: (g // groups, g % groups, 0, 0)),
            pl.BlockSpec((seq, w), lambda g, i: (g // groups, v_col0 + g % groups)),
        ],
        out_specs=pl.BlockSpec((t, w), lambda g, i: ((g // groups) * nq + i, g % groups)),
        out_shape=jax.ShapeDtypeStruct((n, BRANCH_W), BF16),
        scratch_shapes=[
            pltpu.VMEM((nh, nb, t, t), F32),
            pltpu.VMEM((nh, t, LANES), F32),
            pltpu.VMEM((nh, t, LANES), F32),
        ],
        compiler_params=_cparams("parallel", "arbitrary"),
        name="moba_attention",
    )(qb, ka, p16)


def _mlstm_kernel(u_ref, v_ref, o_ref, ifc_ref, ift_ref, cw_ref, ltri_ref, utri_ref, y_ref,
                  xbuf, c_ref, n_ref, m_ref):
    L = ML_CHUNK
    W = BRANCH_W
    halo = SUBLANES

    @pl.when(pl.program_id(1) == 0)
    def _():
        xbuf[0:halo, :] = jnp.zeros((halo, 2 * W), F32)
        c_ref[...] = jnp.zeros_like(c_ref)
        n_ref[...] = jnp.zeros_like(n_ref)
        m_ref[...] = jnp.zeros_like(m_ref)

    xbuf[halo:, :] = u_ref[...]
    conv = jnp.zeros((L, 2 * W), F32)
    for j in range(CONV_W):
        off = halo - (CONV_W - 1) + j
        conv = conv + cw_ref[j:j + 1, :] * xbuf[off:off + L, :]
    xbuf[0:halo, :] = u_ref[L - halo:, :]
    qk = conv * _sigmoid(conv)

    ift = ift_ref[0]
    lf_rows = _log_sigmoid(ift)
    bcum_rows = _dot_exact_rhs(lf_rows, utri_ref[...])
    lf_cols = _log_sigmoid(ifc_ref[...])
    bcum_cols = _dot_exact_lhs(ltri_ref[...], lf_cols)

    row = lax.broadcasted_iota(jnp.int32, (L, L), 0)
    col = lax.broadcasted_iota(jnp.int32, (L, L), 1)
    causal = col <= row

    for h in range(ML_HEADS):
        sl = slice(h * ML_HDIM, (h + 1) * ML_HDIM)
        q = qk[:, sl].astype(BF16)
        k = (qk[:, W + h * ML_HDIM:W + (h + 1) * ML_HDIM] * (ML_HDIM ** -0.5)).astype(BF16)
        v = v_ref[:, sl]
        a_row = ift[h:h + 1, :] - bcum_rows[ML_HEADS + h:ML_HEADS + h + 1, :]
        bc = bcum_cols[:, ML_HEADS + h:ML_HEADS + h + 1]
        m_prev = m_ref[h, 0:1, 0:1]

        a_mat = jnp.where(causal, a_row, NEG_BIG)
        mu = jnp.maximum(jnp.max(a_mat, axis=-1, keepdims=True), m_prev)
        w_intra = jnp.exp(a_mat - mu)
        w_inter = jnp.exp(m_prev - mu)
        sc = _dot_nt(q, k) * w_intra
        c_prev = c_ref[h]
        n_prev = n_ref[h, 0:1, :]
        num = _dot(sc.astype(BF16), v) + w_inter * _dot_nt(q, c_prev.astype(BF16))
        qn = jnp.sum(q.astype(F32) * n_prev, axis=-1, keepdims=True)
        den = jnp.sum(sc, axis=-1, keepdims=True) + w_inter * qn
        m_t = bc + mu
        hs = num / jnp.maximum(jnp.abs(den), jnp.exp(-m_t))
        gate = _sigmoid(o_ref[:, sl].astype(F32))
        y_ref[:, sl] = (hs * gate).astype(y_ref.dtype)

        mu_last = mu[L - 1:L, :]
        m_new = bc[L - 1:L, :] + mu_last
        decay = jnp.exp(m_prev - mu_last)
        wk_row = jnp.exp(a_row - mu_last)
        vt = v.astype(F32).T
        c_ref[h] = decay * c_prev + _dot((vt * wk_row).astype(BF16), k)
        wk8 = jnp.broadcast_to(wk_row, (SUBLANES, L)).astype(BF16)
        n_ref[h] = decay * n_ref[h] + _dot(wk8, k)
        m_ref[h] = jnp.broadcast_to(m_new, (SUBLANES, LANES))


def _mlstm_call(p32, p16, ifc, ift, conv_w, layer, ltri, utri, batch, seq):
    n = batch * seq
    L = ML_CHUNK
    nc = seq // L
    W = BRANCH_W
    return pl.pallas_call(
        _mlstm_kernel,
        grid=(batch, nc),
        in_specs=[
            pl.BlockSpec((L, 2 * W), lambda b, i: (b * nc + i, 1)),
            pl.BlockSpec((L, W), lambda b, i: (b * nc + i, 4)),
            pl.BlockSpec((L, W), lambda b, i: (b * nc + i, 5)),
            pl.BlockSpec((L, LANES), lambda b, i: (b * nc + i, 0)),
            pl.BlockSpec((1, SUBLANES, L), lambda b, i: (b, 0, i)),
            _layer_spec(layer, (CONV_W, 2 * W), lambda b, i: (0, 0)),
            pl.BlockSpec((L, L), lambda b, i: (0, 0)),
            pl.BlockSpec((L, L), lambda b, i: (0, 0)),
        ],
        out_specs=pl.BlockSpec((L, W), lambda b, i: (b * nc + i, 0)),
        out_shape=jax.ShapeDtypeStruct((n, W), BF16),
        scratch_shapes=[
            pltpu.VMEM((L + SUBLANES, 2 * W), F32),
            pltpu.VMEM((ML_HEADS, ML_HDIM, ML_HDIM), F32),
            pltpu.VMEM((ML_HEADS, SUBLANES, ML_HDIM), F32),
            pltpu.VMEM((ML_HEADS, SUBLANES, LANES), F32),
        ],
        compiler_params=_cparams("parallel", "arbitrary"),
        name="mlstm",
    )(p32, p16, p16, ifc, ift, conv_w, ltri, utri)


def _merge_kernel(ysb_ref, ymb_ref, yml_ref, gl_ref, x_ref, wb_ref, wo_ref, g2_ref, wr_ref, br_ref,
                  h_ref, xn_ref, route_ref, route_t_ref):
    merged = jnp.zeros((ROW_TILE, D_MODEL), F32)
    for b, y_ref in enumerate((ysb_ref, ymb_ref, yml_ref)):
        gate = _sigmoid(gl_ref[:, b * D_MODEL:(b + 1) * D_MODEL].astype(F32))
        merged = merged + gate * _dot(y_ref[...], wb_ref[b])
    hres = x_ref[...] + _dot(merged.astype(BF16), wo_ref[...])
    h_ref[...] = hres

    ms = jnp.mean(hres * hres, axis=-1, keepdims=True)
    xn = (hres * lax.rsqrt(ms + RMS_EPS) * g2_ref[...]).astype(BF16)
    xn_ref[...] = xn

    logits = _dot(xn, wr_ref[...]) + br_ref[...]
    lane = lax.broadcasted_iota(jnp.int32, (ROW_TILE, LANES), 1).astype(F32)
    far = float(LANES)
    is_g = lane < N_GROUPS
    gl = jnp.where(is_g, logits, NEG_BIG)
    gmax = jnp.max(gl, axis=-1, keepdims=True)
    gsum = jnp.sum(jnp.where(is_g, jnp.exp(gl - gmax), 0.0), axis=-1, keepdims=True)
    g_w = 1.0 / gsum
    g_idx = jnp.min(jnp.where(is_g & (gl == gmax), lane, far), axis=-1, keepdims=True)
    e_lo = N_GROUPS + g_idx * EXPERTS_PER_GROUP
    in_grp = (lane >= e_lo) & (lane < e_lo + EXPERTS_PER_GROUP)
    el = jnp.where(in_grp, logits, NEG_BIG)
    l1 = jnp.max(el, axis=-1, keepdims=True)
    i1 = jnp.min(jnp.where(in_grp & (el == l1), lane, far), axis=-1, keepdims=True)
    el2 = jnp.where(lane == i1, NEG_BIG, el)
    l2 = jnp.max(el2, axis=-1, keepdims=True)
    i2 = jnp.min(jnp.where(in_grp & (el2 == l2), lane, far), axis=-1, keepdims=True)
    p2 = jnp.exp(l2 - l1)
    w1 = g_w / (1.0 + p2)
    w2 = g_w * p2 / (1.0 + p2)
    slot = lane + e_lo
    cw = jnp.where(slot == i1, w1, 0.0) + jnp.where(slot == i2, w2, 0.0)
    cw = jnp.where(lane < EXPERTS_PER_GROUP, cw, 0.0)
    route = jnp.where(lane == EXPERTS_PER_GROUP, g_idx, cw)
    route_ref[...] = route
    route_t_ref[...] = route.T[0:2 * SUBLANES, :]


def _merge_call(ysb, ymb, yml, p16, x, wb, wo, g2, wr, br, layer):
    n = x.shape[0]
    row = lambda i: (i, 0)
    const2 = lambda i: (0, 0)
    return pl.pallas_call(
        _merge_kernel,
        grid=(n // ROW_TILE,),
        in_specs=[
            pl.BlockSpec((ROW_TILE, BRANCH_W), row),
            pl.BlockSpec((ROW_TILE, BRANCH_W), row),
            pl.BlockSpec((ROW_TILE, BRANCH_W), row),
            pl.BlockSpec((ROW_TILE, N_BRANCH * D_MODEL), lambda i: (i, 1)),
            pl.BlockSpec((ROW_TILE, D_MODEL), row),
            _layer_spec(layer, (N_BRANCH, BRANCH_W, D_MODEL), lambda i: (0, 0, 0)),
            _layer_spec(layer, (D_MODEL, D_MODEL), const2),
            _layer_spec(layer, (1, D_MODEL), const2),
            _layer_spec(layer, (D_MODEL, LANES), const2),
            _layer_spec(layer, (1, LANES), const2),
        ],
        out_specs=[
            pl.BlockSpec((ROW_TILE, D_MODEL), row),
            pl.BlockSpec((ROW_TILE, D_MODEL), row),
            pl.BlockSpec((ROW_TILE, LANES), row),
            pl.BlockSpec((2 * SUBLANES, ROW_TILE), lambda i: (0, i)),
        ],
        out_shape=[
            jax.ShapeDtypeStruct((n, D_MODEL), F32),
            jax.ShapeDtypeStruct((n, D_MODEL), BF16),
            jax.ShapeDtypeStruct((n, LANES), F32),
            jax.ShapeDtypeStruct((2 * SUBLANES, n), F32),
        ],
        compiler_params=_cparams("parallel"),
        name="merge_outproj_router",
    )(ysb, ymb, yml, p16, x, wb, wo, g2, wr, br)


def _moe_kernel(x_ref, h_ref, rt_ref, rtt_ref, lx_ref, w1_ref, w3_ref, w2_ref, y_ref,
                keyc_ref, keyr_ref):
    tb = MOE_BLOCK
    sub = MOE_SUB
    grp = pl.program_id(1).astype(F32)
    gid_lane = EXPERTS_PER_GROUP
    gid_col = rt_ref[:, gid_lane:gid_lane + 1]
    route_t = rtt_ref[...]
    gid_row = route_t[gid_lane:gid_lane + 1, :]

    @pl.when(pl.program_id(1) == 0)
    def _():
        y_ref[...] = h_ref[...]
        lx = lx_ref[...]
        lane = lax.broadcasted_iota(jnp.int32, (tb, LANES), 1).astype(F32)
        own_col = lane == gid_col
        before_col = _dot(lx, jnp.where(own_col, 1.0, 0.0).astype(BF16))
        pos_col = jnp.sum(jnp.where(own_col, before_col, 0.0), axis=-1, keepdims=True)
        keyc_ref[...] = jnp.broadcast_to(pos_col, (tb, LANES))
        row = lax.broadcasted_iota(jnp.int32, (SUBLANES, tb), 0).astype(F32)
        own_row = row == gid_row
        before_row = _dot_nt(jnp.where(own_row, 1.0, 0.0).astype(BF16), lx)
        pos_row = jnp.sum(jnp.where(own_row, before_row, 0.0), axis=0, keepdims=True)
        keyr_ref[...] = jnp.broadcast_to(pos_row, (SUBLANES, tb))

    in_col = gid_col == grp
    in_row = gid_row == grp
    key_col = jnp.where(in_col, keyc_ref[...], -1.0)
    key_col = jnp.concatenate([key_col] * (sub // LANES), axis=1)
    key_row = jnp.where(in_row, keyr_ref[0:1, :], -1.0)
    count = jnp.sum(jnp.where(in_row, 1.0, 0.0)).astype(jnp.int32)
    rem = count % sub
    short_tail = (rem > 0) & (rem <= sub // 2)
    n_sub = count // sub + jnp.where(rem > sub // 2, 1, 0)
    t_hi, t_mid, t_lo = _split3(route_t)

    def experts_of_sub_tile(base, rows):
        slot_rows = lax.broadcasted_iota(jnp.int32, (rows, tb), 0).astype(F32) + base
        pick = jnp.where(key_row == slot_rows, 1.0, 0.0).astype(BF16)
        xs = _dot(pick, x_ref[...]).astype(BF16)
        cw_t = _dot_nt(t_hi, pick) + _dot_nt(t_mid, pick) + _dot_nt(t_lo, pick)
        cw = jnp.concatenate(
            [cw_t, jnp.zeros((LANES - 2 * SUBLANES, rows), F32)], axis=0).T
        experts = range(EXPERTS_PER_GROUP)
        ups = [(_dot(xs, w1_ref[e]), _dot(xs, w3_ref[e])) for e in experts]
        hid = jnp.concatenate(
            [(a * _sigmoid(a) * b * cw[:, e:e + 1]).astype(BF16) for e, (a, b) in enumerate(ups)],
            axis=1)
        y = _dot(hid, w2_ref[...].reshape(EXPERTS_PER_GROUP * D_EXPERT, D_MODEL))
        return y.astype(BF16)

    def sub_tile(base, rows):
        y = experts_of_sub_tile(base, rows)
        slot_cols = lax.broadcasted_iota(jnp.int32, (tb, rows), 1).astype(F32) + base
        put = jnp.where(key_col[:, :rows] == slot_cols, 1.0, 0.0).astype(BF16)
        y_ref[...] += _dot(put, y)

    def full_tile(s, c):
        sub_tile((s * sub).astype(F32), sub)
        return c

    lax.fori_loop(0, n_sub, full_tile, 0)

    @pl.when(short_tail)
    def _():
        sub_tile((n_sub * sub).astype(F32), sub // 2)


def _moe_call(xn2, h, route, route_t, lx, w1b, w3b, w2b, layer):
    n = xn2.shape[0]
    tb = MOE_BLOCK
    e = EXPERTS_PER_GROUP
    return pl.pallas_call(
        _moe_kernel,
        grid=(n // tb, N_GROUPS),
        in_specs=[
            pl.BlockSpec((tb, D_MODEL), lambda i, g: (i, 0)),
            pl.BlockSpec((tb, D_MODEL), lambda i, g: (i, 0)),
            pl.BlockSpec((tb, LANES), lambda i, g: (i, 0)),
            pl.BlockSpec((2 * SUBLANES, tb), lambda i, g: (0, i)),
            pl.BlockSpec((tb, tb), lambda i, g: (0, 0)),
            _layer_spec(layer, (e, D_MODEL, D_EXPERT), lambda i, g: (g, 0, 0)),
            _layer_spec(layer, (e, D_MODEL, D_EXPERT), lambda i, g: (g, 0, 0)),
            _layer_spec(layer, (e, D_EXPERT, D_MODEL), lambda i, g: (g, 0, 0)),
        ],
        out_specs=pl.BlockSpec((tb, D_MODEL), lambda i, g: (i, 0)),
        out_shape=jax.ShapeDtypeStruct((n, D_MODEL), F32),
        scratch_shapes=[
            pltpu.VMEM((tb, LANES), F32),
            pltpu.VMEM((SUBLANES, tb), F32),
        ],
        compiler_params=_cparams("parallel", "arbitrary"),
        name="moe_group_experts",
    )(xn2, h, route, route_t, lx, w1b, w3b, w2b)


def _tables(positions):
    t = ATT_TILE
    r = jnp.arange(t)
    u_tri = (r[:, None] >= r[None, :]).astype(BF16)
    eye = jnp.eye(t, dtype=BF16)
    L = ML_CHUNK
    rl = jnp.arange(L)
    ltri = (rl[:, None] >= rl[None, :]).astype(BF16)
    utri = ltri.T

    li = jnp.arange(LANES)
    seg = ((li[:, None] // HEAD_DIM) == (li[None, :] // HEAD_DIM)).astype(BF16)
    half = ROPE_DIM // 2
    src, dst = li[:, None], li[None, :]
    d_in = dst % HEAD_DIM
    rot = jnp.where((d_in < half) & (src == dst + half), -1.0,
                    jnp.where((d_in >= half) & (d_in < ROPE_DIM) & (src == dst - half), 1.0, 0.0)
                    ).astype(BF16)

    inv_freq = jnp.power(jnp.float32(ROPE_THETA), -jnp.arange(half, dtype=F32) / half)
    lane_in = li % HEAD_DIM
    ang = positions.reshape(-1).astype(F32)[:, None] * inv_freq[lane_in % half][None, :]
    rotated = (lane_in < ROPE_DIM)[None, :]
    cos_t = jnp.where(rotated, jnp.cos(ang), 1.0)
    sin_t = jnp.where(rotated, jnp.sin(ang), 0.0)
    rb = jnp.arange(MOE_BLOCK)
    lx = (rb[None, :] < rb[:, None]).astype(BF16)
    return dict(u_tri=u_tri, eye=eye, ltri=ltri, utri=utri, seg=seg, rot=rot, cos=cos_t, sin=sin_t,
                lx=lx)


def kernel(x, positions, norm1_g, w_in, qn_g, kn_g, conv_w, ml_gate_b, w_branch, w_out, norm2_g,
           w_rg, b_rg, w_re, b_re, w1, w3, w2):
    batch, seq, d = x.shape
    n = batch * seq
    W = BRANCH_W
    depth = w_in.shape[0]
    tb = _tables(positions)
    xf = x.reshape(n, d)

    n_if = 2 * ML_HEADS
    n_route = N_GROUPS + N_EXPERTS
    w16 = jnp.concatenate(
        [w_in[:, :, 0:3 * W], w_in[:, :, 5 * W:6 * W], w_in[:, :, 8 * W:10 * W],
         w_in[:, :, 10 * W + n_if:]], axis=2).astype(BF16)
    w32 = jnp.concatenate([w_in[:, :, 3 * W:5 * W], w_in[:, :, 6 * W:8 * W]], axis=2).astype(BF16)
    wif = jnp.pad(w_in[:, :, 10 * W:10 * W + n_if],
                  ((0, 0), (0, 0), (0, LANES - n_if))).astype(BF16)
    bif = jnp.pad(ml_gate_b, ((0, 0), (0, LANES - n_if))).reshape(depth, 1, LANES)
    g1 = norm1_g.reshape(depth, 1, d)
    g2 = norm2_g.reshape(depth, 1, d)
    qg = jnp.tile(qn_g, (1, LANES // HEAD_DIM)).reshape(depth, 1, LANES)
    kg = jnp.tile(kn_g, (1, LANES // HEAD_DIM)).reshape(depth, 1, LANES)
    wr = jnp.pad(jnp.concatenate([w_rg, w_re], axis=2),
                 ((0, 0), (0, 0), (0, LANES - n_route))).astype(BF16)
    br = jnp.pad(jnp.concatenate([b_rg, b_re], axis=1),
                 ((0, 0), (0, LANES - n_route))).reshape(depth, 1, LANES)
    wb = w_branch.astype(BF16)
    wo = w_out.astype(BF16)
    w1b, w3b, w2b = w1.astype(BF16), w3.astype(BF16), w2.astype(BF16)

    for l in range(depth):
        xn, ifc = _norm_call(xf, g1, wif, bif, l)
        p16 = _matmul_call(xn, w16, l, BF16, "inproj_bf16")
        p32 = _matmul_call(xn, w32, l, F32, "inproj_f32")

        y_sb = _sb_call(p16, tb["u_tri"], batch, seq)

        qa, ka, km = _moba_prep_call(p32, tb["cos"], tb["sin"], qg, kg, l, tb["seg"], tb["rot"],
                                     batch, seq)
        qb = _moba_gate_call(qa, km.transpose(0, 2, 1, 3), tb["eye"], batch, seq)
        y_mb = _moba_call(qb, ka, p16, batch, seq)

        ift = ifc[:, :SUBLANES].reshape(batch, seq, SUBLANES).transpose(0, 2, 1)
        y_ml = _mlstm_call(p32, p16, ifc, ift, conv_w, l, tb["ltri"], tb["utri"], batch, seq)

        hres, xn2, route, route_t = _merge_call(y_sb, y_mb, y_ml, p16, xf, wb, wo, g2, wr, br, l)
        xf = _moe_call(xn2, hres, route, route_t, tb["lx"], w1b, w3b, w2b, l)

    return xf.reshape(batch, seq, d)
```

```python
import functools

import jax
import jax.numpy as jnp
from jax import lax
from jax.experimental import pallas as pl
from jax.experimental.pallas import tpu as pltpu

D_MODEL = 1024
BRANCH_W = D_MODEL // 2
N_BRANCH = 3
HEAD_DIM = 64
N_ATT_HEADS = BRANCH_W // HEAD_DIM
MB_BLOCK = 256
MB_TOPK = 3
ROPE_THETA = 500000.0
ROPE_DIM = HEAD_DIM // 4
ML_HEADS = 4
ML_HDIM = BRANCH_W // ML_HEADS
CONV_W = 4
N_GROUPS = 4
EXPERTS_PER_GROUP = 8
N_EXPERTS = N_GROUPS * EXPERTS_PER_GROUP
D_EXPERT = D_MODEL // 4
RMS_EPS = 1e-6

LANES = 128
SUBLANES = 8
VMEM_LIMIT_BYTES = 56 * 1024 * 1024

ROW_TILE = 512
COL_TILE = 1024
ATT_TILE = 256
ML_CHUNK = 256
MBQ_COL = 12 * BRANCH_W
MLQK_COL = 14 * BRANCH_W
MOE_BLOCK = 1024
MOE_SUB = 256

LOG2_E = 1.4426950408889634
SB_PAIRS = 4
SB_EXIT_LOG2 = 160.0
MB_PAIRS = 4
NEG_BIG = -(2.0 ** 100)

F32 = jnp.float32
BF16 = jnp.bfloat16


def _cparams(*sem):
    return pltpu.CompilerParams(dimension_semantics=sem, vmem_limit_bytes=VMEM_LIMIT_BYTES)


def _layer_spec(layer, block, index_map):
    return pl.BlockSpec((None,) + tuple(block), lambda *idx: (layer,) + tuple(index_map(*idx)))


def _dot(a, b):
    return jnp.dot(a, b, preferred_element_type=F32)


def _dot_nt(a, b):
    return lax.dot_general(a, b, (((1,), (1,)), ((), ())), preferred_element_type=F32)


def _split3(x):
    hi = x.astype(BF16)
    r1 = x - hi.astype(F32)
    mid = r1.astype(BF16)
    lo = (r1 - mid.astype(F32)).astype(BF16)
    return hi, mid, lo


def _dot_exact_rhs(x, m):
    hi, mid, lo = _split3(x)
    return _dot(hi, m) + _dot(mid, m) + _dot(lo, m)


def _dot_2term_rhs(x, m):
    hi = x.astype(BF16)
    lo = (x - hi.astype(F32)).astype(BF16)
    return _dot(hi, m) + _dot(lo, m)


def _dot_exact_lhs(m, x):
    hi, mid, lo = _split3(x)
    return _dot(m, hi) + _dot(m, mid) + _dot(m, lo)


def _log_sigmoid(x):
    return jnp.minimum(x, 0.0) - jnp.log(1.0 + jnp.exp(-jnp.abs(x)))


def _sigmoid(x):
    return 0.5 * jnp.tanh(0.5 * x) + 0.5


def _norm_kernel(x_ref, g_ref, wif_ref, bif_ref, xn_ref, if_ref):
    x = x_ref[...]
    ms = jnp.mean(x * x, axis=-1, keepdims=True)
    xn = (x * lax.rsqrt(ms + RMS_EPS) * g_ref[...]).astype(BF16)
    xn_ref[...] = xn
    if_ref[...] = _dot(xn, wif_ref[...]) + bif_ref[...]


def _norm_call(x, g, wif, bif, layer):
    n = x.shape[0]
    return pl.pallas_call(
        _norm_kernel,
        grid=(n // ROW_TILE,),
        in_specs=[
            pl.BlockSpec((ROW_TILE, D_MODEL), lambda i: (i, 0)),
            _layer_spec(layer, (1, D_MODEL), lambda i: (0, 0)),
            _layer_spec(layer, (D_MODEL, LANES), lambda i: (0, 0)),
            _layer_spec(layer, (1, LANES), lambda i: (0, 0)),
        ],
        out_specs=[
            pl.BlockSpec((ROW_TILE, D_MODEL), lambda i: (i, 0)),
            pl.BlockSpec((ROW_TILE, LANES), lambda i: (i, 0)),
        ],
        out_shape=[
            jax.ShapeDtypeStruct((n, D_MODEL), BF16),
            jax.ShapeDtypeStruct((n, LANES), F32),
        ],
        compiler_params=_cparams("parallel"),
        name="norm_gateproj",
    )(x, g, wif, bif)


def _matmul_kernel(a_ref, w_ref, o_ref):
    o_ref[...] = _dot(a_ref[...], w_ref[...]).astype(o_ref.dtype)


def _matmul_call(a, w, layer, out_dtype, name):
    n, k = a.shape
    c = w.shape[2]
    tm = 2 * ROW_TILE
    return pl.pallas_call(
        _matmul_kernel,
        grid=(n // tm, c // COL_TILE),
        in_specs=[
            pl.BlockSpec((tm, k), lambda i, j: (i, 0)),
            _layer_spec(layer, (k, COL_TILE), lambda i, j: (0, j)),
        ],
        out_specs=pl.BlockSpec((tm, COL_TILE), lambda i, j: (i, j)),
        out_shape=jax.ShapeDtypeStruct((n, c), out_dtype),
        compiler_params=_cparams("parallel", "parallel"),
        name=name,
    )(a, w)


def _sb_kernel(q_ref, k_ref, v_ref, u_ref, o_ref, acc_ref, carry_ref):
    t = ATT_TILE
    qi = pl.program_id(1)
    lane = lax.broadcasted_iota(jnp.int32, (1, LANES), 1)
    head_mask = (lane < HEAD_DIM, lane >= HEAD_DIM)
    n_heads = 2 * SB_PAIRS
    q = q_ref[...].astype(F32) * (HEAD_DIM ** -0.5 * LOG2_E)
    qh = []
    for p in range(SB_PAIRS):
        qp = q[:, p * LANES:(p + 1) * LANES]
        qh.extend(jnp.where(m, qp, 0.0).astype(BF16) for m in head_mask)
    u_tri = u_ref[...]

    acc_ref[...] = jnp.zeros_like(acc_ref)
    carry_ref[...] = jnp.zeros_like(carry_ref)

    def tile_step(kj, diagonal):
        start = pl.multiple_of(kj * t, t)
        if diagonal:
            row = lax.broadcasted_iota(jnp.int32, (t, t), 0)
            col = lax.broadcasted_iota(jnp.int32, (t, t), 1)
            past = col < row
        heads = range(n_heads)
        zs = [_dot_nt(qh[i], k_ref[pl.ds(start, t), (i // 2) * LANES:(i // 2 + 1) * LANES])
              for i in heads]
        sps = [jnp.maximum(z, 0.0) + jnp.log2(1.0 + jnp.exp2(-jnp.abs(z))) for z in zs]
        if diagonal:
            sps = [jnp.where(past, sp, 0.0) for sp in sps]
        rs = [_dot(sps[i].astype(BF16), u_tri)
              + jnp.concatenate([carry_ref[i]] * (t // LANES), axis=1) for i in heads]
        ws = [jnp.exp2((zs[i] - rs[i]).astype(BF16)) for i in heads]
        if diagonal:
            ws = [jnp.where(past, w, jnp.zeros_like(w)) for w in ws]
        carry_min = None
        for i in heads:
            v = v_ref[pl.ds(start, t), (i // 2) * LANES:(i // 2 + 1) * LANES]
            vh = jnp.where(head_mask[i % 2], v, jnp.zeros_like(v))
            acc_ref[i] += _dot(ws[i].astype(BF16), vh)
            carry = jnp.broadcast_to(rs[i][:, 0:1], (t, LANES))
            carry_ref[i] = carry
            carry_min = carry if i == 0 else jnp.minimum(carry_min, carry)
        return jnp.min(carry_min)

    def cond(c):
        it, carry_min = c
        return (it < qi) & (carry_min < SB_EXIT_LOG2)

    def body(c):
        it, _ = c
        return it + 1, tile_step(qi - 1 - it, False)

    lax.while_loop(cond, body, (jnp.int32(0), tile_step(qi, True)))
    for p in range(SB_PAIRS):
        o_ref[:, p * LANES:(p + 1) * LANES] = (acc_ref[2 * p] + acc_ref[2 * p + 1]).astype(o_ref.dtype)


def _sb_call(p16, u_tri, batch, seq):
    n = batch * seq
    t = ATT_TILE
    nq = seq // t
    groups = BRANCH_W // (SB_PAIRS * LANES)
    w = SB_PAIRS * LANES
    return pl.pallas_call(
        _sb_kernel,
        grid=(batch * groups, nq),
        in_specs=[
            pl.BlockSpec((t, w), lambda g, i: ((g // groups) * nq + i, g % groups)),
            pl.BlockSpec((seq, w), lambda g, i: (g // groups, groups + g % groups)),
            pl.BlockSpec((seq, w), lambda g, i: (g // groups, 2 * groups + g % groups)),
            pl.BlockSpec((t, t), lambda g, i: (0, 0)),
        ],
        out_specs=pl.BlockSpec((t, w), lambda g, i: ((g // groups) * nq + i, g % groups)),
        out_shape=jax.ShapeDtypeStruct((n, BRANCH_W), BF16),
        scratch_shapes=[
            pltpu.VMEM((2 * SB_PAIRS, t, LANES), F32),
            pltpu.VMEM((2 * SB_PAIRS, t, LANES), F32),
        ],
        compiler_params=_cparams("parallel", "arbitrary"),
        name="stick_breaking_attention",
    )(p16, p16, p16, u_tri)


def _moba_prep_kernel(q_ref, k_ref, cos_ref, sin_ref, qg_ref, kg_ref, seg_ref, rot_ref,
                      qa_ref, ka_ref, km_ref):
    blk = pl.program_id(1)
    lane = lax.broadcasted_iota(jnp.int32, (1, LANES), 1)
    even = lane < HEAD_DIM
    cos = cos_ref[...]
    sin = sin_ref[...]
    seg = seg_ref[...]
    rot = rot_ref[...]

    def norm_rope(x, g):
        ss = _dot_2term_rhs(x * x, seg)
        xn = x * lax.rsqrt(ss * (1.0 / HEAD_DIM) + RMS_EPS) * g
        return xn * cos + _dot_2term_rhs(xn, rot) * sin

    onehot_even = jnp.where(lane == HEAD_DIM + blk, 1.0, 0.0)
    onehot_odd = jnp.where(lane == blk, 1.0, 0.0)

    km_rows = []
    for c in range(BRANCH_W // LANES):
        sl = slice(c * LANES, (c + 1) * LANES)
        qr = norm_rope(q_ref[:, sl].astype(F32), qg_ref[...]) * (HEAD_DIM ** -0.5 * LOG2_E)
        kr = norm_rope(k_ref[:, sl].astype(F32), kg_ref[...])
        qa_ref[0, 2 * c] = jnp.where(even, qr, 0.0).astype(BF16)
        qa_ref[0, 2 * c + 1] = jnp.where(even, 0.0, qr).astype(BF16)
        ka_ref[0, 2 * c] = jnp.where(even, kr, onehot_even).astype(BF16)
        ka_ref[0, 2 * c + 1] = jnp.where(even, onehot_odd, kr).astype(BF16)
        kmean = jnp.mean(kr, axis=0, keepdims=True)
        km_rows.append(jnp.where(even, kmean, 0.0))
        km_rows.append(jnp.where(even, 0.0, kmean))
    km_ref[0, 0] = jnp.concatenate(km_rows, axis=0)


def _moba_prep_call(proj, cos_t, sin_t, qg, kg, layer, seg, rot, batch, seq):
    t = MB_BLOCK
    nb = seq // t
    h = N_ATT_HEADS
    return pl.pallas_call(
        _moba_prep_kernel,
        grid=(batch, nb),
        in_specs=[
            pl.BlockSpec((t, BRANCH_W), lambda b, i: (b * nb + i, MBQ_COL // BRANCH_W)),
            pl.BlockSpec((t, BRANCH_W), lambda b, i: (b * nb + i, MBQ_COL // BRANCH_W + 1)),
            pl.BlockSpec((t, LANES), lambda b, i: (b * nb + i, 0)),
            pl.BlockSpec((t, LANES), lambda b, i: (b * nb + i, 0)),
            _layer_spec(layer, (1, LANES), lambda b, i: (0, 0)),
            _layer_spec(layer, (1, LANES), lambda b, i: (0, 0)),
            pl.BlockSpec((LANES, LANES), lambda b, i: (0, 0)),
            pl.BlockSpec((LANES, LANES), lambda b, i: (0, 0)),
        ],
        out_specs=[
            pl.BlockSpec((1, h, t, LANES), lambda b, i: (b, 0, i, 0)),
            pl.BlockSpec((1, h, t, LANES), lambda b, i: (b, 0, i, 0)),
            pl.BlockSpec((1, 1, h, LANES), lambda b, i: (b, i, 0, 0)),
        ],
        out_shape=[
            jax.ShapeDtypeStruct((batch, h, seq, LANES), BF16),
            jax.ShapeDtypeStruct((batch, h, seq, LANES), BF16),
            jax.ShapeDtypeStruct((batch, nb, h, LANES), F32),
        ],
        compiler_params=_cparams("parallel", "parallel"),
        name="moba_qk_prep",
    )(proj, proj, cos_t, sin_t, qg, kg, seg, rot)


def _moba_gate_kernel(qa_ref, km_ref, eye_ref, qb_ref, *, nb):
    t = ATT_TILE
    seq = qa_ref.shape[2]
    blk_row = lax.broadcasted_iota(jnp.int32, (nb, seq), 0)
    own = lax.broadcasted_iota(jnp.int32, (nb, seq), 1) // t
    for h in range(2):
        q = qa_ref[0, h]
        km_hi, km_mid, km_lo = _split3(km_ref[0, h])
        g = _dot_nt(km_hi, q) + _dot_nt(km_mid, q) + _dot_nt(km_lo, q)
        cnt = jnp.zeros((nb, seq), jnp.int32)
        for m in range(nb):
            gm = g[m:m + 1, :]
            beats = (gm > g) | ((gm == g) & (m < blk_row))
            cnt = cnt + jnp.where(beats & (m < own), 1, 0)
        keep = ((blk_row < own) & (cnt < MB_TOPK)) | (blk_row == own)
        bias = jnp.where(keep, 0.0, NEG_BIG)
        aux_lo = HEAD_DIM if h == 0 else 0
        pieces = []
        if aux_lo:
            pieces.append(jnp.zeros((aux_lo, seq), F32))
        pieces.append(bias)
        pieces.append(jnp.zeros((LANES - aux_lo - nb, seq), F32))
        bias_rows = jnp.concatenate(pieces, axis=0).astype(BF16)
        for j in range(seq // t):
            sl = slice(j * t, (j + 1) * t)
            bias_cols = _dot_nt(eye_ref[...], bias_rows[:, sl])
            qb_ref[0, h, sl, :] = (q[sl, :].astype(F32) + bias_cols).astype(BF16)


def _moba_gate_call(qa, km, eye, batch, seq):
    nb = seq // MB_BLOCK
    pairs = N_ATT_HEADS // 2
    spec = pl.BlockSpec((1, 2, seq, LANES), lambda b, p: (b, p, 0, 0))
    return pl.pallas_call(
        functools.partial(_moba_gate_kernel, nb=nb),
        grid=(batch, pairs),
        in_specs=[
            spec,
            pl.BlockSpec((1, 2, nb, LANES), lambda b, p: (b, p, 0, 0)),
            pl.BlockSpec((ATT_TILE, ATT_TILE), lambda b, p: (0, 0)),
        ],
        out_specs=spec,
        out_shape=jax.ShapeDtypeStruct(qa.shape, BF16),
        compiler_params=_cparams("parallel", "parallel"),
        name="moba_block_gate",
    )(qa, km, eye)


def _moba_kernel(qb_ref, ka_ref, v_ref, o_ref, s_scr, mx_ref, acc_ref):
    t = ATT_TILE
    own = pl.program_id(1)
    n_heads = 2 * MB_PAIRS
    lane = lax.broadcasted_iota(jnp.int32, (1, LANES), 1)
    head_mask = (lane < HEAD_DIM, lane >= HEAD_DIM)

    heads = range(n_heads)

    def lane_fold(x, op):
        parts = [x[:, c * LANES:(c + 1) * LANES] for c in range(t // LANES)]
        return functools.reduce(op, parts)

    def score_step(kj, diagonal):
        start = pl.multiple_of(kj * t, t)
        ss = [_dot_nt(qb_ref[0, h], ka_ref[0, h, pl.ds(start, t), :]) for h in heads]
        if diagonal:
            row = lax.broadcasted_iota(jnp.int32, (t, t), 0)
            col = lax.broadcasted_iota(jnp.int32, (t, t), 1)
            ss = [jnp.where(col <= row, s, NEG_BIG) for s in ss]
        for h in heads:
            s_scr[h, kj] = ss[h]
            part = lane_fold(ss[h], jnp.maximum)
            mx_ref[h] = part if diagonal else jnp.maximum(mx_ref[h], part)

    score_step(own, True)

    def score_body(it, c):
        score_step(2 * it, False)
        score_step(2 * it + 1, False)
        return c

    lax.fori_loop(0, own // 2, score_body, 0)

    @pl.when(own % 2 == 1)
    def _():
        score_step(own - 1, False)

    for h in heads:
        mx_ref[h] = jnp.broadcast_to(jnp.max(mx_ref[h], axis=-1, keepdims=True), (t, LANES))
    acc_ref[...] = jnp.zeros_like(acc_ref)

    ones_lane = (HEAD_DIM, 0)
    ones_col = [jnp.where(lane == ones_lane[e], 1.0, 0.0).astype(BF16) for e in range(2)]

    def value_step(kj):
        start = pl.multiple_of(kj * t, t)
        ps = [jnp.exp2((s_scr[h, kj] - jnp.concatenate([mx_ref[h]] * (t // LANES), axis=1))
                       .astype(BF16)) for h in heads]
        for h in heads:
            v = v_ref[pl.ds(start, t), (h // 2) * LANES:(h // 2 + 1) * LANES]
            vh = jnp.where(head_mask[h % 2], v, ones_col[h % 2])
            acc_ref[h] += _dot(ps[h], vh)

    def value_body(it, c):
        value_step(2 * it)
        value_step(2 * it + 1)
        return c

    lax.fori_loop(0, (own + 1) // 2, value_body, 0)

    @pl.when(own % 2 == 0)
    def _():
        value_step(own)

    for pr in range(MB_PAIRS):
        acc0 = acc_ref[2 * pr]
        acc1 = acc_ref[2 * pr + 1]
        inv0 = 1.0 / acc0[:, ones_lane[0]:ones_lane[0] + 1]
        inv1 = 1.0 / acc1[:, ones_lane[1]:ones_lane[1] + 1]
        out = jnp.where(head_mask[0], acc0 * inv0, acc1 * inv1)
        o_ref[:, pr * LANES:(pr + 1) * LANES] = out.astype(o_ref.dtype)


def _moba_call(qb, ka, p16, batch, seq):
    n = batch * seq
    t = ATT_TILE
    nq = seq // t
    nb = seq // MB_BLOCK
    groups = BRANCH_W // (MB_PAIRS * LANES)
    nh = 2 * MB_PAIRS
    w = MB_PAIRS * LANES
    v_col0 = 3 * BRANCH_W // w
    return pl.pallas_call(
        _moba_kernel,
        grid=(batch * groups, nq),
        in_specs=[
            pl.BlockSpec((1, nh, t, LANES), lambda g, i: (g // groups, g % groups, i, 0)),
            pl.BlockSpec((1, nh, seq, LANES), lambda g, i: (g // groups, g % groups, 0, 0)),
            pl.BlockSpec((seq, w), lambda g, i: (g // groups, v_col0 + g % groups)),
        ],
        out_specs=pl.BlockSpec((t, w), lambda g, i: ((g // groups) * nq + i, g % groups)),
        out_shape=jax.ShapeDtypeStruct((n, BRANCH_W), BF16),
        scratch_shapes=[
            pltpu.VMEM((nh, nb, t, t), F32),
            pltpu.VMEM((nh, t, LANES), F32),
            pltpu.VMEM((nh, t, LANES), F32),
        ],
        compiler_params=_cparams("parallel", "arbitrary"),
        name="moba_attention",
    )(qb, ka, p16)


def _mlstm_kernel(u_ref, v_ref, o_ref, ifc_ref, ift_ref, cw_ref, ltri_ref, utri_ref, y_ref,
                  xbuf, c_ref, n_ref, m_ref):
    L = ML_CHUNK
    W = BRANCH_W
    halo = SUBLANES

    @pl.when(pl.program_id(1) == 0)
    def _():
        xbuf[0:halo, :] = jnp.zeros((halo, 2 * W), F32)
        c_ref[...] = jnp.zeros_like(c_ref)
        n_ref[...] = jnp.zeros_like(n_ref)
        m_ref[...] = jnp.zeros_like(m_ref)

    xbuf[halo:, :] = u_ref[...].astype(F32)
    conv = jnp.zeros((L, 2 * W), F32)
    for j in range(CONV_W):
        off = halo - (CONV_W - 1) + j
        conv = conv + cw_ref[j:j + 1, :] * xbuf[off:off + L, :]
    xbuf[0:halo, :] = xbuf[L:L + halo, :]
    qk = conv * _sigmoid(conv)

    ift = ift_ref[0]
    lf_rows = _log_sigmoid(ift)
    bcum_rows = _dot_exact_rhs(lf_rows, utri_ref[...])
    lf_cols = _log_sigmoid(ifc_ref[...])
    bcum_cols = _dot_exact_lhs(ltri_ref[...], lf_cols)

    row = lax.broadcasted_iota(jnp.int32, (L, L), 0)
    col = lax.broadcasted_iota(jnp.int32, (L, L), 1)
    causal = col <= row

    for h in range(ML_HEADS):
        sl = slice(h * ML_HDIM, (h + 1) * ML_HDIM)
        q = qk[:, sl].astype(BF16)
        k = (qk[:, W + h * ML_HDIM:W + (h + 1) * ML_HDIM] * (ML_HDIM ** -0.5)).astype(BF16)
        v = v_ref[:, sl]
        a_row = ift[h:h + 1, :] - bcum_rows[ML_HEADS + h:ML_HEADS + h + 1, :]
        bc = bcum_cols[:, ML_HEADS + h:ML_HEADS + h + 1]
        m_prev = m_ref[h, 0:1, 0:1]

        a_mat = jnp.where(causal, a_row, NEG_BIG)
        mu = jnp.maximum(jnp.max(a_mat, axis=-1, keepdims=True), m_prev)
        w_intra = jnp.exp(a_mat - mu)
        w_inter = jnp.exp(m_prev - mu)
        sc = _dot_nt(q, k) * w_intra
        c_prev = c_ref[h]
        n_prev = n_ref[h, 0:1, :]
        num = _dot(sc.astype(BF16), v) + w_inter * _dot_nt(q, c_prev.astype(BF16))
        qn = jnp.sum(q.astype(F32) * n_prev, axis=-1, keepdims=True)
        den = jnp.sum(sc, axis=-1, keepdims=True) + w_inter * qn
        m_t = bc + mu
        hs = num / jnp.maximum(jnp.abs(den), jnp.exp(-m_t))
        gate = _sigmoid(o_ref[:, sl].astype(F32))
        y_ref[:, sl] = (hs * gate).astype(y_ref.dtype)

        mu_last = mu[L - 1:L, :]
        m_new = bc[L - 1:L, :] + mu_last
        decay = jnp.exp(m_prev - mu_last)
        wk_row = jnp.exp(a_row - mu_last)
        vt = v.astype(F32).T
        c_ref[h] = decay * c_prev + _dot((vt * wk_row).astype(BF16), k)
        wk8 = jnp.broadcast_to(wk_row, (SUBLANES, L)).astype(BF16)
        n_ref[h] = decay * n_ref[h] + _dot(wk8, k)
        m_ref[h] = jnp.broadcast_to(m_new, (SUBLANES, LANES))


def _mlstm_call(proj, ifc, ift, conv_w, layer, ltri, utri, batch, seq):
    n = batch * seq
    L = ML_CHUNK
    nc = seq // L
    W = BRANCH_W
    return pl.pallas_call(
        _mlstm_kernel,
        grid=(batch, nc),
        in_specs=[
            pl.BlockSpec((L, 2 * W), lambda b, i: (b * nc + i, MLQK_COL // (2 * W))),
            pl.BlockSpec((L, W), lambda b, i: (b * nc + i, 4)),
            pl.BlockSpec((L, W), lambda b, i: (b * nc + i, 5)),
            pl.BlockSpec((L, LANES), lambda b, i: (b * nc + i, 0)),
            pl.BlockSpec((1, SUBLANES, L), lambda b, i: (b, 0, i)),
            _layer_spec(layer, (CONV_W, 2 * W), lambda b, i: (0, 0)),
            pl.BlockSpec((L, L), lambda b, i: (0, 0)),
            pl.BlockSpec((L, L), lambda b, i: (0, 0)),
        ],
        out_specs=pl.BlockSpec((L, W), lambda b, i: (b * nc + i, 0)),
        out_shape=jax.ShapeDtypeStruct((n, W), BF16),
        scratch_shapes=[
            pltpu.VMEM((L + SUBLANES, 2 * W), F32),
            pltpu.VMEM((ML_HEADS, ML_HDIM, ML_HDIM), F32),
            pltpu.VMEM((ML_HEADS, SUBLANES, ML_HDIM), F32),
            pltpu.VMEM((ML_HEADS, SUBLANES, LANES), F32),
        ],
        compiler_params=_cparams("parallel", "arbitrary"),
        name="mlstm",
    )(proj, proj, proj, ifc, ift, conv_w, ltri, utri)


def _merge_kernel(ysb_ref, ymb_ref, yml_ref, gl_ref, x_ref, wb_ref, wo_ref, g2_ref, wr_ref, br_ref,
                  h_ref, xn_ref, route_ref, route_t_ref):
    merged = jnp.zeros((ROW_TILE, D_MODEL), F32)
    for b, y_ref in enumerate((ysb_ref, ymb_ref, yml_ref)):
        gate = _sigmoid(gl_ref[:, b * D_MODEL:(b + 1) * D_MODEL].astype(F32))
        merged = merged + gate * _dot(y_ref[...], wb_ref[b])
    hres = x_ref[...] + _dot(merged.astype(BF16), wo_ref[...])
    h_ref[...] = hres

    ms = jnp.mean(hres * hres, axis=-1, keepdims=True)
    xn = (hres * lax.rsqrt(ms + RMS_EPS) * g2_ref[...]).astype(BF16)
    xn_ref[...] = xn

    logits = _dot(xn, wr_ref[...]) + br_ref[...]
    lane = lax.broadcasted_iota(jnp.int32, (ROW_TILE, LANES), 1).astype(F32)
    far = float(LANES)
    is_g = lane < N_GROUPS
    gl = jnp.where(is_g, logits, NEG_BIG)
    gmax = jnp.max(gl, axis=-1, keepdims=True)
    gsum = jnp.sum(jnp.where(is_g, jnp.exp(gl - gmax), 0.0), axis=-1, keepdims=True)
    g_w = 1.0 / gsum
    g_idx = jnp.min(jnp.where(is_g & (gl == gmax), lane, far), axis=-1, keepdims=True)
    e_lo = N_GROUPS + g_idx * EXPERTS_PER_GROUP
    in_grp = (lane >= e_lo) & (lane < e_lo + EXPERTS_PER_GROUP)
    el = jnp.where(in_grp, logits, NEG_BIG)
    l1 = jnp.max(el, axis=-1, keepdims=True)
    i1 = jnp.min(jnp.where(in_grp & (el == l1), lane, far), axis=-1, keepdims=True)
    el2 = jnp.where(lane == i1, NEG_BIG, el)
    l2 = jnp.max(el2, axis=-1, keepdims=True)
    i2 = jnp.min(jnp.where(in_grp & (el2 == l2), lane, far), axis=-1, keepdims=True)
    p2 = jnp.exp(l2 - l1)
    w1 = g_w / (1.0 + p2)
    w2 = g_w * p2 / (1.0 + p2)
    slot = lane + e_lo
    cw = jnp.where(slot == i1, w1, 0.0) + jnp.where(slot == i2, w2, 0.0)
    cw = jnp.where(lane < EXPERTS_PER_GROUP, cw, 0.0)
    route = jnp.where(lane == EXPERTS_PER_GROUP, g_idx, cw)
    route_ref[...] = route
    route_t_ref[...] = route.T[0:2 * SUBLANES, :]


def _merge_call(ysb, ymb, yml, p16, x, wb, wo, g2, wr, br, layer):
    n = x.shape[0]
    row = lambda i: (i, 0)
    const2 = lambda i: (0, 0)
    return pl.pallas_call(
        _merge_kernel,
        grid=(n // ROW_TILE,),
        in_specs=[
            pl.BlockSpec((ROW_TILE, BRANCH_W), row),
            pl.BlockSpec((ROW_TILE, BRANCH_W), row),
            pl.BlockSpec((ROW_TILE, BRANCH_W), row),
            pl.BlockSpec((ROW_TILE, N_BRANCH * D_MODEL), lambda i: (i, 1)),
            pl.BlockSpec((ROW_TILE, D_MODEL), row),
            _layer_spec(layer, (N_BRANCH, BRANCH_W, D_MODEL), lambda i: (0, 0, 0)),
            _layer_spec(layer, (D_MODEL, D_MODEL), const2),
            _layer_spec(layer, (1, D_MODEL), const2),
            _layer_spec(layer, (D_MODEL, LANES), const2),
            _layer_spec(layer, (1, LANES), const2),
        ],
        out_specs=[
            pl.BlockSpec((ROW_TILE, D_MODEL), row),
            pl.BlockSpec((ROW_TILE, D_MODEL), row),
            pl.BlockSpec((ROW_TILE, LANES), row),
            pl.BlockSpec((2 * SUBLANES, ROW_TILE), lambda i: (0, i)),
        ],
        out_shape=[
            jax.ShapeDtypeStruct((n, D_MODEL), F32),
            jax.ShapeDtypeStruct((n, D_MODEL), BF16),
            jax.ShapeDtypeStruct((n, LANES), F32),
            jax.ShapeDtypeStruct((2 * SUBLANES, n), F32),
        ],
        compiler_params=_cparams("parallel"),
        name="merge_outproj_router",
    )(ysb, ymb, yml, p16, x, wb, wo, g2, wr, br)


def _moe_kernel(x_ref, h_ref, rt_ref, rtt_ref, lx_ref, w1_ref, w3_ref, w2_ref, y_ref,
                keyc_ref, keyr_ref):
    tb = MOE_BLOCK
    sub = MOE_SUB
    grp = pl.program_id(1).astype(F32)
    gid_lane = EXPERTS_PER_GROUP
    gid_col = rt_ref[:, gid_lane:gid_lane + 1]
    route_t = rtt_ref[...]
    gid_row = route_t[gid_lane:gid_lane + 1, :]

    @pl.when(pl.program_id(1) == 0)
    def _():
        y_ref[...] = h_ref[...]
        lx = lx_ref[...]
        lane = lax.broadcasted_iota(jnp.int32, (tb, LANES), 1).astype(F32)
        own_col = lane == gid_col
        before_col = _dot(lx, jnp.where(own_col, 1.0, 0.0).astype(BF16))
        pos_col = jnp.sum(jnp.where(own_col, before_col, 0.0), axis=-1, keepdims=True)
        keyc_ref[...] = jnp.broadcast_to(pos_col, (tb, LANES))
        row = lax.broadcasted_iota(jnp.int32, (SUBLANES, tb), 0).astype(F32)
        own_row = row == gid_row
        before_row = _dot_nt(jnp.where(own_row, 1.0, 0.0).astype(BF16), lx)
        pos_row = jnp.sum(jnp.where(own_row, before_row, 0.0), axis=0, keepdims=True)
        keyr_ref[...] = jnp.broadcast_to(pos_row, (SUBLANES, tb))

    in_col = gid_col == grp
    in_row = gid_row == grp
    key_col = jnp.where(in_col, keyc_ref[...], -1.0)
    key_col = jnp.concatenate([key_col] * (sub // LANES), axis=1)
    key_row = jnp.where(in_row, keyr_ref[0:1, :], -1.0)
    count = jnp.sum(jnp.where(in_row, 1.0, 0.0)).astype(jnp.int32)
    rem = count % sub
    short_tail = (rem > 0) & (rem <= sub // 2)
    n_sub = count // sub + jnp.where(rem > sub // 2, 1, 0)
    t_hi, t_mid, t_lo = _split3(route_t)

    def experts_of_sub_tile(base, rows):
        slot_rows = lax.broadcasted_iota(jnp.int32, (rows, tb), 0).astype(F32) + base
        pick = jnp.where(key_row == slot_rows, 1.0, 0.0).astype(BF16)
        xs = _dot(pick, x_ref[...]).astype(BF16)
        cw_t = _dot_nt(t_hi, pick) + _dot_nt(t_mid, pick) + _dot_nt(t_lo, pick)
        cw = jnp.concatenate(
            [cw_t, jnp.zeros((LANES - 2 * SUBLANES, rows), F32)], axis=0).T
        experts = range(EXPERTS_PER_GROUP)
        ups = [(_dot(xs, w1_ref[e]), _dot(xs, w3_ref[e])) for e in experts]
        hid = jnp.concatenate(
            [(a * _sigmoid(a) * b * cw[:, e:e + 1]).astype(BF16) for e, (a, b) in enumerate(ups)],
            axis=1)
        y = _dot(hid, w2_ref[...].reshape(EXPERTS_PER_GROUP * D_EXPERT, D_MODEL))
        return y.astype(BF16)

    def sub_tile(base, rows):
        y = experts_of_sub_tile(base, rows)
        slot_cols = lax.broadcasted_iota(jnp.int32, (tb, rows), 1).astype(F32) + base
        put = jnp.where(key_col[:, :rows] == slot_cols, 1.0, 0.0).astype(BF16)
        y_ref[...] += _dot(put, y)

    def full_tile(s, c):
        sub_tile((s * sub).astype(F32), sub)
        return c

    lax.fori_loop(0, n_sub, full_tile, 0)

    @pl.when(short_tail)
    def _():
        sub_tile((n_sub * sub).astype(F32), sub // 2)


def _moe_call(xn2, h, route, route_t, lx, w1b, w3b, w2b, layer):
    n = xn2.shape[0]
    tb = MOE_BLOCK
    e = EXPERTS_PER_GROUP
    return pl.pallas_call(
        _moe_kernel,
        grid=(n // tb, N_GROUPS),
        in_specs=[
            pl.BlockSpec((tb, D_MODEL), lambda i, g: (i, 0)),
            pl.BlockSpec((tb, D_MODEL), lambda i, g: (i, 0)),
            pl.BlockSpec((tb, LANES), lambda i, g: (i, 0)),
            pl.BlockSpec((2 * SUBLANES, tb), lambda i, g: (0, i)),
            pl.BlockSpec((tb, tb), lambda i, g: (0, 0)),
            _layer_spec(layer, (e, D_MODEL, D_EXPERT), lambda i, g: (g, 0, 0)),
            _layer_spec(layer, (e, D_MODEL, D_EXPERT), lambda i, g: (g, 0, 0)),
            _layer_spec(layer, (e, D_EXPERT, D_MODEL), lambda i, g: (g, 0, 0)),
        ],
        out_specs=pl.BlockSpec((tb, D_MODEL), lambda i, g: (i, 0)),
        out_shape=jax.ShapeDtypeStruct((n, D_MODEL), F32),
        scratch_shapes=[
            pltpu.VMEM((tb, LANES), F32),
            pltpu.VMEM((SUBLANES, tb), F32),
        ],
        compiler_params=_cparams("parallel", "arbitrary"),
        name="moe_group_experts",
    )(xn2, h, route, route_t, lx, w1b, w3b, w2b)


def _tables(positions):
    t = ATT_TILE
    r = jnp.arange(t)
    u_tri = (r[:, None] >= r[None, :]).astype(BF16)
    eye = jnp.eye(t, dtype=BF16)
    L = ML_CHUNK
    rl = jnp.arange(L)
    ltri = (rl[:, None] >= rl[None, :]).astype(BF16)
    utri = ltri.T

    li = jnp.arange(LANES)
    seg = ((li[:, None] // HEAD_DIM) == (li[None, :] // HEAD_DIM)).astype(BF16)
    half = ROPE_DIM // 2
    src, dst = li[:, None], li[None, :]
    d_in = dst % HEAD_DIM
    rot = jnp.where((d_in < half) & (src == dst + half), -1.0,
                    jnp.where((d_in >= half) & (d_in < ROPE_DIM) & (src == dst - half), 1.0, 0.0)
                    ).astype(BF16)

    inv_freq = jnp.power(jnp.float32(ROPE_THETA), -jnp.arange(half, dtype=F32) / half)
    lane_in = li % HEAD_DIM
    ang = positions.reshape(-1).astype(F32)[:, None] * inv_freq[lane_in % half][None, :]
    rotated = (lane_in < ROPE_DIM)[None, :]
    cos_t = jnp.where(rotated, jnp.cos(ang), 1.0)
    sin_t = jnp.where(rotated, jnp.sin(ang), 0.0)
    rb = jnp.arange(MOE_BLOCK)
    lx = (rb[None, :] < rb[:, None]).astype(BF16)
    return dict(u_tri=u_tri, eye=eye, ltri=ltri, utri=utri, seg=seg, rot=rot, cos=cos_t, sin=sin_t,
                lx=lx)


def kernel(x, positions, norm1_g, w_in, qn_g, kn_g, conv_w, ml_gate_b, w_branch, w_out, norm2_g,
           w_rg, b_rg, w_re, b_re, w1, w3, w2):
    batch, seq, d = x.shape
    n = batch * seq
    W = BRANCH_W
    depth = w_in.shape[0]
    tb = _tables(positions)
    xf = x.reshape(n, d)

    n_if = 2 * ML_HEADS
    n_route = N_GROUPS + N_EXPERTS
    w_proj = jnp.concatenate(
        [w_in[:, :, 0:3 * W], w_in[:, :, 5 * W:6 * W], w_in[:, :, 8 * W:10 * W],
         w_in[:, :, 10 * W + n_if:], w_in[:, :, 3 * W:5 * W], w_in[:, :, 6 * W:8 * W]],
        axis=2).astype(BF16)
    wif = jnp.pad(w_in[:, :, 10 * W:10 * W + n_if],
                  ((0, 0), (0, 0), (0, LANES - n_if))).astype(BF16)
    bif = jnp.pad(ml_gate_b, ((0, 0), (0, LANES - n_if))).reshape(depth, 1, LANES)
    g1 = norm1_g.reshape(depth, 1, d)
    g2 = norm2_g.reshape(depth, 1, d)
    qg = jnp.tile(qn_g, (1, LANES // HEAD_DIM)).reshape(depth, 1, LANES)
    kg = jnp.tile(kn_g, (1, LANES // HEAD_DIM)).reshape(depth, 1, LANES)
    wr = jnp.pad(jnp.concatenate([w_rg, w_re], axis=2),
                 ((0, 0), (0, 0), (0, LANES - n_route))).astype(BF16)
    br = jnp.pad(jnp.concatenate([b_rg, b_re], axis=1),
                 ((0, 0), (0, LANES - n_route))).reshape(depth, 1, LANES)
    wb = w_branch.astype(BF16)
    wo = w_out.astype(BF16)
    w1b, w3b, w2b = w1.astype(BF16), w3.astype(BF16), w2.astype(BF16)

    for l in range(depth):
        xn, ifc = _norm_call(xf, g1, wif, bif, l)
        p16 = _matmul_call(xn, w_proj, l, BF16, "inproj")

        y_sb = _sb_call(p16, tb["u_tri"], batch, seq)

        qa, ka, km = _moba_prep_call(p16, tb["cos"], tb["sin"], qg, kg, l, tb["seg"], tb["rot"],
                                     batch, seq)
        qb = _moba_gate_call(qa, km.transpose(0, 2, 1, 3), tb["eye"], batch, seq)
        y_mb = _moba_call(qb, ka, p16, batch, seq)

        ift = ifc[:, :SUBLANES].reshape(batch, seq, SUBLANES).transpose(0, 2, 1)
        y_ml = _mlstm_call(p16, ifc, ift, conv_w, l, tb["ltri"], tb["utri"], batch, seq)

        hres, xn2, route, route_t = _merge_call(y_sb, y_mb, y_ml, p16, xf, wb, wo, g2, wr, br, l)
        xf = _moe_call(xn2, hres, route, route_t, tb["lx"], w1b, w3b, w2b, l)

    return xf.reshape(batch, seq, d)
```

```python
import functools

import jax
import jax.numpy as jnp
from jax import lax
from jax.experimental import pallas as pl
from jax.experimental.pallas import tpu as pltpu

D_MODEL = 1024
BRANCH_W = D_MODEL // 2
N_BRANCH = 3
HEAD_DIM = 64
N_ATT_HEADS = BRANCH_W // HEAD_DIM
MB_BLOCK = 256
MB_TOPK = 3
ROPE_THETA = 500000.0
ROPE_DIM = HEAD_DIM // 4
ML_HEADS = 4
ML_HDIM = BRANCH_W // ML_HEADS
CONV_W = 4
N_GROUPS = 4
EXPERTS_PER_GROUP = 8
N_EXPERTS = N_GROUPS * EXPERTS_PER_GROUP
D_EXPERT = D_MODEL // 4
RMS_EPS = 1e-6

LANES = 128
SUBLANES = 8
VMEM_LIMIT_BYTES = 56 * 1024 * 1024

ROW_TILE = 512
COL_TILE = 2048
ATT_TILE = 256
ML_CHUNK = 256
MBQ_COL = 12 * BRANCH_W
MLQK_COL = 14 * BRANCH_W
MOE_BLOCK = 1024
MOE_SUB = 256

LOG2_E = 1.4426950408889634
SB_PAIRS = 4
SB_EXIT_LOG2 = 160.0
MB_PAIRS = 4
NEG_BIG = -(2.0 ** 100)

F32 = jnp.float32
BF16 = jnp.bfloat16


def _cparams(*sem):
    return pltpu.CompilerParams(dimension_semantics=sem, vmem_limit_bytes=VMEM_LIMIT_BYTES)


def _layer_spec(layer, block, index_map):
    return pl.BlockSpec((None,) + tuple(block), lambda *idx: (layer,) + tuple(index_map(*idx)))


def _dot(a, b):
    return jnp.dot(a, b, preferred_element_type=F32)


def _dot_nt(a, b):
    return lax.dot_general(a, b, (((1,), (1,)), ((), ())), preferred_element_type=F32)


def _split3(x):
    hi = x.astype(BF16)
    r1 = x - hi.astype(F32)
    mid = r1.astype(BF16)
    lo = (r1 - mid.astype(F32)).astype(BF16)
    return hi, mid, lo


def _dot_exact_rhs(x, m):
    hi, mid, lo = _split3(x)
    return _dot(hi, m) + _dot(mid, m) + _dot(lo, m)


def _dot_2term_rhs(x, m):
    hi = x.astype(BF16)
    lo = (x - hi.astype(F32)).astype(BF16)
    return _dot(hi, m) + _dot(lo, m)


def _dot_exact_lhs(m, x):
    hi, mid, lo = _split3(x)
    return _dot(m, hi) + _dot(m, mid) + _dot(m, lo)


def _log_sigmoid(x):
    return jnp.minimum(x, 0.0) - jnp.log(1.0 + jnp.exp(-jnp.abs(x)))


def _sigmoid(x):
    return 0.5 * jnp.tanh(0.5 * x) + 0.5


def _norm_kernel(x_ref, g_ref, wif_ref, bif_ref, xn_ref, if_ref, ift_ref):
    x = x_ref[...]
    ms = jnp.mean(x * x, axis=-1, keepdims=True)
    xn = (x * lax.rsqrt(ms + RMS_EPS) * g_ref[...]).astype(BF16)
    xn_ref[...] = xn
    gates = _dot(xn, wif_ref[...]) + bif_ref[...]
    if_ref[...] = gates
    ift_ref[...] = gates.T[0:2 * SUBLANES, :]


def _norm_call(x, g, wif, bif, layer):
    n = x.shape[0]
    return pl.pallas_call(
        _norm_kernel,
        grid=(n // ROW_TILE,),
        in_specs=[
            pl.BlockSpec((ROW_TILE, D_MODEL), lambda i: (i, 0)),
            _layer_spec(layer, (1, D_MODEL), lambda i: (0, 0)),
            _layer_spec(layer, (D_MODEL, LANES), lambda i: (0, 0)),
            _layer_spec(layer, (1, LANES), lambda i: (0, 0)),
        ],
        out_specs=[
            pl.BlockSpec((ROW_TILE, D_MODEL), lambda i: (i, 0)),
            pl.BlockSpec((ROW_TILE, LANES), lambda i: (i, 0)),
            pl.BlockSpec((2 * SUBLANES, ROW_TILE), lambda i: (0, i)),
        ],
        out_shape=[
            jax.ShapeDtypeStruct((n, D_MODEL), BF16),
            jax.ShapeDtypeStruct((n, LANES), F32),
            jax.ShapeDtypeStruct((2 * SUBLANES, n), F32),
        ],
        compiler_params=_cparams("parallel"),
        name="norm_gateproj",
    )(x, g, wif, bif)


def _matmul_kernel(a_ref, w_ref, o_ref):
    o_ref[...] = _dot(a_ref[...], w_ref[...]).astype(o_ref.dtype)


def _matmul_call(a, w, layer, out_dtype, name):
    n, k = a.shape
    c = w.shape[2]
    tm = 2 * ROW_TILE
    return pl.pallas_call(
        _matmul_kernel,
        grid=(n // tm, c // COL_TILE),
        in_specs=[
            pl.BlockSpec((tm, k), lambda i, j: (i, 0)),
            _layer_spec(layer, (k, COL_TILE), lambda i, j: (0, j)),
        ],
        out_specs=pl.BlockSpec((tm, COL_TILE), lambda i, j: (i, j)),
        out_shape=jax.ShapeDtypeStruct((n, c), out_dtype),
        compiler_params=_cparams("parallel", "parallel"),
        name=name,
    )(a, w)


def _sb_kernel(q_ref, k_ref, v_ref, u_ref, o_ref, acc_ref, carry_ref):
    t = ATT_TILE
    qi = pl.program_id(1)
    lane = lax.broadcasted_iota(jnp.int32, (1, LANES), 1)
    head_mask = (lane < HEAD_DIM, lane >= HEAD_DIM)
    n_heads = 2 * SB_PAIRS
    q = q_ref[...].astype(F32) * (HEAD_DIM ** -0.5 * LOG2_E)
    qh = []
    for p in range(SB_PAIRS):
        qp = q[:, p * LANES:(p + 1) * LANES]
        qh.extend(jnp.where(m, qp, 0.0).astype(BF16) for m in head_mask)
    u_tri = u_ref[...]

    acc_ref[...] = jnp.zeros_like(acc_ref)
    carry_ref[...] = jnp.zeros_like(carry_ref)

    def tile_step(kj, diagonal):
        start = pl.multiple_of(kj * t, t)
        if diagonal:
            row = lax.broadcasted_iota(jnp.int32, (t, t), 0)
            col = lax.broadcasted_iota(jnp.int32, (t, t), 1)
            past = col < row
        heads = range(n_heads)
        zs = [_dot_nt(qh[i], k_ref[pl.ds(start, t), (i // 2) * LANES:(i // 2 + 1) * LANES])
              for i in heads]
        sps = [jnp.maximum(z, 0.0) + jnp.log2(1.0 + jnp.exp2(-jnp.abs(z))) for z in zs]
        if diagonal:
            sps = [jnp.where(past, sp, 0.0) for sp in sps]
        rs = [_dot(sps[i].astype(BF16), u_tri)
              + jnp.concatenate([carry_ref[i]] * (t // LANES), axis=1) for i in heads]
        ws = [jnp.exp2((zs[i] - rs[i]).astype(BF16)) for i in heads]
        if diagonal:
            ws = [jnp.where(past, w, jnp.zeros_like(w)) for w in ws]
        carry_min = None
        for i in heads:
            v = v_ref[pl.ds(start, t), (i // 2) * LANES:(i // 2 + 1) * LANES]
            vh = jnp.where(head_mask[i % 2], v, jnp.zeros_like(v))
            acc_ref[i] += _dot(ws[i].astype(BF16), vh)
            carry = jnp.broadcast_to(rs[i][:, 0:1], (t, LANES))
            carry_ref[i] = carry
            carry_min = carry if i == 0 else jnp.minimum(carry_min, carry)
        return jnp.min(carry_min)

    def cond(c):
        it, carry_min = c
        return (it < qi) & (carry_min < SB_EXIT_LOG2)

    def body(c):
        it, _ = c
        return it + 1, tile_step(qi - 1 - it, False)

    lax.while_loop(cond, body, (jnp.int32(0), tile_step(qi, True)))
    for p in range(SB_PAIRS):
        o_ref[:, p * LANES:(p + 1) * LANES] = (acc_ref[2 * p] + acc_ref[2 * p + 1]).astype(o_ref.dtype)


def _sb_call(p16, u_tri, batch, seq):
    n = batch * seq
    t = ATT_TILE
    nq = seq // t
    groups = BRANCH_W // (SB_PAIRS * LANES)
    w = SB_PAIRS * LANES
    return pl.pallas_call(
        _sb_kernel,
        grid=(batch * groups, nq),
        in_specs=[
            pl.BlockSpec((t, w), lambda g, i: ((g // groups) * nq + i, g % groups)),
            pl.BlockSpec((seq, w), lambda g, i: (g // groups, groups + g % groups)),
            pl.BlockSpec((seq, w), lambda g, i: (g // groups, 2 * groups + g % groups)),
            pl.BlockSpec((t, t), lambda g, i: (0, 0)),
        ],
        out_specs=pl.BlockSpec((t, w), lambda g, i: ((g // groups) * nq + i, g % groups)),
        out_shape=jax.ShapeDtypeStruct((n, BRANCH_W), BF16),
        scratch_shapes=[
            pltpu.VMEM((2 * SB_PAIRS, t, LANES), F32),
            pltpu.VMEM((2 * SB_PAIRS, t, LANES), F32),
        ],
        compiler_params=_cparams("parallel", "arbitrary"),
        name="stick_breaking_attention",
    )(p16, p16, p16, u_tri)


def _moba_prep_kernel(q_ref, k_ref, cos_ref, sin_ref, qg_ref, kg_ref, seg_ref, rot_ref,
                      qa_ref, ka_ref, km_ref):
    blk = pl.program_id(1)
    lane = lax.broadcasted_iota(jnp.int32, (1, LANES), 1)
    even = lane < HEAD_DIM
    cos = cos_ref[...]
    sin = sin_ref[...]
    seg = seg_ref[...]
    rot = rot_ref[...]

    def norm_rope(x, g):
        ss = _dot_2term_rhs(x * x, seg)
        xn = x * lax.rsqrt(ss * (1.0 / HEAD_DIM) + RMS_EPS) * g
        return xn * cos + _dot_2term_rhs(xn, rot) * sin

    onehot_even = jnp.where(lane == HEAD_DIM + blk, 1.0, 0.0)
    onehot_odd = jnp.where(lane == blk, 1.0, 0.0)

    km_rows = []
    for c in range(BRANCH_W // LANES):
        sl = slice(c * LANES, (c + 1) * LANES)
        qr = norm_rope(q_ref[:, sl].astype(F32), qg_ref[...]) * (HEAD_DIM ** -0.5 * LOG2_E)
        kr = norm_rope(k_ref[:, sl].astype(F32), kg_ref[...])
        qa_ref[0, 2 * c] = jnp.where(even, qr, 0.0).astype(BF16)
        qa_ref[0, 2 * c + 1] = jnp.where(even, 0.0, qr).astype(BF16)
        ka_ref[0, 2 * c] = jnp.where(even, kr, onehot_even).astype(BF16)
        ka_ref[0, 2 * c + 1] = jnp.where(even, onehot_odd, kr).astype(BF16)
        kmean = jnp.mean(kr, axis=0, keepdims=True)
        km_rows.append(jnp.where(even, kmean, 0.0))
        km_rows.append(jnp.where(even, 0.0, kmean))
    km_ref[0, 0] = jnp.concatenate(km_rows, axis=0)


def _moba_prep_call(proj, cos_t, sin_t, qg, kg, layer, seg, rot, batch, seq):
    t = MB_BLOCK
    nb = seq // t
    h = N_ATT_HEADS
    return pl.pallas_call(
        _moba_prep_kernel,
        grid=(batch, nb),
        in_specs=[
            pl.BlockSpec((t, BRANCH_W), lambda b, i: (b * nb + i, MBQ_COL // BRANCH_W)),
            pl.BlockSpec((t, BRANCH_W), lambda b, i: (b * nb + i, MBQ_COL // BRANCH_W + 1)),
            pl.BlockSpec((t, LANES), lambda b, i: (b * nb + i, 0)),
            pl.BlockSpec((t, LANES), lambda b, i: (b * nb + i, 0)),
            _layer_spec(layer, (1, LANES), lambda b, i: (0, 0)),
            _layer_spec(layer, (1, LANES), lambda b, i: (0, 0)),
            pl.BlockSpec((LANES, LANES), lambda b, i: (0, 0)),
            pl.BlockSpec((LANES, LANES), lambda b, i: (0, 0)),
        ],
        out_specs=[
            pl.BlockSpec((1, h, t, LANES), lambda b, i: (b, 0, i, 0)),
            pl.BlockSpec((1, h, t, LANES), lambda b, i: (b, 0, i, 0)),
            pl.BlockSpec((1, 1, h, LANES), lambda b, i: (b, i, 0, 0)),
        ],
        out_shape=[
            jax.ShapeDtypeStruct((batch, h, seq, LANES), BF16),
            jax.ShapeDtypeStruct((batch, h, seq, LANES), BF16),
            jax.ShapeDtypeStruct((batch, nb, h, LANES), F32),
        ],
        compiler_params=_cparams("parallel", "parallel"),
        name="moba_qk_prep",
    )(proj, proj, cos_t, sin_t, qg, kg, seg, rot)


def _moba_gate_kernel(qa_ref, km_ref, eye_ref, qb_ref, *, nb):
    t = ATT_TILE
    seq = qa_ref.shape[2]
    blk_row = lax.broadcasted_iota(jnp.int32, (nb, seq), 0)
    own = lax.broadcasted_iota(jnp.int32, (nb, seq), 1) // t
    for h in range(2):
        q = qa_ref[0, h]
        km_hi, km_mid, km_lo = _split3(km_ref[0, h])
        g = _dot_nt(km_hi, q) + _dot_nt(km_mid, q) + _dot_nt(km_lo, q)
        cnt = jnp.zeros((nb, seq), jnp.int32)
        for m in range(nb):
            gm = g[m:m + 1, :]
            beats = (gm > g) | ((gm == g) & (m < blk_row))
            cnt = cnt + jnp.where(beats & (m < own), 1, 0)
        keep = ((blk_row < own) & (cnt < MB_TOPK)) | (blk_row == own)
        bias = jnp.where(keep, 0.0, NEG_BIG)
        aux_lo = HEAD_DIM if h == 0 else 0
        pieces = []
        if aux_lo:
            pieces.append(jnp.zeros((aux_lo, seq), F32))
        pieces.append(bias)
        pieces.append(jnp.zeros((LANES - aux_lo - nb, seq), F32))
        bias_rows = jnp.concatenate(pieces, axis=0).astype(BF16)
        for j in range(seq // t):
            sl = slice(j * t, (j + 1) * t)
            bias_cols = _dot_nt(eye_ref[...], bias_rows[:, sl])
            qb_ref[0, h, sl, :] = (q[sl, :].astype(F32) + bias_cols).astype(BF16)


def _moba_gate_call(qa, km, eye, batch, seq):
    nb = seq // MB_BLOCK
    pairs = N_ATT_HEADS // 2
    spec = pl.BlockSpec((1, 2, seq, LANES), lambda b, p: (b, p, 0, 0))
    return pl.pallas_call(
        functools.partial(_moba_gate_kernel, nb=nb),
        grid=(batch, pairs),
        in_specs=[
            spec,
            pl.BlockSpec((1, 2, nb, LANES), lambda b, p: (b, p, 0, 0)),
            pl.BlockSpec((ATT_TILE, ATT_TILE), lambda b, p: (0, 0)),
        ],
        out_specs=spec,
        out_shape=jax.ShapeDtypeStruct(qa.shape, BF16),
        compiler_params=_cparams("parallel", "parallel"),
        name="moba_block_gate",
    )(qa, km, eye)


def _moba_kernel(qb_ref, ka_ref, v_ref, o_ref, s_scr, mx_ref, acc_ref):
    t = ATT_TILE
    own = pl.program_id(1)
    n_heads = 2 * MB_PAIRS
    lane = lax.broadcasted_iota(jnp.int32, (1, LANES), 1)
    head_mask = (lane < HEAD_DIM, lane >= HEAD_DIM)

    heads = range(n_heads)

    def lane_fold(x, op):
        parts = [x[:, c * LANES:(c + 1) * LANES] for c in range(t // LANES)]
        return functools.reduce(op, parts)

    def score_step(kj, diagonal):
        start = pl.multiple_of(kj * t, t)
        ss = [_dot_nt(qb_ref[0, h], ka_ref[0, h, pl.ds(start, t), :]) for h in heads]
        if diagonal:
            row = lax.broadcasted_iota(jnp.int32, (t, t), 0)
            col = lax.broadcasted_iota(jnp.int32, (t, t), 1)
            ss = [jnp.where(col <= row, s, NEG_BIG) for s in ss]
        for h in heads:
            s_scr[h, kj] = ss[h]
            part = lane_fold(ss[h], jnp.maximum)
            mx_ref[h] = part if diagonal else jnp.maximum(mx_ref[h], part)

    score_step(own, True)

    def score_body(it, c):
        score_step(2 * it, False)
        score_step(2 * it + 1, False)
        return c

    lax.fori_loop(0, own // 2, score_body, 0)

    @pl.when(own % 2 == 1)
    def _():
        score_step(own - 1, False)

    for h in heads:
        mx_ref[h] = jnp.broadcast_to(jnp.max(mx_ref[h], axis=-1, keepdims=True), (t, LANES))
    acc_ref[...] = jnp.zeros_like(acc_ref)

    ones_lane = (HEAD_DIM, 0)
    ones_col = [jnp.where(lane == ones_lane[e], 1.0, 0.0).astype(BF16) for e in range(2)]

    def value_step(kj):
        start = pl.multiple_of(kj * t, t)
        ps = [jnp.exp2((s_scr[h, kj] - jnp.concatenate([mx_ref[h]] * (t // LANES), axis=1))
                       .astype(BF16)) for h in heads]
        for h in heads:
            v = v_ref[pl.ds(start, t), (h // 2) * LANES:(h // 2 + 1) * LANES]
            vh = jnp.where(head_mask[h % 2], v, ones_col[h % 2])
            acc_ref[h] += _dot(ps[h], vh)

    def value_body(it, c):
        value_step(2 * it)
        value_step(2 * it + 1)
        return c

    lax.fori_loop(0, (own + 1) // 2, value_body, 0)

    @pl.when(own % 2 == 0)
    def _():
        value_step(own)

    for pr in range(MB_PAIRS):
        acc0 = acc_ref[2 * pr]
        acc1 = acc_ref[2 * pr + 1]
        inv0 = 1.0 / acc0[:, ones_lane[0]:ones_lane[0] + 1]
        inv1 = 1.0 / acc1[:, ones_lane[1]:ones_lane[1] + 1]
        out = jnp.where(head_mask[0], acc0 * inv0, acc1 * inv1)
        o_ref[:, pr * LANES:(pr + 1) * LANES] = out.astype(o_ref.dtype)


def _moba_call(qb, ka, p16, batch, seq):
    n = batch * seq
    t = ATT_TILE
    nq = seq // t
    nb = seq // MB_BLOCK
    groups = BRANCH_W // (MB_PAIRS * LANES)
    nh = 2 * MB_PAIRS
    w = MB_PAIRS * LANES
    v_col0 = 3 * BRANCH_W // w
    return pl.pallas_call(
        _moba_kernel,
        grid=(batch * groups, nq),
        in_specs=[
            pl.BlockSpec((1, nh, t, LANES), lambda g, i: (g // groups, g % groups, i, 0)),
            pl.BlockSpec((1, nh, seq, LANES), lambda g, i: (g // groups, g % groups, 0, 0)),
            pl.BlockSpec((seq, w), lambda g, i: (g // groups, v_col0 + g % groups)),
        ],
        out_specs=pl.BlockSpec((t, w), lambda g, i: ((g // groups) * nq + i, g % groups)),
        out_shape=jax.ShapeDtypeStruct((n, BRANCH_W), BF16),
        scratch_shapes=[
            pltpu.VMEM((nh, nb, t, t), F32),
            pltpu.VMEM((nh, t, LANES), F32),
            pltpu.VMEM((nh, t, LANES), F32),
        ],
        compiler_params=_cparams("parallel", "arbitrary"),
        name="moba_attention",
    )(qb, ka, p16)


def _mlstm_kernel(u_ref, v_ref, o_ref, ifc_ref, ift_ref, cw_ref, ltri_ref, utri_ref, y_ref,
                  xbuf, c_ref, n_ref, m_ref):
    L = ML_CHUNK
    W = BRANCH_W
    halo = SUBLANES

    @pl.when(pl.program_id(1) == 0)
    def _():
        xbuf[0:halo, :] = jnp.zeros((halo, 2 * W), F32)
        c_ref[...] = jnp.zeros_like(c_ref)
        n_ref[...] = jnp.zeros_like(n_ref)
        m_ref[...] = jnp.zeros_like(m_ref)

    xbuf[halo:, :] = u_ref[...].astype(F32)
    conv = jnp.zeros((L, 2 * W), F32)
    for j in range(CONV_W):
        off = halo - (CONV_W - 1) + j
        conv = conv + cw_ref[j:j + 1, :] * xbuf[off:off + L, :]
    xbuf[0:halo, :] = xbuf[L:L + halo, :]
    qk = conv * _sigmoid(conv)

    ift = ift_ref[0:SUBLANES, :]
    lf_rows = _log_sigmoid(ift)
    bcum_rows = _dot_exact_rhs(lf_rows, utri_ref[...])
    lf_cols = _log_sigmoid(ifc_ref[...])
    bcum_cols = _dot_exact_lhs(ltri_ref[...], lf_cols)

    row = lax.broadcasted_iota(jnp.int32, (L, L), 0)
    col = lax.broadcasted_iota(jnp.int32, (L, L), 1)
    causal = col <= row

    for h in range(ML_HEADS):
        sl = slice(h * ML_HDIM, (h + 1) * ML_HDIM)
        q = qk[:, sl].astype(BF16)
        k = (qk[:, W + h * ML_HDIM:W + (h + 1) * ML_HDIM] * (ML_HDIM ** -0.5)).astype(BF16)
        v = v_ref[:, sl]
        a_row = ift[h:h + 1, :] - bcum_rows[ML_HEADS + h:ML_HEADS + h + 1, :]
        bc = bcum_cols[:, ML_HEADS + h:ML_HEADS + h + 1]
        m_prev = m_ref[h, 0:1, 0:1]

        a_mat = jnp.where(causal, a_row, NEG_BIG)
        mu = jnp.maximum(jnp.max(a_mat, axis=-1, keepdims=True), m_prev)
        w_intra = jnp.exp(a_mat - mu)
        w_inter = jnp.exp(m_prev - mu)
        sc = _dot_nt(q, k) * w_intra
        c_prev = c_ref[h]
        n_prev = n_ref[h, 0:1, :]
        num = _dot(sc.astype(BF16), v) + w_inter * _dot_nt(q, c_prev.astype(BF16))
        qn = jnp.sum(q.astype(F32) * n_prev, axis=-1, keepdims=True)
        den = jnp.sum(sc, axis=-1, keepdims=True) + w_inter * qn
        m_t = bc + mu
        hs = num / jnp.maximum(jnp.abs(den), jnp.exp(-m_t))
        gate = _sigmoid(o_ref[:, sl].astype(F32))
        y_ref[:, sl] = (hs * gate).astype(y_ref.dtype)

        mu_last = mu[L - 1:L, :]
        m_new = bc[L - 1:L, :] + mu_last
        decay = jnp.exp(m_prev - mu_last)
        wk_row = jnp.exp(a_row - mu_last)
        vt = v.astype(F32).T
        c_ref[h] = decay * c_prev + _dot((vt * wk_row).astype(BF16), k)
        wk8 = jnp.broadcast_to(wk_row, (SUBLANES, L)).astype(BF16)
        n_ref[h] = decay * n_ref[h] + _dot(wk8, k)
        m_ref[h] = jnp.broadcast_to(m_new, (SUBLANES, LANES))


def _mlstm_call(proj, ifc, ift, conv_w, layer, ltri, utri, batch, seq):
    n = batch * seq
    L = ML_CHUNK
    nc = seq // L
    W = BRANCH_W
    return pl.pallas_call(
        _mlstm_kernel,
        grid=(batch, nc),
        in_specs=[
            pl.BlockSpec((L, 2 * W), lambda b, i: (b * nc + i, MLQK_COL // (2 * W))),
            pl.BlockSpec((L, W), lambda b, i: (b * nc + i, 4)),
            pl.BlockSpec((L, W), lambda b, i: (b * nc + i, 5)),
            pl.BlockSpec((L, LANES), lambda b, i: (b * nc + i, 0)),
            pl.BlockSpec((2 * SUBLANES, L), lambda b, i: (0, b * nc + i)),
            _layer_spec(layer, (CONV_W, 2 * W), lambda b, i: (0, 0)),
            pl.BlockSpec((L, L), lambda b, i: (0, 0)),
            pl.BlockSpec((L, L), lambda b, i: (0, 0)),
        ],
        out_specs=pl.BlockSpec((L, W), lambda b, i: (b * nc + i, 0)),
        out_shape=jax.ShapeDtypeStruct((n, W), BF16),
        scratch_shapes=[
            pltpu.VMEM((L + SUBLANES, 2 * W), F32),
            pltpu.VMEM((ML_HEADS, ML_HDIM, ML_HDIM), F32),
            pltpu.VMEM((ML_HEADS, SUBLANES, ML_HDIM), F32),
            pltpu.VMEM((ML_HEADS, SUBLANES, LANES), F32),
        ],
        compiler_params=_cparams("parallel", "arbitrary"),
        name="mlstm",
    )(proj, proj, proj, ifc, ift, conv_w, ltri, utri)


def _merge_kernel(ysb_ref, ymb_ref, yml_ref, gl_ref, x_ref, wb_ref, wo_ref, g2_ref, wr_ref, br_ref,
                  h_ref, xn_ref, route_ref, route_t_ref):
    merged = jnp.zeros((ROW_TILE, D_MODEL), F32)
    for b, y_ref in enumerate((ysb_ref, ymb_ref, yml_ref)):
        gate = _sigmoid(gl_ref[:, b * D_MODEL:(b + 1) * D_MODEL].astype(F32))
        merged = merged + gate * _dot(y_ref[...], wb_ref[b])
    hres = x_ref[...] + _dot(merged.astype(BF16), wo_ref[...])
    h_ref[...] = hres

    ms = jnp.mean(hres * hres, axis=-1, keepdims=True)
    xn = (hres * lax.rsqrt(ms + RMS_EPS) * g2_ref[...]).astype(BF16)
    xn_ref[...] = xn

    logits = _dot(xn, wr_ref[...]) + br_ref[...]
    lane = lax.broadcasted_iota(jnp.int32, (ROW_TILE, LANES), 1).astype(F32)
    far = float(LANES)
    is_g = lane < N_GROUPS
    gl = jnp.where(is_g, logits, NEG_BIG)
    gmax = jnp.max(gl, axis=-1, keepdims=True)
    gsum = jnp.sum(jnp.where(is_g, jnp.exp(gl - gmax), 0.0), axis=-1, keepdims=True)
    g_w = 1.0 / gsum
    g_idx = jnp.min(jnp.where(is_g & (gl == gmax), lane, far), axis=-1, keepdims=True)
    e_lo = N_GROUPS + g_idx * EXPERTS_PER_GROUP
    in_grp = (lane >= e_lo) & (lane < e_lo + EXPERTS_PER_GROUP)
    el = jnp.where(in_grp, logits, NEG_BIG)
    l1 = jnp.max(el, axis=-1, keepdims=True)
    i1 = jnp.min(jnp.where(in_grp & (el == l1), lane, far), axis=-1, keepdims=True)
    el2 = jnp.where(lane == i1, NEG_BIG, el)
    l2 = jnp.max(el2, axis=-1, keepdims=True)
    i2 = jnp.min(jnp.where(in_grp & (el2 == l2), lane, far), axis=-1, keepdims=True)
    p2 = jnp.exp(l2 - l1)
    w1 = g_w / (1.0 + p2)
    w2 = g_w * p2 / (1.0 + p2)
    slot = lane + e_lo
    cw = jnp.where(slot == i1, w1, 0.0) + jnp.where(slot == i2, w2, 0.0)
    cw = jnp.where(lane < EXPERTS_PER_GROUP, cw, 0.0)
    route = jnp.where(lane == EXPERTS_PER_GROUP, g_idx, cw)
    route_ref[...] = route
    route_t_ref[...] = route.T[0:2 * SUBLANES, :]


def _merge_call(ysb, ymb, yml, p16, x, wb, wo, g2, wr, br, layer):
    n = x.shape[0]
    row = lambda i: (i, 0)
    const2 = lambda i: (0, 0)
    return pl.pallas_call(
        _merge_kernel,
        grid=(n // ROW_TILE,),
        in_specs=[
            pl.BlockSpec((ROW_TILE, BRANCH_W), row),
            pl.BlockSpec((ROW_TILE, BRANCH_W), row),
            pl.BlockSpec((ROW_TILE, BRANCH_W), row),
            pl.BlockSpec((ROW_TILE, N_BRANCH * D_MODEL), lambda i: (i, 1)),
            pl.BlockSpec((ROW_TILE, D_MODEL), row),
            _layer_spec(layer, (N_BRANCH, BRANCH_W, D_MODEL), lambda i: (0, 0, 0)),
            _layer_spec(layer, (D_MODEL, D_MODEL), const2),
            _layer_spec(layer, (1, D_MODEL), const2),
            _layer_spec(layer, (D_MODEL, LANES), const2),
            _layer_spec(layer, (1, LANES), const2),
        ],
        out_specs=[
            pl.BlockSpec((ROW_TILE, D_MODEL), row),
            pl.BlockSpec((ROW_TILE, D_MODEL), row),
            pl.BlockSpec((ROW_TILE, LANES), row),
            pl.BlockSpec((2 * SUBLANES, ROW_TILE), lambda i: (0, i)),
        ],
        out_shape=[
            jax.ShapeDtypeStruct((n, D_MODEL), F32),
            jax.ShapeDtypeStruct((n, D_MODEL), BF16),
            jax.ShapeDtypeStruct((n, LANES), F32),
            jax.ShapeDtypeStruct((2 * SUBLANES, n), F32),
        ],
        compiler_params=_cparams("parallel"),
        name="merge_outproj_router",
    )(ysb, ymb, yml, p16, x, wb, wo, g2, wr, br)


def _moe_kernel(x_ref, h_ref, rt_ref, rtt_ref, lx_ref, w1_ref, w3_ref, w2_ref, y_ref,
                keyc_ref, keyr_ref):
    tb = MOE_BLOCK
    sub = MOE_SUB
    grp = pl.program_id(1).astype(F32)
    gid_lane = EXPERTS_PER_GROUP
    gid_col = rt_ref[:, gid_lane:gid_lane + 1]
    route_t = rtt_ref[...]
    gid_row = route_t[gid_lane:gid_lane + 1, :]

    @pl.when(pl.program_id(1) == 0)
    def _():
        y_ref[...] = h_ref[...]
        lx = lx_ref[...]
        lane = lax.broadcasted_iota(jnp.int32, (tb, LANES), 1).astype(F32)
        own_col = lane == gid_col
        before_col = _dot(lx, jnp.where(own_col, 1.0, 0.0).astype(BF16))
        pos_col = jnp.sum(jnp.where(own_col, before_col, 0.0), axis=-1, keepdims=True)
        keyc_ref[...] = jnp.broadcast_to(pos_col, (tb, LANES))
        row = lax.broadcasted_iota(jnp.int32, (SUBLANES, tb), 0).astype(F32)
        own_row = row == gid_row
        before_row = _dot_nt(jnp.where(own_row, 1.0, 0.0).astype(BF16), lx)
        pos_row = jnp.sum(jnp.where(own_row, before_row, 0.0), axis=0, keepdims=True)
        keyr_ref[...] = jnp.broadcast_to(pos_row, (SUBLANES, tb))

    in_col = gid_col == grp
    in_row = gid_row == grp
    key_col = jnp.where(in_col, keyc_ref[...], -1.0)
    key_col = jnp.concatenate([key_col] * (sub // LANES), axis=1)
    key_row = jnp.where(in_row, keyr_ref[0:1, :], -1.0)
    count = jnp.sum(jnp.where(in_row, 1.0, 0.0)).astype(jnp.int32)
    rem = count % sub
    short_tail = (rem > 0) & (rem <= sub // 2)
    n_sub = count // sub + jnp.where(rem > sub // 2, 1, 0)
    t_hi, t_mid, t_lo = _split3(route_t)

    def experts_of_sub_tile(base, rows):
        slot_rows = lax.broadcasted_iota(jnp.int32, (rows, tb), 0).astype(F32) + base
        pick = jnp.where(key_row == slot_rows, 1.0, 0.0).astype(BF16)
        xs = _dot(pick, x_ref[...]).astype(BF16)
        cw_t = _dot_nt(t_hi, pick) + _dot_nt(t_mid, pick) + _dot_nt(t_lo, pick)
        cw = jnp.concatenate(
            [cw_t, jnp.zeros((LANES - 2 * SUBLANES, rows), F32)], axis=0).T
        experts = range(EXPERTS_PER_GROUP)
        ups = [(_dot(xs, w1_ref[e]), _dot(xs, w3_ref[e])) for e in experts]
        hid = jnp.concatenate(
            [(a * _sigmoid(a) * b * cw[:, e:e + 1]).astype(BF16) for e, (a, b) in enumerate(ups)],
            axis=1)
        y = _dot(hid, w2_ref[...].reshape(EXPERTS_PER_GROUP * D_EXPERT, D_MODEL))
        return y.astype(BF16)

    def sub_tile(base, rows):
        y = experts_of_sub_tile(base, rows)
        slot_cols = lax.broadcasted_iota(jnp.int32, (tb, rows), 1).astype(F32) + base
        put = jnp.where(key_col[:, :rows] == slot_cols, 1.0, 0.0).astype(BF16)
        y_ref[...] += _dot(put, y)

    def full_tile(s, c):
        sub_tile((s * sub).astype(F32), sub)
        return c

    lax.fori_loop(0, n_sub, full_tile, 0)

    @pl.when(short_tail)
    def _():
        sub_tile((n_sub * sub).astype(F32), sub // 2)


def _moe_call(xn2, h, route, route_t, lx, w1b, w3b, w2b, layer):
    n = xn2.shape[0]
    tb = MOE_BLOCK
    e = EXPERTS_PER_GROUP
    return pl.pallas_call(
        _moe_kernel,
        grid=(n // tb, N_GROUPS),
        in_specs=[
            pl.BlockSpec((tb, D_MODEL), lambda i, g: (i, 0)),
            pl.BlockSpec((tb, D_MODEL), lambda i, g: (i, 0)),
            pl.BlockSpec((tb, LANES), lambda i, g: (i, 0)),
            pl.BlockSpec((2 * SUBLANES, tb), lambda i, g: (0, i)),
            pl.BlockSpec((tb, tb), lambda i, g: (0, 0)),
            _layer_spec(layer, (e, D_MODEL, D_EXPERT), lambda i, g: (g, 0, 0)),
            _layer_spec(layer, (e, D_MODEL, D_EXPERT), lambda i, g: (g, 0, 0)),
            _layer_spec(layer, (e, D_EXPERT, D_MODEL), lambda i, g: (g, 0, 0)),
        ],
        out_specs=pl.BlockSpec((tb, D_MODEL), lambda i, g: (i, 0)),
        out_shape=jax.ShapeDtypeStruct((n, D_MODEL), F32),
        scratch_shapes=[
            pltpu.VMEM((tb, LANES), F32),
            pltpu.VMEM((SUBLANES, tb), F32),
        ],
        compiler_params=_cparams("parallel", "arbitrary"),
        name="moe_group_experts",
    )(xn2, h, route, route_t, lx, w1b, w3b, w2b)


def _tables(positions):
    t = ATT_TILE
    r = jnp.arange(t)
    u_tri = (r[:, None] >= r[None, :]).astype(BF16)
    eye = jnp.eye(t, dtype=BF16)
    L = ML_CHUNK
    rl = jnp.arange(L)
    ltri = (rl[:, None] >= rl[None, :]).astype(BF16)
    utri = ltri.T

    li = jnp.arange(LANES)
    seg = ((li[:, None] // HEAD_DIM) == (li[None, :] // HEAD_DIM)).astype(BF16)
    half = ROPE_DIM // 2
    src, dst = li[:, None], li[None, :]
    d_in = dst % HEAD_DIM
    rot = jnp.where((d_in < half) & (src == dst + half), -1.0,
                    jnp.where((d_in >= half) & (d_in < ROPE_DIM) & (src == dst - half), 1.0, 0.0)
                    ).astype(BF16)

    inv_freq = jnp.power(jnp.float32(ROPE_THETA), -jnp.arange(half, dtype=F32) / half)
    lane_in = li % HEAD_DIM
    ang = positions.reshape(-1).astype(F32)[:, None] * inv_freq[lane_in % half][None, :]
    rotated = (lane_in < ROPE_DIM)[None, :]
    cos_t = jnp.where(rotated, jnp.cos(ang), 1.0)
    sin_t = jnp.where(rotated, jnp.sin(ang), 0.0)
    rb = jnp.arange(MOE_BLOCK)
    lx = (rb[None, :] < rb[:, None]).astype(BF16)
    return dict(u_tri=u_tri, eye=eye, ltri=ltri, utri=utri, seg=seg, rot=rot, cos=cos_t, sin=sin_t,
                lx=lx)


def kernel(x, positions, norm1_g, w_in, qn_g, kn_g, conv_w, ml_gate_b, w_branch, w_out, norm2_g,
           w_rg, b_rg, w_re, b_re, w1, w3, w2):
    batch, seq, d = x.shape
    n = batch * seq
    W = BRANCH_W
    depth = w_in.shape[0]
    tb = _tables(positions)
    xf = x.reshape(n, d)

    n_if = 2 * ML_HEADS
    n_route = N_GROUPS + N_EXPERTS
    w_proj = jnp.concatenate(
        [w_in[:, :, 0:3 * W], w_in[:, :, 5 * W:6 * W], w_in[:, :, 8 * W:10 * W],
         w_in[:, :, 10 * W + n_if:], w_in[:, :, 3 * W:5 * W], w_in[:, :, 6 * W:8 * W]],
        axis=2).astype(BF16)
    wif = jnp.pad(w_in[:, :, 10 * W:10 * W + n_if],
                  ((0, 0), (0, 0), (0, LANES - n_if))).astype(BF16)
    bif = jnp.pad(ml_gate_b, ((0, 0), (0, LANES - n_if))).reshape(depth, 1, LANES)
    g1 = norm1_g.reshape(depth, 1, d)
    g2 = norm2_g.reshape(depth, 1, d)
    qg = jnp.tile(qn_g, (1, LANES // HEAD_DIM)).reshape(depth, 1, LANES)
    kg = jnp.tile(kn_g, (1, LANES // HEAD_DIM)).reshape(depth, 1, LANES)
    wr = jnp.pad(jnp.concatenate([w_rg, w_re], axis=2),
                 ((0, 0), (0, 0), (0, LANES - n_route))).astype(BF16)
    br = jnp.pad(jnp.concatenate([b_rg, b_re], axis=1),
                 ((0, 0), (0, LANES - n_route))).reshape(depth, 1, LANES)
    wb = w_branch.astype(BF16)
    wo = w_out.astype(BF16)
    w1b, w3b, w2b = w1.astype(BF16), w3.astype(BF16), w2.astype(BF16)

    for l in range(depth):
        xn, ifc, ift = _norm_call(xf, g1, wif, bif, l)
        p16 = _matmul_call(xn, w_proj, l, BF16, "inproj")

        y_sb = _sb_call(p16, tb["u_tri"], batch, seq)

        qa, ka, km = _moba_prep_call(p16, tb["cos"], tb["sin"], qg, kg, l, tb["seg"], tb["rot"],
                                     batch, seq)
        qb = _moba_gate_call(qa, km.transpose(0, 2, 1, 3), tb["eye"], batch, seq)
        y_mb = _moba_call(qb, ka, p16, batch, seq)

        y_ml = _mlstm_call(p16, ifc, ift, conv_w, l, tb["ltri"], tb["utri"], batch, seq)

        hres, xn2, route, route_t = _merge_call(y_sb, y_mb, y_ml, p16, xf, wb, wo, g2, wr, br, l)
        xf = _moe_call(xn2, hres, route, route_t, tb["lx"], w1b, w3b, w2b, l)

    return xf.reshape(batch, seq, d)
```

```python
import functools

import jax
import jax.numpy as jnp
from jax import lax
from jax.experimental import pallas as pl
from jax.experimental.pallas import tpu as pltpu

D_MODEL = 1024
BRANCH_W = D_MODEL // 2
N_BRANCH = 3
HEAD_DIM = 64
N_ATT_HEADS = BRANCH_W // HEAD_DIM
MB_BLOCK = 256
MB_TOPK = 3
ROPE_THETA = 500000.0
ROPE_DIM = HEAD_DIM // 4
ML_HEADS = 4
ML_HDIM = BRANCH_W // ML_HEADS
CONV_W = 4
N_GROUPS = 4
EXPERTS_PER_GROUP = 8
N_EXPERTS = N_GROUPS * EXPERTS_PER_GROUP
D_EXPERT = D_MODEL // 4
RMS_EPS = 1e-6

LANES = 128
SUBLANES = 8
VMEM_LIMIT_BYTES = 56 * 1024 * 1024

ROW_TILE = 512
COL_TILE = 2048
ATT_TILE = 256
ML_CHUNK = 256
MBQ_COL = 12 * BRANCH_W
MLQK_COL = 14 * BRANCH_W
MOE_BLOCK = 1024
MOE_SUB = 256

LOG2_E = 1.4426950408889634
SB_PAIRS = 4
SB_EXIT_LOG2 = 160.0
MB_PAIRS = 4
NEG_BIG = -(2.0 ** 100)

F32 = jnp.float32
BF16 = jnp.bfloat16


def _cparams(*sem):
    return pltpu.CompilerParams(dimension_semantics=sem, vmem_limit_bytes=VMEM_LIMIT_BYTES)


def _layer_spec(layer, block, index_map):
    return pl.BlockSpec((None,) + tuple(block), lambda *idx: (layer,) + tuple(index_map(*idx)))


def _dot(a, b):
    return jnp.dot(a, b, preferred_element_type=F32)


def _dot_nt(a, b):
    return lax.dot_general(a, b, (((1,), (1,)), ((), ())), preferred_element_type=F32)


def _split3(x):
    hi = x.astype(BF16)
    r1 = x - hi.astype(F32)
    mid = r1.astype(BF16)
    lo = (r1 - mid.astype(F32)).astype(BF16)
    return hi, mid, lo


def _dot_exact_rhs(x, m):
    hi, mid, lo = _split3(x)
    return _dot(hi, m) + _dot(mid, m) + _dot(lo, m)


def _dot_2term_rhs(x, m):
    hi = x.astype(BF16)
    lo = (x - hi.astype(F32)).astype(BF16)
    return _dot(hi, m) + _dot(lo, m)


def _dot_exact_lhs(m, x):
    hi, mid, lo = _split3(x)
    return _dot(m, hi) + _dot(m, mid) + _dot(m, lo)


def _log_sigmoid(x):
    return jnp.minimum(x, 0.0) - jnp.log(1.0 + jnp.exp(-jnp.abs(x)))


def _sigmoid(x):
    return 0.5 * jnp.tanh(0.5 * x) + 0.5


def _norm_kernel(x_ref, g_ref, wif_ref, bif_ref, xn_ref, if_ref, ift_ref):
    x = x_ref[...]
    ms = jnp.mean(x * x, axis=-1, keepdims=True)
    xn = (x * lax.rsqrt(ms + RMS_EPS) * g_ref[...]).astype(BF16)
    xn_ref[...] = xn
    gates = _dot(xn, wif_ref[...]) + bif_ref[...]
    if_ref[...] = gates
    ift_ref[...] = gates.T[0:2 * SUBLANES, :]


def _norm_proj_kernel(x_ref, g_ref, wif_ref, bif_ref, w_ref, o_ref, if_ref, ift_ref, xn_scr):
    @pl.when(pl.program_id(1) == 0)
    def _():
        _norm_kernel(x_ref, g_ref, wif_ref, bif_ref, xn_scr, if_ref, ift_ref)

    o_ref[...] = _dot(xn_scr[...], w_ref[...]).astype(o_ref.dtype)


def _norm_proj_call(x, g, wif, bif, w, layer):
    n, k = x.shape
    c = w.shape[2]
    tm = 2 * ROW_TILE
    return pl.pallas_call(
        _norm_proj_kernel,
        grid=(n // tm, c // COL_TILE),
        in_specs=[
            pl.BlockSpec((tm, k), lambda i, j: (i, 0)),
            _layer_spec(layer, (1, k), lambda i, j: (0, 0)),
            _layer_spec(layer, (k, LANES), lambda i, j: (0, 0)),
            _layer_spec(layer, (1, LANES), lambda i, j: (0, 0)),
            _layer_spec(layer, (k, COL_TILE), lambda i, j: (0, j)),
        ],
        out_specs=[
            pl.BlockSpec((tm, COL_TILE), lambda i, j: (i, j)),
            pl.BlockSpec((tm, LANES), lambda i, j: (i, 0)),
            pl.BlockSpec((2 * SUBLANES, tm), lambda i, j: (0, i)),
        ],
        out_shape=[
            jax.ShapeDtypeStruct((n, c), BF16),
            jax.ShapeDtypeStruct((n, LANES), F32),
            jax.ShapeDtypeStruct((2 * SUBLANES, n), F32),
        ],
        scratch_shapes=[pltpu.VMEM((tm, k), BF16)],
        compiler_params=_cparams("parallel", "arbitrary"),
        name="norm_inproj",
    )(x, g, wif, bif, w)


def _sb_kernel(q_ref, k_ref, v_ref, u_ref, o_ref, acc_ref, carry_ref):
    t = ATT_TILE
    qi = pl.program_id(1)
    lane = lax.broadcasted_iota(jnp.int32, (1, LANES), 1)
    head_mask = (lane < HEAD_DIM, lane >= HEAD_DIM)
    n_heads = 2 * SB_PAIRS
    q = q_ref[...].astype(F32) * (HEAD_DIM ** -0.5 * LOG2_E)
    qh = []
    for p in range(SB_PAIRS):
        qp = q[:, p * LANES:(p + 1) * LANES]
        qh.extend(jnp.where(m, qp, 0.0).astype(BF16) for m in head_mask)
    u_tri = u_ref[...]

    acc_ref[...] = jnp.zeros_like(acc_ref)
    carry_ref[...] = jnp.zeros_like(carry_ref)

    def tile_step(kj, diagonal):
        start = pl.multiple_of(kj * t, t)
        if diagonal:
            row = lax.broadcasted_iota(jnp.int32, (t, t), 0)
            col = lax.broadcasted_iota(jnp.int32, (t, t), 1)
            past = col < row
        heads = range(n_heads)
        zs = [_dot_nt(qh[i], k_ref[pl.ds(start, t), (i // 2) * LANES:(i // 2 + 1) * LANES])
              for i in heads]
        sps = [jnp.maximum(z, 0.0) + jnp.log2(1.0 + jnp.exp2(-jnp.abs(z))) for z in zs]
        if diagonal:
            sps = [jnp.where(past, sp, 0.0) for sp in sps]
        rs = [_dot(sps[i].astype(BF16), u_tri)
              + jnp.concatenate([carry_ref[i]] * (t // LANES), axis=1) for i in heads]
        ws = [jnp.exp2((zs[i] - rs[i]).astype(BF16)) for i in heads]
        if diagonal:
            ws = [jnp.where(past, w, jnp.zeros_like(w)) for w in ws]
        carry_min = None
        for i in heads:
            v = v_ref[pl.ds(start, t), (i // 2) * LANES:(i // 2 + 1) * LANES]
            vh = jnp.where(head_mask[i % 2], v, jnp.zeros_like(v))
            acc_ref[i] += _dot(ws[i].astype(BF16), vh)
            carry = jnp.broadcast_to(rs[i][:, 0:1], (t, LANES))
            carry_ref[i] = carry
            carry_min = carry if i == 0 else jnp.minimum(carry_min, carry)
        return jnp.min(carry_min)

    def cond(c):
        it, carry_min = c
        return (it < qi) & (carry_min < SB_EXIT_LOG2)

    def body(c):
        it, _ = c
        return it + 1, tile_step(qi - 1 - it, False)

    lax.while_loop(cond, body, (jnp.int32(0), tile_step(qi, True)))
    for p in range(SB_PAIRS):
        o_ref[:, p * LANES:(p + 1) * LANES] = (acc_ref[2 * p] + acc_ref[2 * p + 1]).astype(o_ref.dtype)


def _sb_call(p16, u_tri, batch, seq):
    n = batch * seq
    t = ATT_TILE
    nq = seq // t
    groups = BRANCH_W // (SB_PAIRS * LANES)
    w = SB_PAIRS * LANES
    return pl.pallas_call(
        _sb_kernel,
        grid=(batch * groups, nq),
        in_specs=[
            pl.BlockSpec((t, w), lambda g, i: ((g // groups) * nq + i, g % groups)),
            pl.BlockSpec((seq, w), lambda g, i: (g // groups, groups + g % groups)),
            pl.BlockSpec((seq, w), lambda g, i: (g // groups, 2 * groups + g % groups)),
            pl.BlockSpec((t, t), lambda g, i: (0, 0)),
        ],
        out_specs=pl.BlockSpec((t, w), lambda g, i: ((g // groups) * nq + i, g % groups)),
        out_shape=jax.ShapeDtypeStruct((n, BRANCH_W), BF16),
        scratch_shapes=[
            pltpu.VMEM((2 * SB_PAIRS, t, LANES), F32),
            pltpu.VMEM((2 * SB_PAIRS, t, LANES), F32),
        ],
        compiler_params=_cparams("parallel", "arbitrary"),
        name="stick_breaking_attention",
    )(p16, p16, p16, u_tri)


def _moba_prep_kernel(q_ref, k_ref, cos_ref, sin_ref, qg_ref, kg_ref, seg_ref, rot_ref,
                      qa_ref, ka_ref, km_ref):
    blk = pl.program_id(1)
    lane = lax.broadcasted_iota(jnp.int32, (1, LANES), 1)
    even = lane < HEAD_DIM
    cos = cos_ref[...]
    sin = sin_ref[...]
    seg = seg_ref[...]
    rot = rot_ref[...]

    def norm_rope(x, g):
        ss = _dot_2term_rhs(x * x, seg)
        xn = x * lax.rsqrt(ss * (1.0 / HEAD_DIM) + RMS_EPS) * g
        return xn * cos + _dot_2term_rhs(xn, rot) * sin

    onehot_even = jnp.where(lane == HEAD_DIM + blk, 1.0, 0.0)
    onehot_odd = jnp.where(lane == blk, 1.0, 0.0)

    km_rows = []
    for c in range(BRANCH_W // LANES):
        sl = slice(c * LANES, (c + 1) * LANES)
        qr = norm_rope(q_ref[:, sl].astype(F32), qg_ref[...]) * (HEAD_DIM ** -0.5 * LOG2_E)
        kr = norm_rope(k_ref[:, sl].astype(F32), kg_ref[...])
        qa_ref[0, 2 * c] = jnp.where(even, qr, 0.0).astype(BF16)
        qa_ref[0, 2 * c + 1] = jnp.where(even, 0.0, qr).astype(BF16)
        ka_ref[0, 2 * c] = jnp.where(even, kr, onehot_even).astype(BF16)
        ka_ref[0, 2 * c + 1] = jnp.where(even, onehot_odd, kr).astype(BF16)
        kmean = jnp.mean(kr, axis=0, keepdims=True)
        km_rows.append(jnp.where(even, kmean, 0.0))
        km_rows.append(jnp.where(even, 0.0, kmean))
    km_ref[0, 0] = jnp.concatenate(km_rows, axis=0)


def _moba_prep_call(proj, cos_t, sin_t, qg, kg, layer, seg, rot, batch, seq):
    t = MB_BLOCK
    nb = seq // t
    h = N_ATT_HEADS
    return pl.pallas_call(
        _moba_prep_kernel,
        grid=(batch, nb),
        in_specs=[
            pl.BlockSpec((t, BRANCH_W), lambda b, i: (b * nb + i, MBQ_COL // BRANCH_W)),
            pl.BlockSpec((t, BRANCH_W), lambda b, i: (b * nb + i, MBQ_COL // BRANCH_W + 1)),
            pl.BlockSpec((t, LANES), lambda b, i: (b * nb + i, 0)),
            pl.BlockSpec((t, LANES), lambda b, i: (b * nb + i, 0)),
            _layer_spec(layer, (1, LANES), lambda b, i: (0, 0)),
            _layer_spec(layer, (1, LANES), lambda b, i: (0, 0)),
            pl.BlockSpec((LANES, LANES), lambda b, i: (0, 0)),
            pl.BlockSpec((LANES, LANES), lambda b, i: (0, 0)),
        ],
        out_specs=[
            pl.BlockSpec((1, h, t, LANES), lambda b, i: (b, 0, i, 0)),
            pl.BlockSpec((1, h, t, LANES), lambda b, i: (b, 0, i, 0)),
            pl.BlockSpec((1, 1, h, LANES), lambda b, i: (b, i, 0, 0)),
        ],
        out_shape=[
            jax.ShapeDtypeStruct((batch, h, seq, LANES), BF16),
            jax.ShapeDtypeStruct((batch, h, seq, LANES), BF16),
            jax.ShapeDtypeStruct((batch, nb, h, LANES), F32),
        ],
        compiler_params=_cparams("parallel", "parallel"),
        name="moba_qk_prep",
    )(proj, proj, cos_t, sin_t, qg, kg, seg, rot)


def _moba_gate_kernel(qa_ref, km_ref, eye_ref, qb_ref, *, nb):
    t = ATT_TILE
    seq = qa_ref.shape[2]
    blk_row = lax.broadcasted_iota(jnp.int32, (nb, seq), 0)
    own = lax.broadcasted_iota(jnp.int32, (nb, seq), 1) // t
    for h in range(2):
        q = qa_ref[0, h]
        km_hi, km_mid, km_lo = _split3(km_ref[0, h])
        g = _dot_nt(km_hi, q) + _dot_nt(km_mid, q) + _dot_nt(km_lo, q)
        cnt = jnp.zeros((nb, seq), jnp.int32)
        for m in range(nb):
            gm = g[m:m + 1, :]
            beats = (gm > g) | ((gm == g) & (m < blk_row))
            cnt = cnt + jnp.where(beats & (m < own), 1, 0)
        keep = ((blk_row < own) & (cnt < MB_TOPK)) | (blk_row == own)
        bias = jnp.where(keep, 0.0, NEG_BIG)
        aux_lo = HEAD_DIM if h == 0 else 0
        pieces = []
        if aux_lo:
            pieces.append(jnp.zeros((aux_lo, seq), F32))
        pieces.append(bias)
        pieces.append(jnp.zeros((LANES - aux_lo - nb, seq), F32))
        bias_rows = jnp.concatenate(pieces, axis=0).astype(BF16)
        for j in range(seq // t):
            sl = slice(j * t, (j + 1) * t)
            bias_cols = _dot_nt(eye_ref[...], bias_rows[:, sl])
            qb_ref[0, h, sl, :] = (q[sl, :].astype(F32) + bias_cols).astype(BF16)


def _moba_gate_call(qa, km, eye, batch, seq):
    nb = seq // MB_BLOCK
    pairs = N_ATT_HEADS // 2
    spec = pl.BlockSpec((1, 2, seq, LANES), lambda b, p: (b, p, 0, 0))
    return pl.pallas_call(
        functools.partial(_moba_gate_kernel, nb=nb),
        grid=(batch, pairs),
        in_specs=[
            spec,
            pl.BlockSpec((1, 2, nb, LANES), lambda b, p: (b, p, 0, 0)),
            pl.BlockSpec((ATT_TILE, ATT_TILE), lambda b, p: (0, 0)),
        ],
        out_specs=spec,
        out_shape=jax.ShapeDtypeStruct(qa.shape, BF16),
        compiler_params=_cparams("parallel", "parallel"),
        name="moba_block_gate",
    )(qa, km, eye)


def _moba_kernel(qb_ref, ka_ref, v_ref, o_ref, s_scr, mx_ref, acc_ref):
    t = ATT_TILE
    own = pl.program_id(1)
    n_heads = 2 * MB_PAIRS
    lane = lax.broadcasted_iota(jnp.int32, (1, LANES), 1)
    head_mask = (lane < HEAD_DIM, lane >= HEAD_DIM)

    heads = range(n_heads)

    def lane_fold(x, op):
        parts = [x[:, c * LANES:(c + 1) * LANES] for c in range(t // LANES)]
        return functools.reduce(op, parts)

    def score_step(kj, diagonal):
        start = pl.multiple_of(kj * t, t)
        ss = [_dot_nt(qb_ref[0, h], ka_ref[0, h, pl.ds(start, t), :]) for h in heads]
        if diagonal:
            row = lax.broadcasted_iota(jnp.int32, (t, t), 0)
            col = lax.broadcasted_iota(jnp.int32, (t, t), 1)
            ss = [jnp.where(col <= row, s, NEG_BIG) for s in ss]
        for h in heads:
            s_scr[h, kj] = ss[h]
            part = lane_fold(ss[h], jnp.maximum)
            mx_ref[h] = part if diagonal else jnp.maximum(mx_ref[h], part)

    score_step(own, True)

    def score_body(it, c):
        score_step(2 * it, False)
        score_step(2 * it + 1, False)
        return c

    lax.fori_loop(0, own // 2, score_body, 0)

    @pl.when(own % 2 == 1)
    def _():
        score_step(own - 1, False)

    for h in heads:
        mx_ref[h] = jnp.broadcast_to(jnp.max(mx_ref[h], axis=-1, keepdims=True), (t, LANES))
    acc_ref[...] = jnp.zeros_like(acc_ref)

    ones_lane = (HEAD_DIM, 0)
    ones_col = [jnp.where(lane == ones_lane[e], 1.0, 0.0).astype(BF16) for e in range(2)]

    def value_step(kj):
        start = pl.multiple_of(kj * t, t)
        ps = [jnp.exp2((s_scr[h, kj] - jnp.concatenate([mx_ref[h]] * (t // LANES), axis=1))
                       .astype(BF16)) for h in heads]
        for h in heads:
            v = v_ref[pl.ds(start, t), (h // 2) * LANES:(h // 2 + 1) * LANES]
            vh = jnp.where(head_mask[h % 2], v, ones_col[h % 2])
            acc_ref[h] += _dot(ps[h], vh)

    def value_body(it, c):
        value_step(2 * it)
        value_step(2 * it + 1)
        return c

    lax.fori_loop(0, (own + 1) // 2, value_body, 0)

    @pl.when(own % 2 == 0)
    def _():
        value_step(own)

    for pr in range(MB_PAIRS):
        acc0 = acc_ref[2 * pr]
        acc1 = acc_ref[2 * pr + 1]
        inv0 = 1.0 / acc0[:, ones_lane[0]:ones_lane[0] + 1]
        inv1 = 1.0 / acc1[:, ones_lane[1]:ones_lane[1] + 1]
        out = jnp.where(head_mask[0], acc0 * inv0, acc1 * inv1)
        o_ref[:, pr * LANES:(pr + 1) * LANES] = out.astype(o_ref.dtype)


def _moba_call(qb, ka, p16, batch, seq):
    n = batch * seq
    t = ATT_TILE
    nq = seq // t
    nb = seq // MB_BLOCK
    groups = BRANCH_W // (MB_PAIRS * LANES)
    nh = 2 * MB_PAIRS
    w = MB_PAIRS * LANES
    v_col0 = 3 * BRANCH_W // w
    return pl.pallas_call(
        _moba_kernel,
        grid=(batch * groups, nq),
        in_specs=[
            pl.BlockSpec((1, nh, t, LANES), lambda g, i: (g // groups, g % groups, i, 0)),
            pl.BlockSpec((1, nh, seq, LANES), lambda g, i: (g // groups, g % groups, 0, 0)),
            pl.BlockSpec((seq, w), lambda g, i: (g // groups, v_col0 + g % groups)),
        ],
        out_specs=pl.BlockSpec((t, w), lambda g, i: ((g // groups) * nq + i, g % groups)),
        out_shape=jax.ShapeDtypeStruct((n, BRANCH_W), BF16),
        scratch_shapes=[
            pltpu.VMEM((nh, nb, t, t), F32),
            pltpu.VMEM((nh, t, LANES), F32),
            pltpu.VMEM((nh, t, LANES), F32),
        ],
        compiler_params=_cparams("parallel", "arbitrary"),
        name="moba_attention",
    )(qb, ka, p16)


def _mlstm_kernel(u_ref, v_ref, o_ref, ifc_ref, ift_ref, cw_ref, ltri_ref, utri_ref, y_ref,
                  xbuf, c_ref, n_ref, m_ref):
    L = ML_CHUNK
    W = BRANCH_W
    halo = SUBLANES

    @pl.when(pl.program_id(1) == 0)
    def _():
        xbuf[0:halo, :] = jnp.zeros((halo, 2 * W), F32)
        c_ref[...] = jnp.zeros_like(c_ref)
        n_ref[...] = jnp.zeros_like(n_ref)
        m_ref[...] = jnp.zeros_like(m_ref)

    xbuf[halo:, :] = u_ref[...].astype(F32)
    conv = jnp.zeros((L, 2 * W), F32)
    for j in range(CONV_W):
        off = halo - (CONV_W - 1) + j
        conv = conv + cw_ref[j:j + 1, :] * xbuf[off:off + L, :]
    xbuf[0:halo, :] = xbuf[L:L + halo, :]
    qk = conv * _sigmoid(conv)

    ift = ift_ref[0:SUBLANES, :]
    lf_rows = _log_sigmoid(ift)
    bcum_rows = _dot_exact_rhs(lf_rows, utri_ref[...])
    lf_cols = _log_sigmoid(ifc_ref[...])
    bcum_cols = _dot_exact_lhs(ltri_ref[...], lf_cols)

    row = lax.broadcasted_iota(jnp.int32, (L, L), 0)
    col = lax.broadcasted_iota(jnp.int32, (L, L), 1)
    causal = col <= row

    for h in range(ML_HEADS):
        sl = slice(h * ML_HDIM, (h + 1) * ML_HDIM)
        q = qk[:, sl].astype(BF16)
        k = (qk[:, W + h * ML_HDIM:W + (h + 1) * ML_HDIM] * (ML_HDIM ** -0.5)).astype(BF16)
        v = v_ref[:, sl]
        a_row = ift[h:h + 1, :] - bcum_rows[ML_HEADS + h:ML_HEADS + h + 1, :]
        bc = bcum_cols[:, ML_HEADS + h:ML_HEADS + h + 1]
        m_prev = m_ref[h, 0:1, 0:1]

        a_mat = jnp.where(causal, a_row, NEG_BIG)
        mu = jnp.maximum(jnp.max(a_mat, axis=-1, keepdims=True), m_prev)
        w_intra = jnp.exp(a_mat - mu)
        w_inter = jnp.exp(m_prev - mu)
        sc = _dot_nt(q, k) * w_intra
        c_prev = c_ref[h]
        n_prev = n_ref[h, 0:1, :]
        num = _dot(sc.astype(BF16), v) + w_inter * _dot_nt(q, c_prev.astype(BF16))
        qn = jnp.sum(q.astype(F32) * n_prev, axis=-1, keepdims=True)
        den = jnp.sum(sc, axis=-1, keepdims=True) + w_inter * qn
        m_t = bc + mu
        hs = num / jnp.maximum(jnp.abs(den), jnp.exp(-m_t))
        gate = _sigmoid(o_ref[:, sl].astype(F32))
        y_ref[:, sl] = (hs * gate).astype(y_ref.dtype)

        mu_last = mu[L - 1:L, :]
        m_new = bc[L - 1:L, :] + mu_last
        decay = jnp.exp(m_prev - mu_last)
        wk_row = jnp.exp(a_row - mu_last)
        vt = v.astype(F32).T
        c_ref[h] = decay * c_prev + _dot((vt * wk_row).astype(BF16), k)
        wk8 = jnp.broadcast_to(wk_row, (SUBLANES, L)).astype(BF16)
        n_ref[h] = decay * n_ref[h] + _dot(wk8, k)
        m_ref[h] = jnp.broadcast_to(m_new, (SUBLANES, LANES))


def _mlstm_call(proj, ifc, ift, conv_w, layer, ltri, utri, batch, seq):
    n = batch * seq
    L = ML_CHUNK
    nc = seq // L
    W = BRANCH_W
    return pl.pallas_call(
        _mlstm_kernel,
        grid=(batch, nc),
        in_specs=[
            pl.BlockSpec((L, 2 * W), lambda b, i: (b * nc + i, MLQK_COL // (2 * W))),
            pl.BlockSpec((L, W), lambda b, i: (b * nc + i, 4)),
            pl.BlockSpec((L, W), lambda b, i: (b * nc + i, 5)),
            pl.BlockSpec((L, LANES), lambda b, i: (b * nc + i, 0)),
            pl.BlockSpec((2 * SUBLANES, L), lambda b, i: (0, b * nc + i)),
            _layer_spec(layer, (CONV_W, 2 * W), lambda b, i: (0, 0)),
            pl.BlockSpec((L, L), lambda b, i: (0, 0)),
            pl.BlockSpec((L, L), lambda b, i: (0, 0)),
        ],
        out_specs=pl.BlockSpec((L, W), lambda b, i: (b * nc + i, 0)),
        out_shape=jax.ShapeDtypeStruct((n, W), BF16),
        scratch_shapes=[
            pltpu.VMEM((L + SUBLANES, 2 * W), F32),
            pltpu.VMEM((ML_HEADS, ML_HDIM, ML_HDIM), F32),
            pltpu.VMEM((ML_HEADS, SUBLANES, ML_HDIM), F32),
            pltpu.VMEM((ML_HEADS, SUBLANES, LANES), F32),
        ],
        compiler_params=_cparams("parallel", "arbitrary"),
        name="mlstm",
    )(proj, proj, proj, ifc, ift, conv_w, ltri, utri)


def _merge_kernel(ysb_ref, ymb_ref, yml_ref, gl_ref, x_ref, wb_ref, wo_ref, g2_ref, wr_ref, br_ref,
                  h_ref, xn_ref, route_ref, route_t_ref):
    merged = jnp.zeros((ROW_TILE, D_MODEL), F32)
    for b, y_ref in enumerate((ysb_ref, ymb_ref, yml_ref)):
        gate = _sigmoid(gl_ref[:, b * D_MODEL:(b + 1) * D_MODEL].astype(F32))
        merged = merged + gate * _dot(y_ref[...], wb_ref[b])
    hres = x_ref[...] + _dot(merged.astype(BF16), wo_ref[...])
    h_ref[...] = hres

    ms = jnp.mean(hres * hres, axis=-1, keepdims=True)
    xn = (hres * lax.rsqrt(ms + RMS_EPS) * g2_ref[...]).astype(BF16)
    xn_ref[...] = xn

    logits = _dot(xn, wr_ref[...]) + br_ref[...]
    lane = lax.broadcasted_iota(jnp.int32, (ROW_TILE, LANES), 1).astype(F32)
    far = float(LANES)
    is_g = lane < N_GROUPS
    gl = jnp.where(is_g, logits, NEG_BIG)
    gmax = jnp.max(gl, axis=-1, keepdims=True)
    gsum = jnp.sum(jnp.where(is_g, jnp.exp(gl - gmax), 0.0), axis=-1, keepdims=True)
    g_w = 1.0 / gsum
    g_idx = jnp.min(jnp.where(is_g & (gl == gmax), lane, far), axis=-1, keepdims=True)
    e_lo = N_GROUPS + g_idx * EXPERTS_PER_GROUP
    in_grp = (lane >= e_lo) & (lane < e_lo + EXPERTS_PER_GROUP)
    el = jnp.where(in_grp, logits, NEG_BIG)
    l1 = jnp.max(el, axis=-1, keepdims=True)
    i1 = jnp.min(jnp.where(in_grp & (el == l1), lane, far), axis=-1, keepdims=True)
    el2 = jnp.where(lane == i1, NEG_BIG, el)
    l2 = jnp.max(el2, axis=-1, keepdims=True)
    i2 = jnp.min(jnp.where(in_grp & (el2 == l2), lane, far), axis=-1, keepdims=True)
    p2 = jnp.exp(l2 - l1)
    w1 = g_w / (1.0 + p2)
    w2 = g_w * p2 / (1.0 + p2)
    slot = lane + e_lo
    cw = jnp.where(slot == i1, w1, 0.0) + jnp.where(slot == i2, w2, 0.0)
    cw = jnp.where(lane < EXPERTS_PER_GROUP, cw, 0.0)
    route = jnp.where(lane == EXPERTS_PER_GROUP, g_idx, cw)
    route_ref[...] = route
    route_t_ref[...] = route.T[0:2 * SUBLANES, :]


def _merge_call(ysb, ymb, yml, p16, x, wb, wo, g2, wr, br, layer):
    n = x.shape[0]
    row = lambda i: (i, 0)
    const2 = lambda i: (0, 0)
    return pl.pallas_call(
        _merge_kernel,
        grid=(n // ROW_TILE,),
        in_specs=[
            pl.BlockSpec((ROW_TILE, BRANCH_W), row),
            pl.BlockSpec((ROW_TILE, BRANCH_W), row),
            pl.BlockSpec((ROW_TILE, BRANCH_W), row),
            pl.BlockSpec((ROW_TILE, N_BRANCH * D_MODEL), lambda i: (i, 1)),
            pl.BlockSpec((ROW_TILE, D_MODEL), row),
            _layer_spec(layer, (N_BRANCH, BRANCH_W, D_MODEL), lambda i: (0, 0, 0)),
            _layer_spec(layer, (D_MODEL, D_MODEL), const2),
            _layer_spec(layer, (1, D_MODEL), const2),
            _layer_spec(layer, (D_MODEL, LANES), const2),
            _layer_spec(layer, (1, LANES), const2),
        ],
        out_specs=[
            pl.BlockSpec((ROW_TILE, D_MODEL), row),
            pl.BlockSpec((ROW_TILE, D_MODEL), row),
            pl.BlockSpec((ROW_TILE, LANES), row),
            pl.BlockSpec((2 * SUBLANES, ROW_TILE), lambda i: (0, i)),
        ],
        out_shape=[
            jax.ShapeDtypeStruct((n, D_MODEL), F32),
            jax.ShapeDtypeStruct((n, D_MODEL), BF16),
            jax.ShapeDtypeStruct((n, LANES), F32),
            jax.ShapeDtypeStruct((2 * SUBLANES, n), F32),
        ],
        compiler_params=_cparams("parallel"),
        name="merge_outproj_router",
    )(ysb, ymb, yml, p16, x, wb, wo, g2, wr, br)


def _moe_kernel(x_ref, h_ref, rt_ref, rtt_ref, lx_ref, w1_ref, w3_ref, w2_ref, y_ref,
                keyc_ref, keyr_ref):
    tb = MOE_BLOCK
    sub = MOE_SUB
    grp = pl.program_id(1).astype(F32)
    gid_lane = EXPERTS_PER_GROUP
    gid_col = rt_ref[:, gid_lane:gid_lane + 1]
    route_t = rtt_ref[...]
    gid_row = route_t[gid_lane:gid_lane + 1, :]

    @pl.when(pl.program_id(1) == 0)
    def _():
        y_ref[...] = h_ref[...]
        lx = lx_ref[...]
        lane = lax.broadcasted_iota(jnp.int32, (tb, LANES), 1).astype(F32)
        own_col = lane == gid_col
        before_col = _dot(lx, jnp.where(own_col, 1.0, 0.0).astype(BF16))
        pos_col = jnp.sum(jnp.where(own_col, before_col, 0.0), axis=-1, keepdims=True)
        keyc_ref[...] = jnp.broadcast_to(pos_col, (tb, LANES))
        row = lax.broadcasted_iota(jnp.int32, (SUBLANES, tb), 0).astype(F32)
        own_row = row == gid_row
        before_row = _dot_nt(jnp.where(own_row, 1.0, 0.0).astype(BF16), lx)
        pos_row = jnp.sum(jnp.where(own_row, before_row, 0.0), axis=0, keepdims=True)
        keyr_ref[...] = jnp.broadcast_to(pos_row, (SUBLANES, tb))

    in_col = gid_col == grp
    in_row = gid_row == grp
    key_col = jnp.where(in_col, keyc_ref[...], -1.0)
    key_col = jnp.concatenate([key_col] * (sub // LANES), axis=1)
    key_row = jnp.where(in_row, keyr_ref[0:1, :], -1.0)
    count = jnp.sum(jnp.where(in_row, 1.0, 0.0)).astype(jnp.int32)
    rem = count % sub
    short_tail = (rem > 0) & (rem <= sub // 2)
    n_sub = count // sub + jnp.where(rem > sub // 2, 1, 0)
    t_hi, t_mid, t_lo = _split3(route_t)

    def experts_of_sub_tile(base, rows):
        slot_rows = lax.broadcasted_iota(jnp.int32, (rows, tb), 0).astype(F32) + base
        pick = jnp.where(key_row == slot_rows, 1.0, 0.0).astype(BF16)
        xs = _dot(pick, x_ref[...]).astype(BF16)
        cw_t = _dot_nt(t_hi, pick) + _dot_nt(t_mid, pick) + _dot_nt(t_lo, pick)
        cw = jnp.concatenate(
            [cw_t, jnp.zeros((LANES - 2 * SUBLANES, rows), F32)], axis=0).T
        experts = range(EXPERTS_PER_GROUP)
        ups = [(_dot(xs, w1_ref[e]), _dot(xs, w3_ref[e])) for e in experts]
        hid = jnp.concatenate(
            [(a * _sigmoid(a) * b * cw[:, e:e + 1]).astype(BF16) for e, (a, b) in enumerate(ups)],
            axis=1)
        y = _dot(hid, w2_ref[...].reshape(EXPERTS_PER_GROUP * D_EXPERT, D_MODEL))
        return y.astype(BF16)

    def sub_tile(base, rows):
        y = experts_of_sub_tile(base, rows)
        slot_cols = lax.broadcasted_iota(jnp.int32, (tb, rows), 1).astype(F32) + base
        put = jnp.where(key_col[:, :rows] == slot_cols, 1.0, 0.0).astype(BF16)
        y_ref[...] += _dot(put, y)

    def full_tile(s, c):
        sub_tile((s * sub).astype(F32), sub)
        return c

    lax.fori_loop(0, n_sub, full_tile, 0)

    @pl.when(short_tail)
    def _():
        sub_tile((n_sub * sub).astype(F32), sub // 2)


def _moe_call(xn2, h, route, route_t, lx, w1b, w3b, w2b, layer):
    n = xn2.shape[0]
    tb = MOE_BLOCK
    e = EXPERTS_PER_GROUP
    return pl.pallas_call(
        _moe_kernel,
        grid=(n // tb, N_GROUPS),
        in_specs=[
            pl.BlockSpec((tb, D_MODEL), lambda i, g: (i, 0)),
            pl.BlockSpec((tb, D_MODEL), lambda i, g: (i, 0)),
            pl.BlockSpec((tb, LANES), lambda i, g: (i, 0)),
            pl.BlockSpec((2 * SUBLANES, tb), lambda i, g: (0, i)),
            pl.BlockSpec((tb, tb), lambda i, g: (0, 0)),
            _layer_spec(layer, (e, D_MODEL, D_EXPERT), lambda i, g: (g, 0, 0)),
            _layer_spec(layer, (e, D_MODEL, D_EXPERT), lambda i, g: (g, 0, 0)),
            _layer_spec(layer, (e, D_EXPERT, D_MODEL), lambda i, g: (g, 0, 0)),
        ],
        out_specs=pl.BlockSpec((tb, D_MODEL), lambda i, g: (i, 0)),
        out_shape=jax.ShapeDtypeStruct((n, D_MODEL), F32),
        scratch_shapes=[
            pltpu.VMEM((tb, LANES), F32),
            pltpu.VMEM((SUBLANES, tb), F32),
        ],
        compiler_params=_cparams("parallel", "arbitrary"),
        name="moe_group_experts",
    )(xn2, h, route, route_t, lx, w1b, w3b, w2b)


def _tables(positions):
    t = ATT_TILE
    r = jnp.arange(t)
    u_tri = (r[:, None] >= r[None, :]).astype(BF16)
    eye = jnp.eye(t, dtype=BF16)
    L = ML_CHUNK
    rl = jnp.arange(L)
    ltri = (rl[:, None] >= rl[None, :]).astype(BF16)
    utri = ltri.T

    li = jnp.arange(LANES)
    seg = ((li[:, None] // HEAD_DIM) == (li[None, :] // HEAD_DIM)).astype(BF16)
    half = ROPE_DIM // 2
    src, dst = li[:, None], li[None, :]
    d_in = dst % HEAD_DIM
    rot = jnp.where((d_in < half) & (src == dst + half), -1.0,
                    jnp.where((d_in >= half) & (d_in < ROPE_DIM) & (src == dst - half), 1.0, 0.0)
                    ).astype(BF16)

    inv_freq = jnp.power(jnp.float32(ROPE_THETA), -jnp.arange(half, dtype=F32) / half)
    lane_in = li % HEAD_DIM
    ang = positions.reshape(-1).astype(F32)[:, None] * inv_freq[lane_in % half][None, :]
    rotated = (lane_in < ROPE_DIM)[None, :]
    cos_t = jnp.where(rotated, jnp.cos(ang), 1.0)
    sin_t = jnp.where(rotated, jnp.sin(ang), 0.0)
    rb = jnp.arange(MOE_BLOCK)
    lx = (rb[None, :] < rb[:, None]).astype(BF16)
    return dict(u_tri=u_tri, eye=eye, ltri=ltri, utri=utri, seg=seg, rot=rot, cos=cos_t, sin=sin_t,
                lx=lx)


def kernel(x, positions, norm1_g, w_in, qn_g, kn_g, conv_w, ml_gate_b, w_branch, w_out, norm2_g,
           w_rg, b_rg, w_re, b_re, w1, w3, w2):
    batch, seq, d = x.shape
    n = batch * seq
    W = BRANCH_W
    depth = w_in.shape[0]
    tb = _tables(positions)
    xf = x.reshape(n, d)

    n_if = 2 * ML_HEADS
    n_route = N_GROUPS + N_EXPERTS
    w_proj = jnp.concatenate(
        [w_in[:, :, 0:3 * W], w_in[:, :, 5 * W:6 * W], w_in[:, :, 8 * W:10 * W],
         w_in[:, :, 10 * W + n_if:], w_in[:, :, 3 * W:5 * W], w_in[:, :, 6 * W:8 * W]],
        axis=2).astype(BF16)
    wif = jnp.pad(w_in[:, :, 10 * W:10 * W + n_if],
                  ((0, 0), (0, 0), (0, LANES - n_if))).astype(BF16)
    bif = jnp.pad(ml_gate_b, ((0, 0), (0, LANES - n_if))).reshape(depth, 1, LANES)
    g1 = norm1_g.reshape(depth, 1, d)
    g2 = norm2_g.reshape(depth, 1, d)
    qg = jnp.tile(qn_g, (1, LANES // HEAD_DIM)).reshape(depth, 1, LANES)
    kg = jnp.tile(kn_g, (1, LANES // HEAD_DIM)).reshape(depth, 1, LANES)
    wr = jnp.pad(jnp.concatenate([w_rg, w_re], axis=2),
                 ((0, 0), (0, 0), (0, LANES - n_route))).astype(BF16)
    br = jnp.pad(jnp.concatenate([b_rg, b_re], axis=1),
                 ((0, 0), (0, LANES - n_route))).reshape(depth, 1, LANES)
    wb = w_branch.astype(BF16)
    wo = w_out.astype(BF16)
    w1b, w3b, w2b = w1.astype(BF16), w3.astype(BF16), w2.astype(BF16)

    for l in range(depth):
        p16, ifc, ift = _norm_proj_call(xf, g1, wif, bif, w_proj, l)

        y_sb = _sb_call(p16, tb["u_tri"], batch, seq)

        qa, ka, km = _moba_prep_call(p16, tb["cos"], tb["sin"], qg, kg, l, tb["seg"], tb["rot"],
                                     batch, seq)
        qb = _moba_gate_call(qa, km.transpose(0, 2, 1, 3), tb["eye"], batch, seq)
        y_mb = _moba_call(qb, ka, p16, batch, seq)

        y_ml = _mlstm_call(p16, ifc, ift, conv_w, l, tb["ltri"], tb["utri"], batch, seq)

        hres, xn2, route, route_t = _merge_call(y_sb, y_mb, y_ml, p16, xf, wb, wo, g2, wr, br, l)
        xf = _moe_call(xn2, hres, route, route_t, tb["lx"], w1b, w3b, w2b, l)

    return xf.reshape(batch, seq, d)
```

```python
import functools

import jax
import jax.numpy as jnp
from jax import lax
from jax.experimental import pallas as pl
from jax.experimental.pallas import tpu as pltpu

D_MODEL = 1024
BRANCH_W = D_MODEL // 2
N_BRANCH = 3
HEAD_DIM = 64
N_ATT_HEADS = BRANCH_W // HEAD_DIM
MB_BLOCK = 256
MB_TOPK = 3
ROPE_THETA = 500000.0
ROPE_DIM = HEAD_DIM // 4
ML_HEADS = 4
ML_HDIM = BRANCH_W // ML_HEADS
CONV_W = 4
N_GROUPS = 4
EXPERTS_PER_GROUP = 8
N_EXPERTS = N_GROUPS * EXPERTS_PER_GROUP
D_EXPERT = D_MODEL // 4
RMS_EPS = 1e-6

LANES = 128
SUBLANES = 8
VMEM_LIMIT_BYTES = 56 * 1024 * 1024

ROW_TILE = 512
COL_TILE = 2048
ATT_TILE = 256
ML_CHUNK = 256
MBQ_COL = 12 * BRANCH_W
MLQK_COL = 14 * BRANCH_W
MOE_BLOCK = 1024
MOE_SUB = 256

LOG2_E = 1.4426950408889634
SB_PAIRS = 4
SB_EXIT_LOG2 = 160.0
MB_PAIRS = 4
NEG_BIG = -(2.0 ** 100)

F32 = jnp.float32
BF16 = jnp.bfloat16


def _cparams(*sem):
    return pltpu.CompilerParams(dimension_semantics=sem, vmem_limit_bytes=VMEM_LIMIT_BYTES)


def _layer_spec(layer, block, index_map):
    return pl.BlockSpec((None,) + tuple(block), lambda *idx: (layer,) + tuple(index_map(*idx)))


def _dot(a, b):
    return jnp.dot(a, b, preferred_element_type=F32)


def _dot_nt(a, b):
    return lax.dot_general(a, b, (((1,), (1,)), ((), ())), preferred_element_type=F32)


def _split3(x):
    hi = x.astype(BF16)
    r1 = x - hi.astype(F32)
    mid = r1.astype(BF16)
    lo = (r1 - mid.astype(F32)).astype(BF16)
    return hi, mid, lo


def _dot_exact_rhs(x, m):
    hi, mid, lo = _split3(x)
    return _dot(hi, m) + _dot(mid, m) + _dot(lo, m)


def _dot_2term_rhs(x, m):
    hi = x.astype(BF16)
    lo = (x - hi.astype(F32)).astype(BF16)
    return _dot(hi, m) + _dot(lo, m)


def _dot_exact_lhs(m, x):
    hi, mid, lo = _split3(x)
    return _dot(m, hi) + _dot(m, mid) + _dot(m, lo)


def _log_sigmoid(x):
    return jnp.minimum(x, 0.0) - jnp.log(1.0 + jnp.exp(-jnp.abs(x)))


def _sigmoid(x):
    return 0.5 * jnp.tanh(0.5 * x) + 0.5


def _norm_kernel(x_ref, g_ref, wif_ref, bif_ref, xn_ref, if_ref, ift_ref):
    x = x_ref[...]
    ms = jnp.mean(x * x, axis=-1, keepdims=True)
    xn = (x * lax.rsqrt(ms + RMS_EPS) * g_ref[...]).astype(BF16)
    xn_ref[...] = xn
    gates = _dot(xn, wif_ref[...]) + bif_ref[...]
    if_ref[...] = gates
    ift_ref[...] = gates.T[0:2 * SUBLANES, :]


def _norm_proj_kernel(x_ref, g_ref, wif_ref, bif_ref, w_ref, o_ref, if_ref, ift_ref, xn_scr):
    @pl.when(pl.program_id(1) == 0)
    def _():
        _norm_kernel(x_ref, g_ref, wif_ref, bif_ref, xn_scr, if_ref, ift_ref)

    o_ref[...] = _dot(xn_scr[...], w_ref[...]).astype(o_ref.dtype)


def _norm_proj_call(x, g, wif, bif, w, layer):
    n, k = x.shape
    c = w.shape[2]
    tm = 2 * ROW_TILE
    return pl.pallas_call(
        _norm_proj_kernel,
        grid=(n // tm, c // COL_TILE),
        in_specs=[
            pl.BlockSpec((tm, k), lambda i, j: (i, 0)),
            _layer_spec(layer, (1, k), lambda i, j: (0, 0)),
            _layer_spec(layer, (k, LANES), lambda i, j: (0, 0)),
            _layer_spec(layer, (1, LANES), lambda i, j: (0, 0)),
            _layer_spec(layer, (k, COL_TILE), lambda i, j: (0, j)),
        ],
        out_specs=[
            pl.BlockSpec((tm, COL_TILE), lambda i, j: (i, j)),
            pl.BlockSpec((tm, LANES), lambda i, j: (i, 0)),
            pl.BlockSpec((2 * SUBLANES, tm), lambda i, j: (0, i)),
        ],
        out_shape=[
            jax.ShapeDtypeStruct((n, c), BF16),
            jax.ShapeDtypeStruct((n, LANES), F32),
            jax.ShapeDtypeStruct((2 * SUBLANES, n), F32),
        ],
        scratch_shapes=[pltpu.VMEM((tm, k), BF16)],
        compiler_params=_cparams("parallel", "arbitrary"),
        name="norm_inproj",
    )(x, g, wif, bif, w)


def _sb_kernel(q_ref, k_ref, v_ref, u_ref, o_ref, acc_ref, carry_ref):
    t = ATT_TILE
    qi = pl.program_id(1)
    lane = lax.broadcasted_iota(jnp.int32, (1, LANES), 1)
    head_mask = (lane < HEAD_DIM, lane >= HEAD_DIM)
    n_heads = 2 * SB_PAIRS
    q = q_ref[...].astype(F32) * (HEAD_DIM ** -0.5 * LOG2_E)
    qh = []
    for p in range(SB_PAIRS):
        qp = q[:, p * LANES:(p + 1) * LANES]
        qh.extend(jnp.where(m, qp, 0.0).astype(BF16) for m in head_mask)
    u_tri = u_ref[...]

    acc_ref[...] = jnp.zeros_like(acc_ref)
    carry_ref[...] = jnp.zeros_like(carry_ref)

    def tile_step(kj, diagonal):
        start = pl.multiple_of(kj * t, t)
        if diagonal:
            row = lax.broadcasted_iota(jnp.int32, (t, t), 0)
            col = lax.broadcasted_iota(jnp.int32, (t, t), 1)
            past = col < row
        heads = range(n_heads)
        zs = [_dot_nt(qh[i], k_ref[pl.ds(start, t), (i // 2) * LANES:(i // 2 + 1) * LANES])
              for i in heads]
        sps = [jnp.maximum(z, 0.0) + jnp.log2(1.0 + jnp.exp2(-jnp.abs(z))) for z in zs]
        if diagonal:
            sps = [jnp.where(past, sp, 0.0) for sp in sps]
        rs = [_dot(sps[i].astype(BF16), u_tri)
              + jnp.concatenate([carry_ref[i]] * (t // LANES), axis=1) for i in heads]
        ws = [jnp.exp2((zs[i] - rs[i]).astype(BF16)) for i in heads]
        if diagonal:
            ws = [jnp.where(past, w, jnp.zeros_like(w)) for w in ws]
        carry_min = None
        for i in heads:
            v = v_ref[pl.ds(start, t), (i // 2) * LANES:(i // 2 + 1) * LANES]
            vh = jnp.where(head_mask[i % 2], v, jnp.zeros_like(v))
            acc_ref[i] += _dot(ws[i].astype(BF16), vh)
            carry = jnp.broadcast_to(rs[i][:, 0:1], (t, LANES))
            carry_ref[i] = carry
            carry_min = carry if i == 0 else jnp.minimum(carry_min, carry)
        return jnp.min(carry_min)

    def cond(c):
        it, carry_min = c
        return (it < qi) & (carry_min < SB_EXIT_LOG2)

    def body(c):
        it, _ = c
        return it + 1, tile_step(qi - 1 - it, False)

    lax.while_loop(cond, body, (jnp.int32(0), tile_step(qi, True)))
    for p in range(SB_PAIRS):
        o_ref[:, p * LANES:(p + 1) * LANES] = (acc_ref[2 * p] + acc_ref[2 * p + 1]).astype(o_ref.dtype)


def _sb_call(p16, u_tri, batch, seq):
    n = batch * seq
    t = ATT_TILE
    nq = seq // t
    groups = BRANCH_W // (SB_PAIRS * LANES)
    w = SB_PAIRS * LANES
    return pl.pallas_call(
        _sb_kernel,
        grid=(batch * groups, nq),
        in_specs=[
            pl.BlockSpec((t, w), lambda g, i: ((g // groups) * nq + i, g % groups)),
            pl.BlockSpec((seq, w), lambda g, i: (g // groups, groups + g % groups)),
            pl.BlockSpec((seq, w), lambda g, i: (g // groups, 2 * groups + g % groups)),
            pl.BlockSpec((t, t), lambda g, i: (0, 0)),
        ],
        out_specs=pl.BlockSpec((t, w), lambda g, i: ((g // groups) * nq + i, g % groups)),
        out_shape=jax.ShapeDtypeStruct((n, BRANCH_W), BF16),
        scratch_shapes=[
            pltpu.VMEM((2 * SB_PAIRS, t, LANES), F32),
            pltpu.VMEM((2 * SB_PAIRS, t, LANES), F32),
        ],
        compiler_params=_cparams("parallel", "arbitrary"),
        name="stick_breaking_attention",
    )(p16, p16, p16, u_tri)


def _moba_prep_kernel(q_ref, k_ref, cos_ref, sin_ref, qg_ref, kg_ref, seg_ref, rot_ref,
                      qa_ref, ka_ref, km_ref):
    blk = pl.program_id(1)
    lane = lax.broadcasted_iota(jnp.int32, (1, LANES), 1)
    even = lane < HEAD_DIM
    cos = cos_ref[...]
    sin = sin_ref[...]
    seg = seg_ref[...]
    rot = rot_ref[...]

    def norm_rope(x, g):
        ss = _dot_2term_rhs(x * x, seg)
        xn = x * lax.rsqrt(ss * (1.0 / HEAD_DIM) + RMS_EPS) * g
        return xn * cos + _dot_2term_rhs(xn, rot) * sin

    onehot_even = jnp.where(lane == HEAD_DIM + blk, 1.0, 0.0)
    onehot_odd = jnp.where(lane == blk, 1.0, 0.0)

    km_rows = []
    for c in range(BRANCH_W // LANES):
        sl = slice(c * LANES, (c + 1) * LANES)
        qr = norm_rope(q_ref[:, sl].astype(F32), qg_ref[...]) * (HEAD_DIM ** -0.5 * LOG2_E)
        kr = norm_rope(k_ref[:, sl].astype(F32), kg_ref[...])
        qa_ref[0, 2 * c] = jnp.where(even, qr, 0.0).astype(BF16)
        qa_ref[0, 2 * c + 1] = jnp.where(even, 0.0, qr).astype(BF16)
        ka_ref[0, 2 * c] = jnp.where(even, kr, onehot_even).astype(BF16)
        ka_ref[0, 2 * c + 1] = jnp.where(even, onehot_odd, kr).astype(BF16)
        kmean = jnp.mean(kr, axis=0, keepdims=True)
        km_rows.append(jnp.where(even, kmean, 0.0))
        km_rows.append(jnp.where(even, 0.0, kmean))
    km_ref[0, 0] = jnp.concatenate(km_rows, axis=0)


def _moba_prep_call(proj, cos_t, sin_t, qg, kg, layer, seg, rot, batch, seq):
    t = MB_BLOCK
    nb = seq // t
    h = N_ATT_HEADS
    return pl.pallas_call(
        _moba_prep_kernel,
        grid=(batch, nb),
        in_specs=[
            pl.BlockSpec((t, BRANCH_W), lambda b, i: (b * nb + i, MBQ_COL // BRANCH_W)),
            pl.BlockSpec((t, BRANCH_W), lambda b, i: (b * nb + i, MBQ_COL // BRANCH_W + 1)),
            pl.BlockSpec((t, LANES), lambda b, i: (b * nb + i, 0)),
            pl.BlockSpec((t, LANES), lambda b, i: (b * nb + i, 0)),
            _layer_spec(layer, (1, LANES), lambda b, i: (0, 0)),
            _layer_spec(layer, (1, LANES), lambda b, i: (0, 0)),
            pl.BlockSpec((LANES, LANES), lambda b, i: (0, 0)),
            pl.BlockSpec((LANES, LANES), lambda b, i: (0, 0)),
        ],
        out_specs=[
            pl.BlockSpec((1, h, t, LANES), lambda b, i: (b, 0, i, 0)),
            pl.BlockSpec((1, h, t, LANES), lambda b, i: (b, 0, i, 0)),
            pl.BlockSpec((1, 1, h, LANES), lambda b, i: (b, i, 0, 0)),
        ],
        out_shape=[
            jax.ShapeDtypeStruct((batch, h, seq, LANES), BF16),
            jax.ShapeDtypeStruct((batch, h, seq, LANES), BF16),
            jax.ShapeDtypeStruct((batch, nb, h, LANES), F32),
        ],
        compiler_params=_cparams("parallel", "parallel"),
        name="moba_qk_prep",
    )(proj, proj, cos_t, sin_t, qg, kg, seg, rot)


def _moba_gate_kernel(qa_ref, km_ref, eye_ref, qb_ref, *, nb):
    t = ATT_TILE
    seq = qa_ref.shape[2]
    blk_row = lax.broadcasted_iota(jnp.int32, (nb, seq), 0)
    own = lax.broadcasted_iota(jnp.int32, (nb, seq), 1) // t
    for h in range(2):
        q = qa_ref[0, h]
        km_hi, km_mid, km_lo = _split3(km_ref[0, h])
        g = _dot_nt(km_hi, q) + _dot_nt(km_mid, q) + _dot_nt(km_lo, q)
        cnt = jnp.zeros((nb, seq), jnp.int32)
        for m in range(nb):
            gm = g[m:m + 1, :]
            beats = (gm > g) | ((gm == g) & (m < blk_row))
            cnt = cnt + jnp.where(beats & (m < own), 1, 0)
        keep = ((blk_row < own) & (cnt < MB_TOPK)) | (blk_row == own)
        bias = jnp.where(keep, 0.0, NEG_BIG)
        aux_lo = HEAD_DIM if h == 0 else 0
        pieces = []
        if aux_lo:
            pieces.append(jnp.zeros((aux_lo, seq), F32))
        pieces.append(bias)
        pieces.append(jnp.zeros((LANES - aux_lo - nb, seq), F32))
        bias_rows = jnp.concatenate(pieces, axis=0).astype(BF16)
        for j in range(seq // t):
            sl = slice(j * t, (j + 1) * t)
            bias_cols = _dot_nt(eye_ref[...], bias_rows[:, sl])
            qb_ref[0, h, sl, :] = (q[sl, :].astype(F32) + bias_cols).astype(BF16)


def _moba_gate_call(qa, km, eye, batch, seq):
    nb = seq // MB_BLOCK
    pairs = N_ATT_HEADS // 2
    spec = pl.BlockSpec((1, 2, seq, LANES), lambda b, p: (b, p, 0, 0))
    return pl.pallas_call(
        functools.partial(_moba_gate_kernel, nb=nb),
        grid=(batch, pairs),
        in_specs=[
            spec,
            pl.BlockSpec((1, 2, nb, LANES), lambda b, p: (b, p, 0, 0)),
            pl.BlockSpec((ATT_TILE, ATT_TILE), lambda b, p: (0, 0)),
        ],
        out_specs=spec,
        out_shape=jax.ShapeDtypeStruct(qa.shape, BF16),
        compiler_params=_cparams("parallel", "parallel"),
        name="moba_block_gate",
    )(qa, km, eye)


def _moba_kernel(qb_ref, ka_ref, v_ref, o_ref, s_scr, mx_ref, acc_ref):
    t = ATT_TILE
    own = pl.program_id(1)
    n_heads = 2 * MB_PAIRS
    lane = lax.broadcasted_iota(jnp.int32, (1, LANES), 1)
    head_mask = (lane < HEAD_DIM, lane >= HEAD_DIM)

    heads = range(n_heads)

    def lane_fold(x, op):
        parts = [x[:, c * LANES:(c + 1) * LANES] for c in range(t // LANES)]
        return functools.reduce(op, parts)

    def score_step(kj, diagonal):
        start = pl.multiple_of(kj * t, t)
        ss = [_dot_nt(qb_ref[0, h], ka_ref[0, h, pl.ds(start, t), :]) for h in heads]
        if diagonal:
            row = lax.broadcasted_iota(jnp.int32, (t, t), 0)
            col = lax.broadcasted_iota(jnp.int32, (t, t), 1)
            ss = [jnp.where(col <= row, s, NEG_BIG) for s in ss]
        for h in heads:
            s_scr[h, kj] = ss[h]
            part = lane_fold(ss[h], jnp.maximum)
            mx_ref[h] = part if diagonal else jnp.maximum(mx_ref[h], part)

    score_step(own, True)

    def score_body(it, c):
        score_step(2 * it, False)
        score_step(2 * it + 1, False)
        return c

    lax.fori_loop(0, own // 2, score_body, 0)

    @pl.when(own % 2 == 1)
    def _():
        score_step(own - 1, False)

    for h in heads:
        mx_ref[h] = jnp.broadcast_to(jnp.max(mx_ref[h], axis=-1, keepdims=True), (t, LANES))
    acc_ref[...] = jnp.zeros_like(acc_ref)

    ones_lane = (HEAD_DIM, 0)
    ones_col = [jnp.where(lane == ones_lane[e], 1.0, 0.0).astype(BF16) for e in range(2)]

    def value_step(kj):
        start = pl.multiple_of(kj * t, t)
        ps = [jnp.exp2((s_scr[h, kj] - jnp.concatenate([mx_ref[h]] * (t // LANES), axis=1))
                       .astype(BF16)) for h in heads]
        for h in heads:
            v = v_ref[pl.ds(start, t), (h // 2) * LANES:(h // 2 + 1) * LANES]
            vh = jnp.where(head_mask[h % 2], v, ones_col[h % 2])
            acc_ref[h] += _dot(ps[h], vh)

    def value_body(it, c):
        value_step(2 * it)
        value_step(2 * it + 1)
        return c

    lax.fori_loop(0, (own + 1) // 2, value_body, 0)

    @pl.when(own % 2 == 0)
    def _():
        value_step(own)

    for pr in range(MB_PAIRS):
        acc0 = acc_ref[2 * pr]
        acc1 = acc_ref[2 * pr + 1]
        inv0 = 1.0 / acc0[:, ones_lane[0]:ones_lane[0] + 1]
        inv1 = 1.0 / acc1[:, ones_lane[1]:ones_lane[1] + 1]
        out = jnp.where(head_mask[0], acc0 * inv0, acc1 * inv1)
        o_ref[:, pr * LANES:(pr + 1) * LANES] = out.astype(o_ref.dtype)


def _moba_call(qb, ka, p16, batch, seq):
    n = batch * seq
    t = ATT_TILE
    nq = seq // t
    nb = seq // MB_BLOCK
    groups = BRANCH_W // (MB_PAIRS * LANES)
    nh = 2 * MB_PAIRS
    w = MB_PAIRS * LANES
    v_col0 = 3 * BRANCH_W // w
    return pl.pallas_call(
        _moba_kernel,
        grid=(batch * groups, nq),
        in_specs=[
            pl.BlockSpec((1, nh, t, LANES), lambda g, i: (g // groups, g % groups, i, 0)),
            pl.BlockSpec((1, nh, seq, LANES), lambda g, i: (g // groups, g % groups, 0, 0)),
            pl.BlockSpec((seq, w), lambda g, i: (g // groups, v_col0 + g % groups)),
        ],
        out_specs=pl.BlockSpec((t, w), lambda g, i: ((g // groups) * nq + i, g % groups)),
        out_shape=jax.ShapeDtypeStruct((n, BRANCH_W), BF16),
        scratch_shapes=[
            pltpu.VMEM((nh, nb, t, t), F32),
            pltpu.VMEM((nh, t, LANES), F32),
            pltpu.VMEM((nh, t, LANES), F32),
        ],
        compiler_params=_cparams("parallel", "arbitrary"),
        name="moba_attention",
    )(qb, ka, p16)


def _mlstm_kernel(u_ref, v_ref, o_ref, ifc_ref, ift_ref, cw_ref, ltri_ref, utri_ref, y_ref,
                  xbuf, c_ref, n_ref, m_ref):
    L = ML_CHUNK
    W = BRANCH_W
    halo = SUBLANES

    @pl.when(pl.program_id(1) == 0)
    def _():
        xbuf[0:halo, :] = jnp.zeros((halo, 2 * W), F32)
        c_ref[...] = jnp.zeros_like(c_ref)
        n_ref[...] = jnp.zeros_like(n_ref)
        m_ref[...] = jnp.zeros_like(m_ref)

    xbuf[halo:, :] = u_ref[...].astype(F32)

    def conv_silu(c0):
        cols = slice(c0, c0 + ML_HDIM)
        conv = jnp.zeros((L, ML_HDIM), F32)
        for j in range(CONV_W):
            off = halo - (CONV_W - 1) + j
            conv = conv + cw_ref[j:j + 1, cols] * xbuf[off:off + L, cols]
        return conv * _sigmoid(conv)

    ift = ift_ref[0:SUBLANES, :]
    lf_rows = _log_sigmoid(ift)
    bcum_rows = _dot_exact_rhs(lf_rows, utri_ref[...])
    lf_cols = _log_sigmoid(ifc_ref[...])
    bcum_cols = _dot_exact_lhs(ltri_ref[...], lf_cols)

    row = lax.broadcasted_iota(jnp.int32, (L, L), 0)
    col = lax.broadcasted_iota(jnp.int32, (L, L), 1)
    causal = col <= row

    for h in range(ML_HEADS):
        sl = slice(h * ML_HDIM, (h + 1) * ML_HDIM)
        q = conv_silu(h * ML_HDIM).astype(BF16)
        k = (conv_silu(W + h * ML_HDIM) * (ML_HDIM ** -0.5)).astype(BF16)
        v = v_ref[:, sl]
        a_row = ift[h:h + 1, :] - bcum_rows[ML_HEADS + h:ML_HEADS + h + 1, :]
        bc = bcum_cols[:, ML_HEADS + h:ML_HEADS + h + 1]
        m_prev = m_ref[h, 0:1, 0:1]

        a_mat = jnp.where(causal, a_row, NEG_BIG)
        mu = jnp.maximum(jnp.max(a_mat, axis=-1, keepdims=True), m_prev)
        w_intra = jnp.exp(a_mat - mu)
        w_inter = jnp.exp(m_prev - mu)
        sc = _dot_nt(q, k) * w_intra
        c_prev = c_ref[h]
        n_prev = n_ref[h, 0:1, :]
        num = _dot(sc.astype(BF16), v) + w_inter * _dot_nt(q, c_prev.astype(BF16))
        qn = jnp.sum(q.astype(F32) * n_prev, axis=-1, keepdims=True)
        den = jnp.sum(sc, axis=-1, keepdims=True) + w_inter * qn
        m_t = bc + mu
        hs = num / jnp.maximum(jnp.abs(den), jnp.exp(-m_t))
        gate = _sigmoid(o_ref[:, sl].astype(F32))
        y_ref[:, sl] = (hs * gate).astype(y_ref.dtype)

        mu_last = mu[L - 1:L, :]
        m_new = bc[L - 1:L, :] + mu_last
        decay = jnp.exp(m_prev - mu_last)
        wk_row = jnp.exp(a_row - mu_last)
        vt = v.astype(F32).T
        c_ref[h] = decay * c_prev + _dot((vt * wk_row).astype(BF16), k)
        wk8 = jnp.broadcast_to(wk_row, (SUBLANES, L)).astype(BF16)
        n_ref[h] = decay * n_ref[h] + _dot(wk8, k)
        m_ref[h] = jnp.broadcast_to(m_new, (SUBLANES, LANES))

    xbuf[0:halo, :] = xbuf[L:L + halo, :]


def _mlstm_call(proj, ifc, ift, conv_w, layer, ltri, utri, batch, seq):
    n = batch * seq
    L = ML_CHUNK
    nc = seq // L
    W = BRANCH_W
    return pl.pallas_call(
        _mlstm_kernel,
        grid=(batch, nc),
        in_specs=[
            pl.BlockSpec((L, 2 * W), lambda b, i: (b * nc + i, MLQK_COL // (2 * W))),
            pl.BlockSpec((L, W), lambda b, i: (b * nc + i, 4)),
            pl.BlockSpec((L, W), lambda b, i: (b * nc + i, 5)),
            pl.BlockSpec((L, LANES), lambda b, i: (b * nc + i, 0)),
            pl.BlockSpec((2 * SUBLANES, L), lambda b, i: (0, b * nc + i)),
            _layer_spec(layer, (CONV_W, 2 * W), lambda b, i: (0, 0)),
            pl.BlockSpec((L, L), lambda b, i: (0, 0)),
            pl.BlockSpec((L, L), lambda b, i: (0, 0)),
        ],
        out_specs=pl.BlockSpec((L, W), lambda b, i: (b * nc + i, 0)),
        out_shape=jax.ShapeDtypeStruct((n, W), BF16),
        scratch_shapes=[
            pltpu.VMEM((L + SUBLANES, 2 * W), F32),
            pltpu.VMEM((ML_HEADS, ML_HDIM, ML_HDIM), F32),
            pltpu.VMEM((ML_HEADS, SUBLANES, ML_HDIM), F32),
            pltpu.VMEM((ML_HEADS, SUBLANES, LANES), F32),
        ],
        compiler_params=_cparams("parallel", "arbitrary"),
        name="mlstm",
    )(proj, proj, proj, ifc, ift, conv_w, ltri, utri)


def _merge_kernel(ysb_ref, ymb_ref, yml_ref, gl_ref, x_ref, wb_ref, wo_ref, g2_ref, wr_ref, br_ref,
                  h_ref, xn_ref, route_ref, route_t_ref):
    half = D_MODEL // 2
    hres = x_ref[...]
    for c in range(2):
        cols = slice(c * half, (c + 1) * half)
        merged = jnp.zeros((ROW_TILE, half), F32)
        for b, y_ref in enumerate((ysb_ref, ymb_ref, yml_ref)):
            gate = _sigmoid(gl_ref[:, b * D_MODEL + c * half:b * D_MODEL + (c + 1) * half].astype(F32))
            merged = merged + gate * _dot(y_ref[...], wb_ref[b, :, cols])
        hres = hres + _dot(merged.astype(BF16), wo_ref[cols, :])
    h_ref[...] = hres

    ms = jnp.mean(hres * hres, axis=-1, keepdims=True)
    xn = (hres * lax.rsqrt(ms + RMS_EPS) * g2_ref[...]).astype(BF16)
    xn_ref[...] = xn

    logits = _dot(xn, wr_ref[...]) + br_ref[...]
    lane = lax.broadcasted_iota(jnp.int32, (ROW_TILE, LANES), 1).astype(F32)
    far = float(LANES)
    is_g = lane < N_GROUPS
    gl = jnp.where(is_g, logits, NEG_BIG)
    gmax = jnp.max(gl, axis=-1, keepdims=True)
    gsum = jnp.sum(jnp.where(is_g, jnp.exp(gl - gmax), 0.0), axis=-1, keepdims=True)
    g_w = 1.0 / gsum
    g_idx = jnp.min(jnp.where(is_g & (gl == gmax), lane, far), axis=-1, keepdims=True)
    e_lo = N_GROUPS + g_idx * EXPERTS_PER_GROUP
    in_grp = (lane >= e_lo) & (lane < e_lo + EXPERTS_PER_GROUP)
    el = jnp.where(in_grp, logits, NEG_BIG)
    l1 = jnp.max(el, axis=-1, keepdims=True)
    i1 = jnp.min(jnp.where(in_grp & (el == l1), lane, far), axis=-1, keepdims=True)
    el2 = jnp.where(lane == i1, NEG_BIG, el)
    l2 = jnp.max(el2, axis=-1, keepdims=True)
    i2 = jnp.min(jnp.where(in_grp & (el2 == l2), lane, far), axis=-1, keepdims=True)
    p2 = jnp.exp(l2 - l1)
    w1 = g_w / (1.0 + p2)
    w2 = g_w * p2 / (1.0 + p2)
    slot = lane + e_lo
    cw = jnp.where(slot == i1, w1, 0.0) + jnp.where(slot == i2, w2, 0.0)
    cw = jnp.where(lane < EXPERTS_PER_GROUP, cw, 0.0)
    route = jnp.where(lane == EXPERTS_PER_GROUP, g_idx, cw)
    route_ref[...] = route
    route_t_ref[...] = route.T[0:2 * SUBLANES, :]


def _merge_call(ysb, ymb, yml, p16, x, wb, wo, g2, wr, br, layer):
    n = x.shape[0]
    row = lambda i: (i, 0)
    const2 = lambda i: (0, 0)
    return pl.pallas_call(
        _merge_kernel,
        grid=(n // ROW_TILE,),
        in_specs=[
            pl.BlockSpec((ROW_TILE, BRANCH_W), row),
            pl.BlockSpec((ROW_TILE, BRANCH_W), row),
            pl.BlockSpec((ROW_TILE, BRANCH_W), row),
            pl.BlockSpec((ROW_TILE, N_BRANCH * D_MODEL), lambda i: (i, 1)),
            pl.BlockSpec((ROW_TILE, D_MODEL), row),
            _layer_spec(layer, (N_BRANCH, BRANCH_W, D_MODEL), lambda i: (0, 0, 0)),
            _layer_spec(layer, (D_MODEL, D_MODEL), const2),
            _layer_spec(layer, (1, D_MODEL), const2),
            _layer_spec(layer, (D_MODEL, LANES), const2),
            _layer_spec(layer, (1, LANES), const2),
        ],
        out_specs=[
            pl.BlockSpec((ROW_TILE, D_MODEL), row),
            pl.BlockSpec((ROW_TILE, D_MODEL), row),
            pl.BlockSpec((ROW_TILE, LANES), row),
            pl.BlockSpec((2 * SUBLANES, ROW_TILE), lambda i: (0, i)),
        ],
        out_shape=[
            jax.ShapeDtypeStruct((n, D_MODEL), F32),
            jax.ShapeDtypeStruct((n, D_MODEL), BF16),
            jax.ShapeDtypeStruct((n, LANES), F32),
            jax.ShapeDtypeStruct((2 * SUBLANES, n), F32),
        ],
        compiler_params=_cparams("parallel"),
        name="merge_outproj_router",
    )(ysb, ymb, yml, p16, x, wb, wo, g2, wr, br)


def _moe_kernel(x_ref, h_ref, rt_ref, rtt_ref, lx_ref, w1_ref, w3_ref, w2_ref, y_ref,
                keyc_ref, keyr_ref):
    tb = MOE_BLOCK
    sub = MOE_SUB
    grp = pl.program_id(1).astype(F32)
    gid_lane = EXPERTS_PER_GROUP
    gid_col = rt_ref[:, gid_lane:gid_lane + 1]
    route_t = rtt_ref[...]
    gid_row = route_t[gid_lane:gid_lane + 1, :]

    @pl.when(pl.program_id(1) == 0)
    def _():
        y_ref[...] = h_ref[...]
        lx = lx_ref[...]
        lane = lax.broadcasted_iota(jnp.int32, (tb, LANES), 1).astype(F32)
        own_col = lane == gid_col
        before_col = _dot(lx, jnp.where(own_col, 1.0, 0.0).astype(BF16))
        pos_col = jnp.sum(jnp.where(own_col, before_col, 0.0), axis=-1, keepdims=True)
        keyc_ref[...] = jnp.broadcast_to(pos_col, (tb, LANES))
        row = lax.broadcasted_iota(jnp.int32, (SUBLANES, tb), 0).astype(F32)
        own_row = row == gid_row
        before_row = _dot_nt(jnp.where(own_row, 1.0, 0.0).astype(BF16), lx)
        pos_row = jnp.sum(jnp.where(own_row, before_row, 0.0), axis=0, keepdims=True)
        keyr_ref[...] = jnp.broadcast_to(pos_row, (SUBLANES, tb))

    in_col = gid_col == grp
    in_row = gid_row == grp
    key_col = jnp.where(in_col, keyc_ref[...], -1.0)
    key_col = jnp.concatenate([key_col] * (sub // LANES), axis=1)
    key_row = jnp.where(in_row, keyr_ref[0:1, :], -1.0)
    count = jnp.sum(jnp.where(in_row, 1.0, 0.0)).astype(jnp.int32)
    rem = count % sub
    short_tail = (rem > 0) & (rem <= sub // 2)
    n_sub = count // sub + jnp.where(rem > sub // 2, 1, 0)
    t_hi, t_mid, t_lo = _split3(route_t)

    def experts_of_sub_tile(base, rows):
        slot_rows = lax.broadcasted_iota(jnp.int32, (rows, tb), 0).astype(F32) + base
        pick = jnp.where(key_row == slot_rows, 1.0, 0.0).astype(BF16)
        xs = _dot(pick, x_ref[...]).astype(BF16)
        cw_t = _dot_nt(t_hi, pick) + _dot_nt(t_mid, pick) + _dot_nt(t_lo, pick)
        cw = jnp.concatenate(
            [cw_t, jnp.zeros((LANES - 2 * SUBLANES, rows), F32)], axis=0).T
        experts = range(EXPERTS_PER_GROUP)
        ups = [(_dot(xs, w1_ref[e]), _dot(xs, w3_ref[e])) for e in experts]
        hid = jnp.concatenate(
            [(a * _sigmoid(a) * b * cw[:, e:e + 1]).astype(BF16) for e, (a, b) in enumerate(ups)],
            axis=1)
        y = _dot(hid, w2_ref[...].reshape(EXPERTS_PER_GROUP * D_EXPERT, D_MODEL))
        return y.astype(BF16)

    def sub_tile(base, rows):
        y = experts_of_sub_tile(base, rows)
        slot_cols = lax.broadcasted_iota(jnp.int32, (tb, rows), 1).astype(F32) + base
        put = jnp.where(key_col[:, :rows] == slot_cols, 1.0, 0.0).astype(BF16)
        y_ref[...] += _dot(put, y)

    def full_tile(s, c):
        sub_tile((s * sub).astype(F32), sub)
        return c

    lax.fori_loop(0, n_sub, full_tile, 0)

    @pl.when(short_tail)
    def _():
        sub_tile((n_sub * sub).astype(F32), sub // 2)


def _moe_call(xn2, h, route, route_t, lx, w1b, w3b, w2b, layer):
    n = xn2.shape[0]
    tb = MOE_BLOCK
    e = EXPERTS_PER_GROUP
    return pl.pallas_call(
        _moe_kernel,
        grid=(n // tb, N_GROUPS),
        in_specs=[
            pl.BlockSpec((tb, D_MODEL), lambda i, g: (i, 0)),
            pl.BlockSpec((tb, D_MODEL), lambda i, g: (i, 0)),
            pl.BlockSpec((tb, LANES), lambda i, g: (i, 0)),
            pl.BlockSpec((2 * SUBLANES, tb), lambda i, g: (0, i)),
            pl.BlockSpec((tb, tb), lambda i, g: (0, 0)),
            _layer_spec(layer, (e, D_MODEL, D_EXPERT), lambda i, g: (g, 0, 0)),
            _layer_spec(layer, (e, D_MODEL, D_EXPERT), lambda i, g: (g, 0, 0)),
            _layer_spec(layer, (e, D_EXPERT, D_MODEL), lambda i, g: (g, 0, 0)),
        ],
        out_specs=pl.BlockSpec((tb, D_MODEL), lambda i, g: (i, 0)),
        out_shape=jax.ShapeDtypeStruct((n, D_MODEL), F32),
        scratch_shapes=[
            pltpu.VMEM((tb, LANES), F32),
            pltpu.VMEM((SUBLANES, tb), F32),
        ],
        compiler_params=_cparams("parallel", "arbitrary"),
        name="moe_group_experts",
    )(xn2, h, route, route_t, lx, w1b, w3b, w2b)


def _tables(positions):
    t = ATT_TILE
    r = jnp.arange(t)
    u_tri = (r[:, None] >= r[None, :]).astype(BF16)
    eye = jnp.eye(t, dtype=BF16)
    L = ML_CHUNK
    rl = jnp.arange(L)
    ltri = (rl[:, None] >= rl[None, :]).astype(BF16)
    utri = ltri.T

    li = jnp.arange(LANES)
    seg = ((li[:, None] // HEAD_DIM) == (li[None, :] // HEAD_DIM)).astype(BF16)
    half = ROPE_DIM // 2
    src, dst = li[:, None], li[None, :]
    d_in = dst % HEAD_DIM
    rot = jnp.where((d_in < half) & (src == dst + half), -1.0,
                    jnp.where((d_in >= half) & (d_in < ROPE_DIM) & (src == dst - half), 1.0, 0.0)
                    ).astype(BF16)

    inv_freq = jnp.power(jnp.float32(ROPE_THETA), -jnp.arange(half, dtype=F32) / half)
    lane_in = li % HEAD_DIM
    ang = positions.reshape(-1).astype(F32)[:, None] * inv_freq[lane_in % half][None, :]
    rotated = (lane_in < ROPE_DIM)[None, :]
    cos_t = jnp.where(rotated, jnp.cos(ang), 1.0)
    sin_t = jnp.where(rotated, jnp.sin(ang), 0.0)
    rb = jnp.arange(MOE_BLOCK)
    lx = (rb[None, :] < rb[:, None]).astype(BF16)
    return dict(u_tri=u_tri, eye=eye, ltri=ltri, utri=utri, seg=seg, rot=rot, cos=cos_t, sin=sin_t,
                lx=lx)


def kernel(x, positions, norm1_g, w_in, qn_g, kn_g, conv_w, ml_gate_b, w_branch, w_out, norm2_g,
           w_rg, b_rg, w_re, b_re, w1, w3, w2):
    batch, seq, d = x.shape
    n = batch * seq
    W = BRANCH_W
    depth = w_in.shape[0]
    tb = _tables(positions)
    xf = x.reshape(n, d)

    n_if = 2 * ML_HEADS
    n_route = N_GROUPS + N_EXPERTS
    w_proj = jnp.concatenate(
        [w_in[:, :, 0:3 * W], w_in[:, :, 5 * W:6 * W], w_in[:, :, 8 * W:10 * W],
         w_in[:, :, 10 * W + n_if:], w_in[:, :, 3 * W:5 * W], w_in[:, :, 6 * W:8 * W]],
        axis=2).astype(BF16)
    wif = jnp.pad(w_in[:, :, 10 * W:10 * W + n_if],
                  ((0, 0), (0, 0), (0, LANES - n_if))).astype(BF16)
    bif = jnp.pad(ml_gate_b, ((0, 0), (0, LANES - n_if))).reshape(depth, 1, LANES)
    g1 = norm1_g.reshape(depth, 1, d)
    g2 = norm2_g.reshape(depth, 1, d)
    qg = jnp.tile(qn_g, (1, LANES // HEAD_DIM)).reshape(depth, 1, LANES)
    kg = jnp.tile(kn_g, (1, LANES // HEAD_DIM)).reshape(depth, 1, LANES)
    wr = jnp.pad(jnp.concatenate([w_rg, w_re], axis=2),
                 ((0, 0), (0, 0), (0, LANES - n_route))).astype(BF16)
    br = jnp.pad(jnp.concatenate([b_rg, b_re], axis=1),
                 ((0, 0), (0, LANES - n_route))).reshape(depth, 1, LANES)
    wb = w_branch.astype(BF16)
    wo = w_out.astype(BF16)
    w1b, w3b, w2b = w1.astype(BF16), w3.astype(BF16), w2.astype(BF16)

    for l in range(depth):
        p16, ifc, ift = _norm_proj_call(xf, g1, wif, bif, w_proj, l)

        y_sb = _sb_call(p16, tb["u_tri"], batch, seq)

        qa, ka, km = _moba_prep_call(p16, tb["cos"], tb["sin"], qg, kg, l, tb["seg"], tb["rot"],
                                     batch, seq)
        qb = _moba_gate_call(qa, km.transpose(0, 2, 1, 3), tb["eye"], batch, seq)
        y_mb = _moba_call(qb, ka, p16, batch, seq)

        y_ml = _mlstm_call(p16, ifc, ift, conv_w, l, tb["ltri"], tb["utri"], batch, seq)

        hres, xn2, route, route_t = _merge_call(y_sb, y_mb, y_ml, p16, xf, wb, wo, g2, wr, br, l)
        xf = _moe_call(xn2, hres, route, route_t, tb["lx"], w1b, w3b, w2b, l)

    return xf.reshape(batch, seq, d)
```
